```python
import math
import jax, jax.numpy as jnp
from jax import lax
import numpy as np

D_MODEL = 1024
BATCH = 8
SEQ = 4096
DEPTH = 4

GLA_HEADS = 4
GLA_DK = 64
GLA_DV = 128
GLA_RANK = 16
GLA_TAU = 16.0
GLA_CHUNK = 64
DIFF_HEADS = 4
DIFF_DH = 64
DIFF_DV = 2 * DIFF_DH
Q_BLOCK = 128
A_QK = GLA_HEADS * GLA_DK
A_V = GLA_HEADS * GLA_DV
B_QK = DIFF_HEADS * 2 * DIFF_DH
B_V = DIFF_HEADS * DIFF_DV
EVEN_IN = 2 * A_QK + 2 * A_V + GLA_RANK + 2 * B_QK + B_V
EVEN_MIX = A_V + B_V
DIL_PATTERNS = ((128, 1), (512, 4), (2048, 16))
DIL_GROUPS = 3
DIL_HEADS = 4
DIL_DH = 128
ODD_IN = DIL_GROUPS * 3 * DIL_HEADS * DIL_DH
ODD_MIX = DIL_HEADS * DIL_DH
D_FF = 2816
CONV_W = 3
EPS = 1e-6
N_EVEN = (DEPTH + 1) // 2
N_ODD = DEPTH // 2

kernel_name = "hybrid_gla_diffattn_dilated_convffn"


def rmsnorm(x, g):
    xf = x.astype(jnp.float32)
    y = xf * lax.rsqrt(jnp.mean(xf * xf, axis=-1, keepdims=True) + EPS)
    return (y * g.astype(jnp.float32)).astype(x.dtype)


def gla(q, k, v, log_a):
    B, S, H, dk = q.shape
    dv = v.shape[-1]
    C = GLA_CHUNK
    N = S // C
    f32 = jnp.float32

    def chunk(t):
        return t.astype(f32).reshape(B, N, C, H, t.shape[-1]).transpose(0, 3, 1, 2, 4)

    qc, kc, vc, lc = chunk(q * (dk ** -0.5)), chunk(k), chunk(v), chunk(log_a)
    b = jnp.cumsum(lc, axis=3)
    b_last = b[:, :, :, -1:, :]
    q_d = qc * jnp.exp(b)
    k_d = kc * jnp.exp(-b)
    causal = jnp.tril(jnp.ones((C, C), dtype=bool))
    att = jnp.where(causal, jnp.einsum('bhnik,bhnjk->bhnij', q_d, k_d), 0.0)
    o_intra = jnp.einsum('bhnij,bhnjv->bhniv', att, vc)
    kv = jnp.einsum('bhnjk,bhnjv->bhnkv', kc * jnp.exp(b_last - b), vc)
    decay = jnp.exp(b_last[:, :, :, 0, :])

    def step(state, inp):
        kv_n, dec_n = inp
        return dec_n[..., None] * state + kv_n, state

    s0 = jnp.zeros((B, H, dk, dv), f32)
    _, states = lax.scan(step, s0, (jnp.moveaxis(kv, 2, 0), jnp.moveaxis(decay, 2, 0)))
    states = jnp.moveaxis(states, 0, 2)
    o = o_intra + jnp.einsum('bhnik,bhnkv->bhniv', q_d, states)
    return o.transpose(0, 2, 3, 1, 4).reshape(B, S, H, dv).astype(v.dtype)


def diff_attention(q, k, v, lam):
    B, S, H, _, dh = q.shape
    dv = v.shape[-1]
    nb = S // Q_BLOCK
    qb = (q * (dh ** -0.5)).reshape(B, nb, Q_BLOCK, H, 2, dh).transpose(1, 0, 3, 4, 2, 5)
    kt = k.transpose(0, 2, 3, 1, 4)
    vt = v.transpose(0, 2, 1, 3)
    kpos = jnp.arange(S)

    def block(args):
        qblk, n = args
        s = jnp.einsum('bhcqd,bhckd->bhcqk', qblk, kt).astype(jnp.float32)
        qpos = n * Q_BLOCK + jnp.arange(Q_BLOCK)
        s = jnp.where(kpos[None, :] <= qpos[:, None], s, -jnp.inf)
        p = jax.nn.softmax(s, axis=-1)
        w = p[:, :, 0] - lam * p[:, :, 1]
        return jnp.einsum('bhqk,bhkv->bhqv', w.astype(vt.dtype), vt)

    o = lax.map(block, (qb, jnp.arange(nb)))
    return o.transpose(1, 0, 3, 2, 4).reshape(B, S, H, dv)


def dilated_branch(q, k, v, dilation, wc):
    B, S, H, dh = q.shape
    L = S // dilation
    nb = -(-L // wc)
    Lp = nb * wc

    def to_class(t):
        t = t.reshape(B, L, dilation, H, dh).transpose(0, 2, 3, 1, 4)
        t = jnp.pad(t, ((0, 0), (0, 0), (0, 0), (0, Lp - L), (0, 0)))
        return t.reshape(B, dilation, H, nb, wc, dh)

    def band(t):
        prev = jnp.pad(t, ((0, 0),) * 3 + ((1, 0), (0, 0), (0, 0)))[:, :, :, :-1]
        return jnp.concatenate([prev, t], axis=-2)

    qc = to_class(q)
    kb = band(to_class(k))
    vb = band(to_class(v))
    s = jnp.einsum('brhnqd,brhnkd->brhnqk', qc, kb).astype(jnp.float32) * (dh ** -0.5)
    i = jnp.arange(wc)[:, None]
    j = jnp.arange(2 * wc)[None, :]
    dist = i + wc - j
    blk = jnp.arange(nb)[:, None, None]
    valid = (dist >= 0) & (dist <= wc) & ((blk > 0) | (j >= wc))
    s = jnp.where(valid, s, -jnp.inf)
    lse = jax.nn.logsumexp(s, axis=-1)
    p = jnp.exp(s - lse[..., None])
    o = jnp.einsum('brhnqk,brhnkd->brhnqd', p.astype(vb.dtype), vb)
    o = o.reshape(B, dilation, H, Lp, dh)[:, :, :, :L].transpose(0, 3, 1, 2, 4).reshape(B, S, H, dh)
    lse = lse.reshape(B, dilation, H, Lp)[..., :L].transpose(0, 3, 1, 2).reshape(B, S, H)
    return o, lse


def even_mixer(h, w_in, w_a2, b_a, gla_gain, dq_gain, dk_gain, lq1, lk1, lq2, lk2, diff_gain, w_out, lambda_init):
    B, S, _ = h.shape
    z = h @ w_in
    sizes = [A_QK, A_QK, A_V, A_V, GLA_RANK, B_QK, B_QK]
    aq, ak, av, ag, ar, bq, bk, bv = jnp.split(z, [int(c) for c in np.cumsum(sizes)], axis=-1)
    log_a = jax.nn.log_sigmoid((ar @ w_a2 + b_a).astype(jnp.float32)) / GLA_TAU
    oa = gla(aq.reshape(B, S, GLA_HEADS, GLA_DK), ak.reshape(B, S, GLA_HEADS, GLA_DK),
             av.reshape(B, S, GLA_HEADS, GLA_DV), log_a.reshape(B, S, GLA_HEADS, GLA_DK))
    oa = rmsnorm(oa, gla_gain) * jax.nn.silu(ag.reshape(B, S, GLA_HEADS, GLA_DV))
    bq = rmsnorm(bq.reshape(B, S, DIFF_HEADS, 2, DIFF_DH), dq_gain)
    bk = rmsnorm(bk.reshape(B, S, DIFF_HEADS, 2, DIFF_DH), dk_gain)
    f32 = jnp.float32
    lam = (jnp.exp(jnp.sum(lq1.astype(f32) * lk1.astype(f32)))
           - jnp.exp(jnp.sum(lq2.astype(f32) * lk2.astype(f32))) + lambda_init)
    ob = diff_attention(bq, bk, bv.reshape(B, S, DIFF_HEADS, DIFF_DV), lam)
    ob = rmsnorm(ob, diff_gain) * (1.0 - lambda_init)
    o = jnp.concatenate([oa.reshape(B, S, A_V), ob.reshape(B, S, B_V)], axis=-1)
    return o @ w_out


def odd_mixer(h, w_in, q_gain, k_gain, w_out):
    B, S, _ = h.shape
    z = (h @ w_in).reshape(B, S, DIL_GROUPS, 3, DIL_HEADS, DIL_DH)
    q = rmsnorm(z[:, :, :, 0], q_gain)
    k = rmsnorm(z[:, :, :, 1], k_gain)
    v = z[:, :, :, 2]
    outs, lses = [], []
    for g, (window, dilation) in enumerate(DIL_PATTERNS):
        o_g, lse_g = dilated_branch(q[:, :, g], k[:, :, g], v[:, :, g], dilation, window // dilation)
        outs.append(o_g)
        lses.append(lse_g)
    wts = jax.nn.softmax(jnp.stack(lses, axis=0), axis=0)
    o = jnp.sum(wts[..., None].astype(v.dtype) * jnp.stack(outs, axis=0), axis=0)
    return o.reshape(B, S, ODD_MIX) @ w_out


def conv_ffn(h, w_up, conv_w, conv_b, w_down):
    u = h @ w_up
    u = lax.conv_general_dilated(u, conv_w, window_strides=(1,), padding=[(CONV_W - 1, 0)],
                                 dimension_numbers=('NWC', 'WIO', 'NWC'),
                                 feature_group_count=2 * D_FF) + conv_b
    gate, val = jnp.split(u, 2, axis=-1)
    return (jax.nn.silu(gate) * val) @ w_down


def setup_inputs(seed: int = 0) -> dict:
    key = jax.random.key(seed)
    ks = iter(jax.random.split(key, 32))
    nrm = lambda shape, scale: jax.random.normal(next(ks), shape, jnp.float32) * scale
    gain = lambda shape: 1.0 + nrm(shape, 0.02)
    return {
        "x": nrm((BATCH, SEQ, D_MODEL), 1.0),
        "norm_mix": gain((DEPTH, D_MODEL)),
        "norm_ffn": gain((DEPTH, D_MODEL)),
        "ev_w_in": nrm((N_EVEN, D_MODEL, EVEN_IN), D_MODEL ** -0.5),
        "ev_w_a2": nrm((N_EVEN, GLA_RANK, A_QK), GLA_RANK ** -0.5),
        "ev_b_a": nrm((N_EVEN, A_QK), 0.1),
        "ev_gla_gain": gain((N_EVEN, GLA_DV)),
        "ev_dq_gain": gain((N_EVEN, DIFF_DH)),
        "ev_dk_gain": gain((N_EVEN, DIFF_DH)),
        "ev_lq1": nrm((N_EVEN, DIFF_DH), 0.1),
        "ev_lk1": nrm((N_EVEN, DIFF_DH), 0.1),
        "ev_lq2": nrm((N_EVEN, DIFF_DH), 0.1),
        "ev_lk2": nrm((N_EVEN, DIFF_DH), 0.1),
        "ev_diff_gain": gain((N_EVEN, DIFF_DV)),
        "ev_w_out": nrm((N_EVEN, EVEN_MIX, D_MODEL), EVEN_MIX ** -0.5),
        "od_w_in": nrm((N_ODD, D_MODEL, ODD_IN), D_MODEL ** -0.5),
        "od_q_gain": gain((N_ODD, DIL_DH)),
        "od_k_gain": gain((N_ODD, DIL_DH)),
        "od_w_out": nrm((N_ODD, ODD_MIX, D_MODEL), ODD_MIX ** -0.5),
        "ffn_w_up": nrm((DEPTH, D_MODEL, 2 * D_FF), D_MODEL ** -0.5),
        "ffn_conv_w": nrm((DEPTH, CONV_W, 1, 2 * D_FF), CONV_W ** -0.5),
        "ffn_conv_b": nrm((DEPTH, 2 * D_FF), 0.01),
        "ffn_w_down": nrm((DEPTH, D_FF, D_MODEL), D_FF ** -0.5),
    }


def reference(x, norm_mix, norm_ffn, ev_w_in, ev_w_a2, ev_b_a, ev_gla_gain, ev_dq_gain, ev_dk_gain,
              ev_lq1, ev_lk1, ev_lq2, ev_lk2, ev_diff_gain, ev_w_out, od_w_in, od_q_gain, od_k_gain,
              od_w_out, ffn_w_up, ffn_conv_w, ffn_conv_b, ffn_w_down):
    for i in range(DEPTH):
        h = rmsnorm(x, norm_mix[i])
        if i % 2 == 0:
            e = i // 2
            lambda_init = 0.8 - 0.6 * math.exp(-0.3 * i)
            x = x + even_mixer(h, ev_w_in[e], ev_w_a2[e], ev_b_a[e], ev_gla_gain[e], ev_dq_gain[e],
                               ev_dk_gain[e], ev_lq1[e], ev_lk1[e], ev_lq2[e], ev_lk2[e],
                               ev_diff_gain[e], ev_w_out[e], lambda_init)
        else:
            o = i // 2
            x = x + odd_mixer(h, od_w_in[o], od_q_gain[o], od_k_gain[o], od_w_out[o])
        x = x + conv_ffn(rmsnorm(x, norm_ffn[i]), ffn_w_up[i], ffn_conv_w[i], ffn_conv_b[i], ffn_w_down[i])
    return x
```

```python
import functools
import math

import jax
import jax.numpy as jnp
from jax import lax
from jax.experimental import pallas as pl
from jax.experimental.pallas import tpu as pltpu

F32 = jnp.float32
BF16 = jnp.bfloat16

EPS = 1e-6
LANES = 128
VMEM_LIMIT = 56 * 1024 * 1024
NEG_BIG = -1e30

D_MODEL = 1024
GLA_HEADS, GLA_DK, GLA_DV, GLA_RANK, GLA_TAU, GLA_CHUNK = 4, 64, 128, 16, 16.0, 64
DIFF_HEADS, DIFF_DH, DIFF_DV = 4, 64, 128
A_QK, A_V, B_QK, B_V = 256, 512, 512, 512
DIL_PATTERNS = ((128, 1), (512, 4), (2048, 16))
DIL_HEADS, DIL_DH = 4, 128
ODD_MIX = DIL_HEADS * DIL_DH
ODD_IN = 3 * 3 * ODD_MIX
D_FF = 2816

EV_AQ, EV_AK, EV_AV, EV_AG, EV_BQ, EV_BK, EV_BV, EV_AR = 0, 256, 512, 1024, 1536, 2048, 2560, 3072
EV_COLS = 3200

ROW_TILE = 512


def _const_spec(shape):
    nd = len(shape)
    return pl.BlockSpec(shape, lambda *_: (0,) * nd)


def _params(sem):
    return pltpu.CompilerParams(dimension_semantics=sem, vmem_limit_bytes=VMEM_LIMIT)


def _rms_rows(x, g):
    return x * lax.rsqrt(jnp.mean(x * x, axis=-1, keepdims=True) + EPS) * g


def _norm_proj_kernel(x_ref, g_ref, w_ref, hg_ref, o_ref, *, norm_group):
    hb = _rms_rows(x_ref[...], g_ref[...]).astype(BF16)
    n_cols = o_ref.shape[1]
    for c in range(n_cols // LANES):
        cs = slice(c * LANES, (c + 1) * LANES)
        z = jnp.dot(hb, w_ref[:, cs], preferred_element_type=F32)
        grp = norm_group[c]
        if grp == LANES:
            ss = jnp.sum(z * z, axis=-1, keepdims=True)
            z = z * lax.rsqrt(ss * (1.0 / LANES) + EPS) * hg_ref[:, cs]
        elif grp == LANES // 2:
            lo = lax.broadcasted_iota(jnp.int32, z.shape, 1) < grp
            sq = z * z
            s_lo = jnp.sum(jnp.where(lo, sq, 0.0), axis=-1, keepdims=True)
            s_hi = jnp.sum(jnp.where(lo, 0.0, sq), axis=-1, keepdims=True)
            ss = jnp.where(lo, s_lo, s_hi)
            z = z * lax.rsqrt(ss * (1.0 / grp) + EPS) * hg_ref[:, cs]
        o_ref[:, cs] = z.astype(o_ref.dtype)


def _norm_proj(x2, g, w, hg, norm_group):
    t, d = x2.shape
    n = w.shape[1]
    return pl.pallas_call(
        functools.partial(_norm_proj_kernel, norm_group=norm_group),
        grid=(t // ROW_TILE,),
        in_specs=[pl.BlockSpec((ROW_TILE, d), lambda i: (i, 0)),
                  _const_spec((1, d)), _const_spec((d, n)), _const_spec((1, n))],
        out_specs=pl.BlockSpec((ROW_TILE, n), lambda i: (i, 0)),
        out_shape=jax.ShapeDtypeStruct((t, n), BF16),
        compiler_params=_params(("parallel",)),
        name="norm_proj",
    )(x2, g, w, hg)


GLA_TILE = 512


def _gla_kernel(q_ref, k_ref, v_ref, gate_ref, ar_ref, wa2_ref, ba_ref, gain_ref, o_ref,
                la_ref, acc_ref, state_ref):
    C = GLA_CHUNK

    @pl.when(pl.program_id(1) == 0)
    def _():
        state_ref[...] = jnp.zeros_like(state_ref)

    logits = jnp.dot(ar_ref[0], wa2_ref[...], preferred_element_type=F32) + ba_ref[...]
    log_sig = jnp.minimum(logits, 0.0) - jnp.log(1.0 + jnp.exp(-jnp.abs(logits)))
    la_ref[...] = log_sig * (1.0 / GLA_TAU)

    row = lax.broadcasted_iota(jnp.int32, (C, C), 0)
    col = lax.broadcasted_iota(jnp.int32, (C, C), 1)
    causal = col <= row
    tril = causal.astype(F32)

    def chunk(c, carry):
        r0 = pl.multiple_of(c * C, C)
        rows = pl.ds(r0, C)
        b = jnp.dot(tril, la_ref[rows, :], preferred_element_type=F32,
                    precision=lax.Precision.HIGHEST)
        b_last = b[C - 1:C, :]
        q_d = (q_ref[0, rows, :].astype(F32) * (GLA_DK ** -0.5) * jnp.exp(b)).astype(BF16)
        k_f = k_ref[0, rows, :].astype(F32)
        k_d = (k_f * jnp.exp(-b)).astype(BF16)
        k_t = (k_f * jnp.exp(b_last - b)).astype(BF16)
        dec = jnp.exp(b_last)
        v_f = v_ref[0, rows, :].astype(F32)
        for h in range(GLA_HEADS):
            ks = slice(h * GLA_DK, (h + 1) * GLA_DK)
            vs = slice(h * GLA_DV, (h + 1) * GLA_DV)
            att = lax.dot_general(q_d[:, ks], k_d[:, ks], (((1,), (1,)), ((), ())),
                                  preferred_element_type=F32)
            att = jnp.where(causal, att, 0.0).astype(BF16)
            v_h = v_f[:, vs]
            o_h = jnp.dot(att, v_h.astype(BF16), preferred_element_type=F32)
            st = state_ref[h]
            o_h = o_h + lax.dot_general(q_d[:, ks], st.astype(BF16), (((1,), (1,)), ((), ())),
                                        preferred_element_type=F32)
            kv_t = jnp.dot(v_h.T.astype(BF16), k_t[:, ks], preferred_element_type=F32)
            state_ref[h] = st * dec[:, ks] + kv_t
            acc_ref[rows, vs] = o_h
        return carry

    lax.fori_loop(0, q_ref.shape[1] // C, chunk, 0)

    gate = gate_ref[0].astype(F32)
    for h in range(GLA_HEADS):
        vs = slice(h * GLA_DV, (h + 1) * GLA_DV)
        o = _rms_rows(acc_ref[:, vs], gain_ref[...])
        g = gate[:, vs]
        o_ref[0, :, vs] = (o * (g / (1.0 + jnp.exp(-g)))).astype(o_ref.dtype)


def _gla(z3, wa2p, ba, gain):
    b, s, _ = z3.shape
    tq = min(GLA_TILE, s)

    def col(width, start):
        return pl.BlockSpec((1, tq, width), lambda bi, i: (bi, i, start // width))

    return pl.pallas_call(
        _gla_kernel,
        grid=(b, s // tq),
        in_specs=[col(A_QK, EV_AQ), col(A_QK, EV_AK), col(A_V, EV_AV), col(A_V, EV_AG),
                  col(LANES, EV_AR), _const_spec((LANES, A_QK)), _const_spec((1, A_QK)),
                  _const_spec((1, GLA_DV))],
        out_specs=pl.BlockSpec((1, tq, A_V), lambda bi, i: (bi, i, 0)),
        out_shape=jax.ShapeDtypeStruct((b, s, A_V), BF16),
        scratch_shapes=[pltpu.VMEM((tq, A_QK), F32), pltpu.VMEM((tq, A_V), F32),
                        pltpu.VMEM((GLA_HEADS, GLA_DV, GLA_DK), F32)],
        compiler_params=_params(("parallel", "arbitrary")),
        name="gla",
    )(z3, z3, z3, z3, z3, wa2p, ba, gain)


DIFF_TILE = 512


def _diff_kernel(qi_ref, ki_ref, q_ref, k_ref, v_ref, lq1_ref, lk1_ref, lq2_ref, lk2_ref, gain_ref,
                 o_ref, qv_ref, m_ref, l_ref, acc_ref, *, lambda_init):
    t = pl.program_id(2)
    qi = qi_ref[t]
    ki = ki_ref[t]
    tq = q_ref.shape[1]
    tk = k_ref.shape[1]

    @pl.when(ki == 0)
    def _():
        q = q_ref[0]
        first = lax.broadcasted_iota(jnp.int32, q.shape, 1) < DIFF_DH
        zero = jnp.zeros_like(q)
        qv_ref[0:tq, :] = jnp.where(first, q, zero)
        qv_ref[tq:2 * tq, :] = jnp.where(first, zero, q)
        m_ref[...] = jnp.full_like(m_ref, NEG_BIG)
        l_ref[...] = jnp.zeros_like(l_ref)
        acc_ref[...] = jnp.zeros_like(acc_ref)

    def update(masked):
        s = lax.dot_general(qv_ref[...], k_ref[0], (((1,), (1,)), ((), ())),
                            preferred_element_type=F32)
        if masked:
            r = lax.broadcasted_iota(jnp.int32, (tq, tk), 0)
            c = lax.broadcasted_iota(jnp.int32, (tq, tk), 1)
            keep = jnp.concatenate([c <= r, c <= r], axis=0)
            s = jnp.where(keep, s, NEG_BIG)
        m_prev = m_ref[:, 0:1]
        m_new = jnp.maximum(m_prev, jnp.max(s, axis=-1, keepdims=True))
        alpha = jnp.exp(m_prev - m_new)
        p = jnp.exp(s - m_new)
        l_ref[...] = jnp.broadcast_to(alpha * l_ref[:, 0:1] + jnp.sum(p, axis=-1, keepdims=True),
                                      l_ref.shape)
        acc_ref[...] = alpha * acc_ref[...] + jnp.dot(p.astype(BF16), v_ref[0],
                                                      preferred_element_type=F32)
        m_ref[...] = jnp.broadcast_to(m_new, m_ref.shape)

    @pl.when(ki < qi)
    def _():
        update(False)

    @pl.when(ki == qi)
    def _():
        update(True)
        lam = (jnp.exp(jnp.sum(lq1_ref[...] * lk1_ref[...], axis=-1, keepdims=True))
               - jnp.exp(jnp.sum(lq2_ref[...] * lk2_ref[...], axis=-1, keepdims=True))
               + lambda_init)
        o1 = acc_ref[0:tq, :] / l_ref[0:tq, 0:1]
        o2 = acc_ref[tq:2 * tq, :] / l_ref[tq:2 * tq, 0:1]
        o = _rms_rows(o1 - lam * o2, gain_ref[...]) * (1.0 - lambda_init)
        o_ref[0] = o.astype(o_ref.dtype)


def _diff_attention(z3, lq1, lk1, lq2, lk2, gain, lambda_init):
    b, s, _ = z3.shape
    tq = min(DIFF_TILE, s)
    nq = s // tq
    pairs = [(qi, ki) for qi in range(nq) for ki in range(qi + 1)]
    qi_tab = jnp.asarray([p[0] for p in pairs], jnp.int32)
    ki_tab = jnp.asarray([p[1] for p in pairs], jnp.int32)
    qc, kc, vc = EV_BQ // LANES, EV_BK // LANES, EV_BV // LANES

    grid_spec = pltpu.PrefetchScalarGridSpec(
        num_scalar_prefetch=2,
        grid=(b, DIFF_HEADS, len(pairs)),
        in_specs=[pl.BlockSpec((1, tq, LANES), lambda bi, h, t, qt, kt: (bi, qt[t], qc + h)),
                  pl.BlockSpec((1, tq, LANES), lambda bi, h, t, qt, kt: (bi, kt[t], kc + h)),
                  pl.BlockSpec((1, tq, LANES), lambda bi, h, t, qt, kt: (bi, kt[t], vc + h)),
                  _const_spec((1, DIFF_DH)), _const_spec((1, DIFF_DH)),
                  _const_spec((1, DIFF_DH)), _const_spec((1, DIFF_DH)),
                  _const_spec((1, DIFF_DV))],
        out_specs=pl.BlockSpec((1, tq, LANES), lambda bi, h, t, qt, kt: (bi, qt[t], h)),
        scratch_shapes=[pltpu.VMEM((2 * tq, LANES), BF16), pltpu.VMEM((2 * tq, LANES), F32),
                        pltpu.VMEM((2 * tq, LANES), F32), pltpu.VMEM((2 * tq, DIFF_DV), F32)],
    )
    return pl.pallas_call(
        functools.partial(_diff_kernel, lambda_init=lambda_init),
        grid_spec=grid_spec,
        out_shape=jax.ShapeDtypeStruct((b, s, B_V), BF16),
        compiler_params=_params(("parallel", "parallel", "arbitrary")),
        name="diff_attention",
    )(qi_tab, ki_tab, z3, z3, z3, lq1, lk1, lq2, lk2, gain)


def _proj2_res_kernel(a_ref, b_ref, w_ref, x_ref, o_ref):
    ka = a_ref.shape[1]
    o_ref[...] = (x_ref[...]
                  + jnp.dot(a_ref[...], w_ref[0:ka, :], preferred_element_type=F32)
                  + jnp.dot(b_ref[...], w_ref[ka:, :], preferred_element_type=F32))


def _proj2_res(a, b, w, x2):
    t, d = x2.shape
    row = lambda width: pl.BlockSpec((ROW_TILE, width), lambda i: (i, 0))
    return pl.pallas_call(
        _proj2_res_kernel,
        grid=(t // ROW_TILE,),
        in_specs=[row(a.shape[1]), row(b.shape[1]), _const_spec(w.shape), row(d)],
        out_specs=row(d),
        out_shape=jax.ShapeDtypeStruct((t, d), F32),
        compiler_params=_params(("parallel",)),
        name="proj2_res",
    )(a, b, w, x2)


DIL_WC = 128


def _dil_kernel(q_ref, kp_ref, kc_ref, vp_ref, vc_ref, o_ref, lse_ref):
    n = pl.program_id(2)
    rows = q_ref.shape[1]
    wc = DIL_WC
    i = lax.broadcasted_iota(jnp.int32, (wc, 2 * wc), 0)
    j = lax.broadcasted_iota(jnp.int32, (wc, 2 * wc), 1)
    dist = i + wc - j
    band = (dist >= 0) & (dist <= wc)
    band_first = band & ((j >= wc) | (n > 0))
    for jb in range(rows // wc):
        rs = slice(jb * wc, (jb + 1) * wc)
        for h in range(DIL_HEADS):
            hs = slice(h * DIL_DH, (h + 1) * DIL_DH)
            if jb == 0:
                k = jnp.concatenate([kp_ref[0, :, hs], kc_ref[0, 0:wc, hs]], axis=0)
                v = jnp.concatenate([vp_ref[0, :, hs], vc_ref[0, 0:wc, hs]], axis=0)
                valid = band_first
            else:
                k = kc_ref[0, (jb - 1) * wc:(jb + 1) * wc, hs]
                v = vc_ref[0, (jb - 1) * wc:(jb + 1) * wc, hs]
                valid = band
            s = lax.dot_general(q_ref[0, rs, hs], k, (((1,), (1,)), ((), ())),
                                preferred_element_type=F32) * (DIL_DH ** -0.5)
            s = jnp.where(valid, s, NEG_BIG)
            m = jnp.max(s, axis=-1, keepdims=True)
            p = jnp.exp(s - m)
            l = jnp.sum(p, axis=-1, keepdims=True)
            o = jnp.dot(p.astype(BF16), v, preferred_element_type=F32) / l
            o_ref[0, rs, hs] = o.astype(o_ref.dtype)
            lse_ref[0, rs, hs] = jnp.broadcast_to(m + jnp.log(l), (wc, DIL_DH))


def _dilated_group(z3, group, dilation):
    b, s, _ = z3.shape
    length = s // dilation
    rows = min(length, 512)
    per_row = rows // DIL_WC
    zc = z3.reshape(b, length, dilation * ODD_IN)
    blocks_per_class = ODD_IN // ODD_MIX
    qcol = lambda r: r * blocks_per_class + group * 3

    def cur(which):
        return pl.BlockSpec((1, rows, ODD_MIX), lambda bi, r, n: (bi, n, qcol(r) + which))

    def prev(which):
        return pl.BlockSpec((1, DIL_WC, ODD_MIX),
                            lambda bi, r, n: (bi, jnp.maximum(n * per_row - 1, 0), qcol(r) + which))

    out_spec = pl.BlockSpec((1, rows, ODD_MIX), lambda bi, r, n: (bi, n, r))
    o, lse = pl.pallas_call(
        _dil_kernel,
        grid=(b, dilation, length // rows),
        in_specs=[cur(0), prev(1), cur(1), prev(2), cur(2)],
        out_specs=[out_spec, out_spec],
        out_shape=[jax.ShapeDtypeStruct((b, length, dilation * ODD_MIX), BF16),
                   jax.ShapeDtypeStruct((b, length, dilation * ODD_MIX), F32)],
        compiler_params=_params(("parallel", "parallel", "arbitrary")),
        name=f"dilated_d{dilation}",
    )(zc, zc, zc, zc, zc)
    return o.reshape(b * s, ODD_MIX), lse.reshape(b * s, ODD_MIX)


def _combine_proj_res_kernel(o1_ref, o2_ref, o3_ref, l1_ref, l2_ref, l3_ref, w_ref, x_ref, out_ref):
    l1, l2, l3 = l1_ref[...], l2_ref[...], l3_ref[...]
    m = jnp.maximum(jnp.maximum(l1, l2), l3)
    e1, e2, e3 = jnp.exp(l1 - m), jnp.exp(l2 - m), jnp.exp(l3 - m)
    mix = (e1 * o1_ref[...].astype(F32) + e2 * o2_ref[...].astype(F32)
           + e3 * o3_ref[...].astype(F32)) / (e1 + e2 + e3)
    out_ref[...] = x_ref[...] + jnp.dot(mix.astype(BF16), w_ref[...], preferred_element_type=F32)


def _combine_proj_res(os_, lses, w, x2):
    t, d = x2.shape
    row = lambda width: pl.BlockSpec((ROW_TILE, width), lambda i: (i, 0))
    return pl.pallas_call(
        _combine_proj_res_kernel,
        grid=(t // ROW_TILE,),
        in_specs=[row(ODD_MIX)] * 6 + [_const_spec(w.shape), row(d)],
        out_specs=row(d),
        out_shape=jax.ShapeDtypeStruct((t, d), F32),
        compiler_params=_params(("parallel",)),
        name="combine_proj_res",
    )(*os_, *lses, w, x2)


FFN_CHUNK = 256
CARRY_ROWS = 8


def _ffn_kernel(x_ref, g_ref, wup_ref, cw_ref, wdn_ref, o_ref, carry_ref, *, tiles_per_seq):
    tm = x_ref.shape[0]
    x = x_ref[...]
    hb = _rms_rows(x, g_ref[...]).astype(BF16)

    @pl.when(pl.program_id(0) % tiles_per_seq == 0)
    def _():
        carry_ref[...] = jnp.zeros_like(carry_ref)

    def conv_part(cols):
        u = jnp.dot(hb, wup_ref[:, cols], preferred_element_type=F32)
        ext = jnp.concatenate([carry_ref[:, cols], u], axis=0)
        u1 = pltpu.roll(ext, 1, 0)[CARRY_ROWS:]
        u2 = pltpu.roll(ext, 2, 0)[CARRY_ROWS:]
        carry_ref[:, cols] = u[tm - CARRY_ROWS:]
        cw = cw_ref[:, cols]
        return u2 * cw[0:1] + u1 * cw[1:2] + u * cw[2:3] + cw[3:4]

    o_ref[...] = x
    for c in range(D_FF // FFN_CHUNK):
        gate = conv_part(slice(c * FFN_CHUNK, (c + 1) * FFN_CHUNK))
        val = conv_part(slice(D_FF + c * FFN_CHUNK, D_FF + (c + 1) * FFN_CHUNK))
        act = (gate / (1.0 + jnp.exp(-gate)) * val).astype(BF16)
        o_ref[...] += jnp.dot(act, wdn_ref[c * FFN_CHUNK:(c + 1) * FFN_CHUNK, :],
                              preferred_element_type=F32)


def _ffn(x2, g, w_up, conv_tab, w_down, seq_len):
    t, d = x2.shape
    row = pl.BlockSpec((ROW_TILE, d), lambda i: (i, 0))
    return pl.pallas_call(
        functools.partial(_ffn_kernel, tiles_per_seq=seq_len // ROW_TILE),
        grid=(t // ROW_TILE,),
        in_specs=[row, _const_spec((1, d)), _const_spec(w_up.shape), _const_spec(conv_tab.shape),
                  _const_spec(w_down.shape)],
        out_specs=row,
        out_shape=jax.ShapeDtypeStruct((t, d), F32),
        scratch_shapes=[pltpu.VMEM((CARRY_ROWS, 2 * D_FF), F32)],
        compiler_params=_params(("arbitrary",)),
        name="conv_ffn",
    )(x2, g, w_up, conv_tab, w_down)


def _even_weights(w_in, w_a2, b_a, dq_gain, dk_gain):
    sizes = [A_QK, A_QK, A_V, A_V, GLA_RANK, B_QK, B_QK, B_V]
    offs = [0]
    for c in sizes:
        offs.append(offs[-1] + c)
    aq, ak, av, ag, ar, bq, bk, bv = [w_in[:, offs[i]:offs[i + 1]] for i in range(8)]
    pad = jnp.zeros((w_in.shape[0], LANES - GLA_RANK), w_in.dtype)
    w = jnp.concatenate([aq, ak, av, ag, bq, bk, bv, ar, pad], axis=1).astype(BF16)
    hg = jnp.ones((EV_COLS,), F32)
    hg = hg.at[EV_BQ:EV_BK].set(jnp.tile(dq_gain.astype(F32), B_QK // DIFF_DH) * (DIFF_DH ** -0.5))
    hg = hg.at[EV_BK:EV_BV].set(jnp.tile(dk_gain.astype(F32), B_QK // DIFF_DH))
    norm_group = tuple(DIFF_DH if EV_BQ <= c * LANES < EV_BV else 0 for c in range(EV_COLS // LANES))
    wa2p = jnp.concatenate([w_a2, jnp.zeros((LANES - GLA_RANK, A_QK), w_a2.dtype)], axis=0).astype(BF16)
    return w, hg.reshape(1, EV_COLS), norm_group, wa2p, b_a.astype(F32).reshape(1, A_QK)


def _odd_weights(q_gain, k_gain):
    hg = jnp.ones((3, 3, ODD_MIX), F32)
    hg = hg.at[:, 0].set(jnp.tile(q_gain.astype(F32), DIL_HEADS))
    hg = hg.at[:, 1].set(jnp.tile(k_gain.astype(F32), DIL_HEADS))
    norm_group = tuple(DIL_DH if (c // DIL_HEADS) % 3 < 2 else 0 for c in range(ODD_IN // LANES))
    return hg.reshape(1, ODD_IN), norm_group


def kernel(x, norm_mix, norm_ffn, ev_w_in, ev_w_a2, ev_b_a, ev_gla_gain, ev_dq_gain, ev_dk_gain,
           ev_lq1, ev_lk1, ev_lq2, ev_lk2, ev_diff_gain, ev_w_out, od_w_in, od_q_gain, od_k_gain,
           od_w_out, ffn_w_up, ffn_conv_w, ffn_conv_b, ffn_w_down):
    b, s, d = x.shape
    depth = norm_mix.shape[0]
    x2 = x.reshape(b * s, d).astype(F32)
    row = lambda v: v.astype(F32).reshape(1, -1)
    for i in range(depth):
        g_mix = row(norm_mix[i])
        if i % 2 == 0:
            e = i // 2
            lambda_init = 0.8 - 0.6 * math.exp(-0.3 * i)
            w, hg, norm_group, wa2p, ba = _even_weights(ev_w_in[e], ev_w_a2[e], ev_b_a[e],
                                                        ev_dq_gain[e], ev_dk_gain[e])
            z3 = _norm_proj(x2, g_mix, w, hg, norm_group).reshape(b, s, EV_COLS)
            oa = _gla(z3, wa2p, ba, row(ev_gla_gain[e]))
            ob = _diff_attention(z3, row(ev_lq1[e]), row(ev_lk1[e]), row(ev_lq2[e]), row(ev_lk2[e]),
                                 row(ev_diff_gain[e]), lambda_init)
            x2 = _proj2_res(oa.reshape(b * s, A_V), ob.reshape(b * s, B_V),
                            ev_w_out[e].astype(BF16), x2)
        else:
            o = i // 2
            hg, norm_group = _odd_weights(od_q_gain[o], od_k_gain[o])
            z3 = _norm_proj(x2, g_mix, od_w_in[o].astype(BF16), hg, norm_group).reshape(b, s, ODD_IN)
            outs = [_dilated_group(z3, gi, dil) for gi, (_, dil) in enumerate(DIL_PATTERNS)]
            x2 = _combine_proj_res([o_ for o_, _ in outs], [l_ for _, l_ in outs],
                                   od_w_out[o].astype(BF16), x2)
        conv_tab = jnp.concatenate([ffn_conv_w[i].reshape(3, 2 * D_FF), ffn_conv_b[i].reshape(1, 2 * D_FF),
                                    jnp.zeros((4, 2 * D_FF), F32)], axis=0).astype(F32)
        x2 = _ffn(x2, row(norm_ffn[i]), ffn_w_up[i].astype(BF16), conv_tab,
                  ffn_w_down[i].astype(BF16), s)
    return x2.reshape(b, s, d).astype(x.dtype)
```

```python
import functools
import math

import jax
import jax.numpy as jnp
from jax import lax
from jax.experimental import pallas as pl
from jax.experimental.pallas import tpu as pltpu

F32 = jnp.float32
BF16 = jnp.bfloat16

EPS = 1e-6
LANES = 128
MXU_COLS = 256
VMEM_LIMIT = 56 * 1024 * 1024
NEG_BIG = -1e30

D_MODEL = 1024
GLA_HEADS, GLA_DK, GLA_DV, GLA_RANK, GLA_TAU, GLA_CHUNK = 4, 64, 128, 16, 16.0, 64
DIFF_HEADS, DIFF_DH, DIFF_DV = 4, 64, 128
A_QK, A_V, B_QK, B_V = 256, 512, 512, 512
DIL_PATTERNS = ((128, 1), (512, 4), (2048, 16))
DIL_HEADS, DIL_DH = 4, 128
ODD_MIX = DIL_HEADS * DIL_DH
ODD_IN = 3 * 3 * ODD_MIX
D_FF = 2816

EV_AQ, EV_AK, EV_AV, EV_AG, EV_BQ, EV_BK, EV_BV, EV_AR = 0, 256, 512, 1024, 1536, 2048, 2560, 3072
EV_COLS = 3200

ROW_TILE = 512


def _const_spec(shape):
    nd = len(shape)
    return pl.BlockSpec(shape, lambda *_: (0,) * nd)


def _params(sem):
    return pltpu.CompilerParams(dimension_semantics=sem, vmem_limit_bytes=VMEM_LIMIT)


def _rms_rows(x, g):
    return x * lax.rsqrt(jnp.mean(x * x, axis=-1, keepdims=True) + EPS) * g


def _norm_proj_kernel(x_ref, g_ref, w_ref, hg_ref, o_ref, *, norm_group):
    hb = _rms_rows(x_ref[...], g_ref[...]).astype(BF16)
    n_cols = o_ref.shape[1]
    for c0 in range(0, n_cols, MXU_COLS):
        width = min(MXU_COLS, n_cols - c0)
        zw = jnp.dot(hb, w_ref[:, c0:c0 + width], preferred_element_type=F32)
        for sub in range(width // LANES):
            cs = slice(c0 + sub * LANES, c0 + (sub + 1) * LANES)
            z = zw[:, sub * LANES:(sub + 1) * LANES]
            grp = norm_group[cs.start // LANES]
            if grp == LANES:
                ss = jnp.sum(z * z, axis=-1, keepdims=True)
                z = z * lax.rsqrt(ss * (1.0 / LANES) + EPS) * hg_ref[:, cs]
            elif grp == LANES // 2:
                lo = lax.broadcasted_iota(jnp.int32, z.shape, 1) < grp
                sq = z * z
                s_lo = jnp.sum(jnp.where(lo, sq, 0.0), axis=-1, keepdims=True)
                s_hi = jnp.sum(jnp.where(lo, 0.0, sq), axis=-1, keepdims=True)
                ss = jnp.where(lo, s_lo, s_hi)
                z = z * lax.rsqrt(ss * (1.0 / grp) + EPS) * hg_ref[:, cs]
            o_ref[:, cs] = z.astype(o_ref.dtype)


def _norm_proj(x2, g, w, hg, norm_group):
    t, d = x2.shape
    n = w.shape[1]
    return pl.pallas_call(
        functools.partial(_norm_proj_kernel, norm_group=norm_group),
        grid=(t // ROW_TILE,),
        in_specs=[pl.BlockSpec((ROW_TILE, d), lambda i: (i, 0)),
                  _const_spec((1, d)), _const_spec((d, n)), _const_spec((1, n))],
        out_specs=pl.BlockSpec((ROW_TILE, n), lambda i: (i, 0)),
        out_shape=jax.ShapeDtypeStruct((t, n), BF16),
        compiler_params=_params(("parallel",)),
        name="norm_proj",
    )(x2, g, w, hg)


GLA_TILE = 512


def _gla_kernel(q_ref, k_ref, v_ref, gate_ref, ar_ref, wa2_ref, ba_ref, gain_ref, o_ref,
                la_ref, acc_ref, state_ref):
    C = GLA_CHUNK

    @pl.when(pl.program_id(1) == 0)
    def _():
        state_ref[...] = jnp.zeros_like(state_ref)

    logits = jnp.dot(ar_ref[0], wa2_ref[...], preferred_element_type=F32) + ba_ref[...]
    log_sig = jnp.minimum(logits, 0.0) - jnp.log(1.0 + jnp.exp(-jnp.abs(logits)))
    la_ref[...] = log_sig * (1.0 / GLA_TAU)

    row = lax.broadcasted_iota(jnp.int32, (C, C), 0)
    col = lax.broadcasted_iota(jnp.int32, (C, C), 1)
    causal = col <= row
    tril = causal.astype(F32)

    def chunk(c, carry):
        r0 = pl.multiple_of(c * C, C)
        rows = pl.ds(r0, C)
        b = jnp.dot(tril, la_ref[rows, :], preferred_element_type=F32,
                    precision=lax.Precision.HIGHEST)
        b_last = b[C - 1:C, :]
        q_d = (q_ref[0, rows, :].astype(F32) * (GLA_DK ** -0.5) * jnp.exp(b)).astype(BF16)
        k_f = k_ref[0, rows, :].astype(F32)
        k_d = (k_f * jnp.exp(-b)).astype(BF16)
        k_t = (k_f * jnp.exp(b_last - b)).astype(BF16)
        dec = jnp.exp(b_last)
        v_f = v_ref[0, rows, :].astype(F32)
        for h in range(GLA_HEADS):
            ks = slice(h * GLA_DK, (h + 1) * GLA_DK)
            vs = slice(h * GLA_DV, (h + 1) * GLA_DV)
            att = lax.dot_general(q_d[:, ks], k_d[:, ks], (((1,), (1,)), ((), ())),
                                  preferred_element_type=F32)
            att = jnp.where(causal, att, 0.0).astype(BF16)
            v_h = v_f[:, vs]
            o_h = jnp.dot(att, v_h.astype(BF16), preferred_element_type=F32)
            st = state_ref[h]
            o_h = o_h + lax.dot_general(q_d[:, ks], st.astype(BF16), (((1,), (1,)), ((), ())),
                                        preferred_element_type=F32)
            kv_t = jnp.dot(v_h.T.astype(BF16), k_t[:, ks], preferred_element_type=F32)
            state_ref[h] = st * dec[:, ks] + kv_t
            acc_ref[rows, vs] = o_h
        return carry

    lax.fori_loop(0, q_ref.shape[1] // C, chunk, 0)

    gate = gate_ref[0].astype(F32)
    for h in range(GLA_HEADS):
        vs = slice(h * GLA_DV, (h + 1) * GLA_DV)
        o = _rms_rows(acc_ref[:, vs], gain_ref[...])
        g = gate[:, vs]
        o_ref[0, :, vs] = (o * (g / (1.0 + jnp.exp(-g)))).astype(o_ref.dtype)


def _gla(z3, wa2p, ba, gain):
    b, s, _ = z3.shape
    tq = min(GLA_TILE, s)

    def col(width, start):
        return pl.BlockSpec((1, tq, width), lambda bi, i: (bi, i, start // width))

    return pl.pallas_call(
        _gla_kernel,
        grid=(b, s // tq),
        in_specs=[col(A_QK, EV_AQ), col(A_QK, EV_AK), col(A_V, EV_AV), col(A_V, EV_AG),
                  col(LANES, EV_AR), _const_spec((LANES, A_QK)), _const_spec((1, A_QK)),
                  _const_spec((1, GLA_DV))],
        out_specs=pl.BlockSpec((1, tq, A_V), lambda bi, i: (bi, i, 0)),
        out_shape=jax.ShapeDtypeStruct((b, s, A_V), BF16),
        scratch_shapes=[pltpu.VMEM((tq, A_QK), F32), pltpu.VMEM((tq, A_V), F32),
                        pltpu.VMEM((GLA_HEADS, GLA_DV, GLA_DK), F32)],
        compiler_params=_params(("parallel", "arbitrary")),
        name="gla",
    )(z3, z3, z3, z3, z3, wa2p, ba, gain)


DIFF_TILE = 512
DIFF_ROWS = 256
LOG2E = math.log2(math.e)


def _diff_kernel(qi_ref, ki_ref, q_ref, k_ref, v_ref, lq1_ref, lk1_ref, lq2_ref, lk2_ref, gain_ref,
                 o_ref, qv_ref, m_ref, acc_ref, *, lambda_init):
    t = pl.program_id(2)
    qi = qi_ref[t]
    ki = ki_ref[t]
    tq = q_ref.shape[1]
    tk = k_ref.shape[1]
    rc = DIFF_ROWS

    @pl.when(ki == 0)
    def _():
        q = q_ref[0]
        first = lax.broadcasted_iota(jnp.int32, q.shape, 1) < DIFF_DH
        zero = jnp.zeros_like(q)
        qv_ref[0:tq, :] = jnp.where(first, q, zero)
        qv_ref[tq:2 * tq, :] = jnp.where(first, zero, q)
        m_ref[...] = jnp.full_like(m_ref, NEG_BIG)
        acc_ref[...] = jnp.zeros_like(acc_ref)

    def update(masked):
        k = k_ref[0]
        v1 = jnp.concatenate([v_ref[0], jnp.ones((tk, LANES), BF16)], axis=1)
        n_chunks = 2 * tq // rc

        def key_width(c):
            return min((c * rc) % tq + rc, tk) if masked else tk

        def scores(c):
            return lax.dot_general(qv_ref[c * rc:(c + 1) * rc, :], k[0:key_width(c)],
                                   (((1,), (1,)), ((), ())), preferred_element_type=F32)

        s_next = scores(0)
        for c in range(n_chunks):
            rs = slice(c * rc, (c + 1) * rc)
            q0 = (c * rc) % tq
            kw = key_width(c)
            s = s_next
            s_next = scores(c + 1) if c + 1 < n_chunks else None
            if masked:
                r = lax.broadcasted_iota(jnp.int32, (rc, kw), 0) + q0
                col = lax.broadcasted_iota(jnp.int32, (rc, kw), 1)
                s = jnp.where(col <= r, s, NEG_BIG)
            m_prev = m_ref[rs, :]
            m_new = jnp.maximum(m_prev, jnp.max(s, axis=-1, keepdims=True))
            alpha = jnp.exp2(m_prev - m_new)
            p = jnp.exp2(s - jnp.tile(m_new, (1, kw // LANES)))
            pv = jnp.dot(p.astype(BF16), v1[0:kw], preferred_element_type=F32)
            acc_ref[rs, :] = acc_ref[rs, :] * jnp.tile(alpha, (1, 2)) + pv
            m_ref[rs, :] = m_new

    @pl.when(ki < qi)
    def _():
        update(False)

    @pl.when(ki == qi)
    def _():
        update(True)
        lam = (jnp.exp(jnp.sum(lq1_ref[...] * lk1_ref[...], axis=-1, keepdims=True))
               - jnp.exp(jnp.sum(lq2_ref[...] * lk2_ref[...], axis=-1, keepdims=True))
               + lambda_init)
        o1 = acc_ref[0:tq, 0:DIFF_DV] / acc_ref[0:tq, DIFF_DV:]
        o2 = acc_ref[tq:2 * tq, 0:DIFF_DV] / acc_ref[tq:2 * tq, DIFF_DV:]
        o = _rms_rows(o1 - lam * o2, gain_ref[...]) * (1.0 - lambda_init)
        o_ref[0] = o.astype(o_ref.dtype)


def _diff_attention(z3, lq1, lk1, lq2, lk2, gain, lambda_init):
    b, s, _ = z3.shape
    tq = min(DIFF_TILE, s)
    nq = s // tq
    pairs = [(qi, ki) for qi in range(nq) for ki in range(qi + 1)]
    qi_tab = jnp.asarray([p[0] for p in pairs], jnp.int32)
    ki_tab = jnp.asarray([p[1] for p in pairs], jnp.int32)
    qc, kc, vc = EV_BQ // LANES, EV_BK // LANES, EV_BV // LANES

    grid_spec = pltpu.PrefetchScalarGridSpec(
        num_scalar_prefetch=2,
        grid=(b, DIFF_HEADS, len(pairs)),
        in_specs=[pl.BlockSpec((1, tq, LANES), lambda bi, h, t, qt, kt: (bi, qt[t], qc + h)),
                  pl.BlockSpec((1, tq, LANES), lambda bi, h, t, qt, kt: (bi, kt[t], kc + h)),
                  pl.BlockSpec((1, tq, LANES), lambda bi, h, t, qt, kt: (bi, kt[t], vc + h)),
                  _const_spec((1, DIFF_DH)), _const_spec((1, DIFF_DH)),
                  _const_spec((1, DIFF_DH)), _const_spec((1, DIFF_DH)),
                  _const_spec((1, DIFF_DV))],
        out_specs=pl.BlockSpec((1, tq, LANES), lambda bi, h, t, qt, kt: (bi, qt[t], h)),
        scratch_shapes=[pltpu.VMEM((2 * tq, LANES), BF16), pltpu.VMEM((2 * tq, LANES), F32),
                        pltpu.VMEM((2 * tq, 2 * DIFF_DV), F32)],
    )
    return pl.pallas_call(
        functools.partial(_diff_kernel, lambda_init=lambda_init),
        grid_spec=grid_spec,
        out_shape=jax.ShapeDtypeStruct((b, s, B_V), BF16),
        compiler_params=_params(("parallel", "parallel", "arbitrary")),
        name="diff_attention",
    )(qi_tab, ki_tab, z3, z3, z3, lq1, lk1, lq2, lk2, gain)


def _proj2_res_kernel(a_ref, b_ref, w_ref, x_ref, o_ref):
    ka = a_ref.shape[1]
    o_ref[...] = (x_ref[...]
                  + jnp.dot(a_ref[...], w_ref[0:ka, :], preferred_element_type=F32)
                  + jnp.dot(b_ref[...], w_ref[ka:, :], preferred_element_type=F32))


def _proj2_res(a, b, w, x2):
    t, d = x2.shape
    row = lambda width: pl.BlockSpec((ROW_TILE, width), lambda i: (i, 0))
    return pl.pallas_call(
        _proj2_res_kernel,
        grid=(t // ROW_TILE,),
        in_specs=[row(a.shape[1]), row(b.shape[1]), _const_spec(w.shape), row(d)],
        out_specs=row(d),
        out_shape=jax.ShapeDtypeStruct((t, d), F32),
        compiler_params=_params(("parallel",)),
        name="proj2_res",
    )(a, b, w, x2)


DIL_WC = 128


def _dil_kernel(q_ref, kp_ref, kc_ref, vp_ref, vc_ref, o_ref, lse_ref):
    n = pl.program_id(2)
    rows = q_ref.shape[1]
    wc = DIL_WC
    i = lax.broadcasted_iota(jnp.int32, (wc, 2 * wc), 0)
    j = lax.broadcasted_iota(jnp.int32, (wc, 2 * wc), 1)
    dist = i + wc - j
    band = (dist >= 0) & (dist <= wc)
    band_first = band & ((j >= wc) | (n > 0))
    for jb in range(rows // wc):
        rs = slice(jb * wc, (jb + 1) * wc)
        for h in range(DIL_HEADS):
            hs = slice(h * DIL_DH, (h + 1) * DIL_DH)
            if jb == 0:
                k = jnp.concatenate([kp_ref[0, :, hs], kc_ref[0, 0:wc, hs]], axis=0)
                v = jnp.concatenate([vp_ref[0, :, hs], vc_ref[0, 0:wc, hs]], axis=0)
                valid = band_first
            else:
                k = kc_ref[0, (jb - 1) * wc:(jb + 1) * wc, hs]
                v = vc_ref[0, (jb - 1) * wc:(jb + 1) * wc, hs]
                valid = band
            s = lax.dot_general(q_ref[0, rs, hs], k, (((1,), (1,)), ((), ())),
                                preferred_element_type=F32) * (DIL_DH ** -0.5)
            s = jnp.where(valid, s, NEG_BIG)
            m = jnp.max(s, axis=-1, keepdims=True)
            p = jnp.exp(s - m)
            l = jnp.sum(p, axis=-1, keepdims=True)
            o = jnp.dot(p.astype(BF16), v, preferred_element_type=F32) / l
            o_ref[0, rs, hs] = o.astype(o_ref.dtype)
            lse_ref[0, rs, hs] = jnp.broadcast_to(m + jnp.log(l), (wc, DIL_DH))


def _dilated_group(z3, group, dilation):
    b, s, _ = z3.shape
    length = s // dilation
    rows = min(length, 512)
    per_row = rows // DIL_WC
    zc = z3.reshape(b, length, dilation * ODD_IN)
    blocks_per_class = ODD_IN // ODD_MIX
    qcol = lambda r: r * blocks_per_class + group * 3

    def cur(which):
        return pl.BlockSpec((1, rows, ODD_MIX), lambda bi, r, n: (bi, n, qcol(r) + which))

    def prev(which):
        return pl.BlockSpec((1, DIL_WC, ODD_MIX),
                            lambda bi, r, n: (bi, jnp.maximum(n * per_row - 1, 0), qcol(r) + which))

    out_spec = pl.BlockSpec((1, rows, ODD_MIX), lambda bi, r, n: (bi, n, r))
    o, lse = pl.pallas_call(
        _dil_kernel,
        grid=(b, dilation, length // rows),
        in_specs=[cur(0), prev(1), cur(1), prev(2), cur(2)],
        out_specs=[out_spec, out_spec],
        out_shape=[jax.ShapeDtypeStruct((b, length, dilation * ODD_MIX), BF16),
                   jax.ShapeDtypeStruct((b, length, dilation * ODD_MIX), F32)],
        compiler_params=_params(("parallel", "parallel", "arbitrary")),
        name=f"dilated_d{dilation}",
    )(zc, zc, zc, zc, zc)
    return o.reshape(b * s, ODD_MIX), lse.reshape(b * s, ODD_MIX)


def _combine_proj_res_kernel(o1_ref, o2_ref, o3_ref, l1_ref, l2_ref, l3_ref, w_ref, x_ref, out_ref):
    l1, l2, l3 = l1_ref[...], l2_ref[...], l3_ref[...]
    m = jnp.maximum(jnp.maximum(l1, l2), l3)
    e1, e2, e3 = jnp.exp(l1 - m), jnp.exp(l2 - m), jnp.exp(l3 - m)
    mix = (e1 * o1_ref[...].astype(F32) + e2 * o2_ref[...].astype(F32)
           + e3 * o3_ref[...].astype(F32)) / (e1 + e2 + e3)
    out_ref[...] = x_ref[...] + jnp.dot(mix.astype(BF16), w_ref[...], preferred_element_type=F32)


def _combine_proj_res(os_, lses, w, x2):
    t, d = x2.shape
    row = lambda width: pl.BlockSpec((ROW_TILE, width), lambda i: (i, 0))
    return pl.pallas_call(
        _combine_proj_res_kernel,
        grid=(t // ROW_TILE,),
        in_specs=[row(ODD_MIX)] * 6 + [_const_spec(w.shape), row(d)],
        out_specs=row(d),
        out_shape=jax.ShapeDtypeStruct((t, d), F32),
        compiler_params=_params(("parallel",)),
        name="combine_proj_res",
    )(*os_, *lses, w, x2)


FFN_CHUNK = 256
CARRY_ROWS = 8


def _ffn_kernel(x_ref, g_ref, wup_ref, cw_ref, wdn_ref, o_ref, carry_ref, *, tiles_per_seq):
    tm = x_ref.shape[0]
    x = x_ref[...]
    hb = _rms_rows(x, g_ref[...]).astype(BF16)
    n_chunks = D_FF // FFN_CHUNK

    @pl.when(pl.program_id(0) % tiles_per_seq == 0)
    def _():
        carry_ref[...] = jnp.zeros_like(carry_ref)

    def cols_of(c):
        return (slice(c * FFN_CHUNK, (c + 1) * FFN_CHUNK),
                slice(D_FF + c * FFN_CHUNK, D_FF + (c + 1) * FFN_CHUNK))

    def up(c):
        return [jnp.dot(hb, wup_ref[:, cols], preferred_element_type=F32) for cols in cols_of(c)]

    def conv(u, cols):
        ext = jnp.concatenate([carry_ref[:, cols], u], axis=0)
        u1 = pltpu.roll(ext, 1, 0)[CARRY_ROWS:]
        u2 = pltpu.roll(ext, 2, 0)[CARRY_ROWS:]
        carry_ref[:, cols] = u[tm - CARRY_ROWS:]
        cw = cw_ref[:, cols]
        return u2 * cw[0:1] + u1 * cw[1:2] + u * cw[2:3] + cw[3:4]

    o_ref[...] = x
    u_cur = up(0)
    for c in range(n_chunks):
        u_next = up(c + 1) if c + 1 < n_chunks else None
        gate, val = [conv(u, cols) for u, cols in zip(u_cur, cols_of(c))]
        act = (gate / (1.0 + jnp.exp(-gate)) * val).astype(BF16)
        o_ref[...] += jnp.dot(act, wdn_ref[c * FFN_CHUNK:(c + 1) * FFN_CHUNK, :],
                              preferred_element_type=F32)
        u_cur = u_next


def _ffn(x2, g, w_up, conv_tab, w_down, seq_len):
    t, d = x2.shape
    row = pl.BlockSpec((ROW_TILE, d), lambda i: (i, 0))
    return pl.pallas_call(
        functools.partial(_ffn_kernel, tiles_per_seq=seq_len // ROW_TILE),
        grid=(t // ROW_TILE,),
        in_specs=[row, _const_spec((1, d)), _const_spec(w_up.shape), _const_spec(conv_tab.shape),
                  _const_spec(w_down.shape)],
        out_specs=row,
        out_shape=jax.ShapeDtypeStruct((t, d), F32),
        scratch_shapes=[pltpu.VMEM((CARRY_ROWS, 2 * D_FF), F32)],
        compiler_params=_params(("arbitrary",)),
        name="conv_ffn",
    )(x2, g, w_up, conv_tab, w_down)


def _even_weights(w_in, w_a2, b_a, dq_gain, dk_gain):
    sizes = [A_QK, A_QK, A_V, A_V, GLA_RANK, B_QK, B_QK, B_V]
    offs = [0]
    for c in sizes:
        offs.append(offs[-1] + c)
    aq, ak, av, ag, ar, bq, bk, bv = [w_in[:, offs[i]:offs[i + 1]] for i in range(8)]
    pad = jnp.zeros((w_in.shape[0], LANES - GLA_RANK), w_in.dtype)
    w = jnp.concatenate([aq, ak, av, ag, bq, bk, bv, ar, pad], axis=1).astype(BF16)
    hg = jnp.ones((EV_COLS,), F32)
    hg = hg.at[EV_BQ:EV_BK].set(jnp.tile(dq_gain.astype(F32), B_QK // DIFF_DH) * (DIFF_DH ** -0.5 * LOG2E))
    hg = hg.at[EV_BK:EV_BV].set(jnp.tile(dk_gain.astype(F32), B_QK // DIFF_DH))
    norm_group = tuple(DIFF_DH if EV_BQ <= c * LANES < EV_BV else 0 for c in range(EV_COLS // LANES))
    wa2p = jnp.concatenate([w_a2, jnp.zeros((LANES - GLA_RANK, A_QK), w_a2.dtype)], axis=0).astype(BF16)
    return w, hg.reshape(1, EV_COLS), norm_group, wa2p, b_a.astype(F32).reshape(1, A_QK)


def _odd_weights(q_gain, k_gain):
    hg = jnp.ones((3, 3, ODD_MIX), F32)
    hg = hg.at[:, 0].set(jnp.tile(q_gain.astype(F32), DIL_HEADS))
    hg = hg.at[:, 1].set(jnp.tile(k_gain.astype(F32), DIL_HEADS))
    norm_group = tuple(DIL_DH if (c // DIL_HEADS) % 3 < 2 else 0 for c in range(ODD_IN // LANES))
    return hg.reshape(1, ODD_IN), norm_group


def kernel(x, norm_mix, norm_ffn, ev_w_in, ev_w_a2, ev_b_a, ev_gla_gain, ev_dq_gain, ev_dk_gain,
           ev_lq1, ev_lk1, ev_lq2, ev_lk2, ev_diff_gain, ev_w_out, od_w_in, od_q_gain, od_k_gain,
           od_w_out, ffn_w_up, ffn_conv_w, ffn_conv_b, ffn_w_down):
    b, s, d = x.shape
    depth = norm_mix.shape[0]
    x2 = x.reshape(b * s, d).astype(F32)
    row = lambda v: v.astype(F32).reshape(1, -1)
    for i in range(depth):
        g_mix = row(norm_mix[i])
        if i % 2 == 0:
            e = i // 2
            lambda_init = 0.8 - 0.6 * math.exp(-0.3 * i)
            w, hg, norm_group, wa2p, ba = _even_weights(ev_w_in[e], ev_w_a2[e], ev_b_a[e],
                                                        ev_dq_gain[e], ev_dk_gain[e])
            z3 = _norm_proj(x2, g_mix, w, hg, norm_group).reshape(b, s, EV_COLS)
            oa = _gla(z3, wa2p, ba, row(ev_gla_gain[e]))
            ob = _diff_attention(z3, row(ev_lq1[e]), row(ev_lk1[e]), row(ev_lq2[e]), row(ev_lk2[e]),
                                 row(ev_diff_gain[e]), lambda_init)
            x2 = _proj2_res(oa.reshape(b * s, A_V), ob.reshape(b * s, B_V),
                            ev_w_out[e].astype(BF16), x2)
        else:
            o = i // 2
            hg, norm_group = _odd_weights(od_q_gain[o], od_k_gain[o])
            z3 = _norm_proj(x2, g_mix, od_w_in[o].astype(BF16), hg, norm_group).reshape(b, s, ODD_IN)
            outs = [_dilated_group(z3, gi, dil) for gi, (_, dil) in enumerate(DIL_PATTERNS)]
            x2 = _combine_proj_res([o_ for o_, _ in outs], [l_ for _, l_ in outs],
                                   od_w_out[o].astype(BF16), x2)
        conv_tab = jnp.concatenate([ffn_conv_w[i].reshape(3, 2 * D_FF), ffn_conv_b[i].reshape(1, 2 * D_FF),
                                    jnp.zeros((4, 2 * D_FF), F32)], axis=0).astype(F32)
        x2 = _ffn(x2, row(norm_ffn[i]), ffn_w_up[i].astype(BF16), conv_tab,
                  ffn_w_down[i].astype(BF16), s)
    return x2.reshape(b, s, d).astype(x.dtype)
```

```python
import functools
import math

import jax
import jax.numpy as jnp
from jax import lax
from jax.experimental import pallas as pl
from jax.experimental.pallas import tpu as pltpu

F32 = jnp.float32
BF16 = jnp.bfloat16

EPS = 1e-6
LANES = 128
MXU_COLS = 256
VMEM_LIMIT = 56 * 1024 * 1024
NEG_BIG = -1e30

D_MODEL = 1024
GLA_HEADS, GLA_DK, GLA_DV, GLA_RANK, GLA_TAU, GLA_CHUNK = 4, 64, 128, 16, 16.0, 64
DIFF_HEADS, DIFF_DH, DIFF_DV = 4, 64, 128
A_QK, A_V, B_QK, B_V = 256, 512, 512, 512
DIL_PATTERNS = ((128, 1), (512, 4), (2048, 16))
DIL_HEADS, DIL_DH = 4, 128
ODD_MIX = DIL_HEADS * DIL_DH
ODD_IN = 3 * 3 * ODD_MIX
D_FF = 2816

EV_AQ, EV_AK, EV_AV, EV_AG, EV_BQ, EV_BK, EV_BV, EV_AR = 0, 256, 512, 1024, 1536, 2048, 2560, 3072
EV_COLS = 3200

ROW_TILE = 512


def _const_spec(shape):
    nd = len(shape)
    return pl.BlockSpec(shape, lambda *_: (0,) * nd)


def _params(sem):
    return pltpu.CompilerParams(dimension_semantics=sem, vmem_limit_bytes=VMEM_LIMIT)


def _rms_rows(x, g):
    return x * lax.rsqrt(jnp.mean(x * x, axis=-1, keepdims=True) + EPS) * g


def _norm_proj_kernel(x_ref, g_ref, w_ref, hg_ref, o_ref, *, norm_group):
    hb = _rms_rows(x_ref[...], g_ref[...]).astype(BF16)
    n_cols = o_ref.shape[1]
    for c0 in range(0, n_cols, MXU_COLS):
        width = min(MXU_COLS, n_cols - c0)
        zw = jnp.dot(hb, w_ref[:, c0:c0 + width], preferred_element_type=F32)
        for sub in range(width // LANES):
            cs = slice(c0 + sub * LANES, c0 + (sub + 1) * LANES)
            z = zw[:, sub * LANES:(sub + 1) * LANES]
            grp = norm_group[cs.start // LANES]
            if grp == LANES:
                ss = jnp.sum(z * z, axis=-1, keepdims=True)
                z = z * lax.rsqrt(ss * (1.0 / LANES) + EPS) * hg_ref[:, cs]
            elif grp == LANES // 2:
                lo = lax.broadcasted_iota(jnp.int32, z.shape, 1) < grp
                sq = z * z
                s_lo = jnp.sum(jnp.where(lo, sq, 0.0), axis=-1, keepdims=True)
                s_hi = jnp.sum(jnp.where(lo, 0.0, sq), axis=-1, keepdims=True)
                ss = jnp.where(lo, s_lo, s_hi)
                z = z * lax.rsqrt(ss * (1.0 / grp) + EPS) * hg_ref[:, cs]
            o_ref[:, cs] = z.astype(o_ref.dtype)


def _norm_proj(x2, g, w, hg, norm_group):
    t, d = x2.shape
    n = w.shape[1]
    return pl.pallas_call(
        functools.partial(_norm_proj_kernel, norm_group=norm_group),
        grid=(t // ROW_TILE,),
        in_specs=[pl.BlockSpec((ROW_TILE, d), lambda i: (i, 0)),
                  _const_spec((1, d)), _const_spec((d, n)), _const_spec((1, n))],
        out_specs=pl.BlockSpec((ROW_TILE, n), lambda i: (i, 0)),
        out_shape=jax.ShapeDtypeStruct((t, n), BF16),
        compiler_params=_params(("parallel",)),
        name="norm_proj",
    )(x2, g, w, hg)


GLA_TILE = 512


def _gla_kernel(q_ref, k_ref, v_ref, gate_ref, ar_ref, wa2_ref, ba_ref, gain_ref, o_ref,
                la_ref, acc_ref, state_ref):
    C = GLA_CHUNK

    @pl.when(pl.program_id(1) == 0)
    def _():
        state_ref[...] = jnp.zeros_like(state_ref)

    logits = jnp.dot(ar_ref[0], wa2_ref[...], preferred_element_type=F32) + ba_ref[...]
    log_sig = jnp.minimum(logits, 0.0) - jnp.log(1.0 + jnp.exp(-jnp.abs(logits)))
    la_ref[...] = log_sig * (1.0 / GLA_TAU)

    row = lax.broadcasted_iota(jnp.int32, (C, C), 0)
    col = lax.broadcasted_iota(jnp.int32, (C, C), 1)
    causal = col <= row
    tril = causal.astype(F32)

    def chunk(c, carry):
        r0 = pl.multiple_of(c * C, C)
        rows = pl.ds(r0, C)
        b = jnp.dot(tril, la_ref[rows, :], preferred_element_type=F32,
                    precision=lax.Precision.HIGHEST)
        b_last = b[C - 1:C, :]
        q_d = (q_ref[0, rows, :].astype(F32) * (GLA_DK ** -0.5) * jnp.exp(b)).astype(BF16)
        k_f = k_ref[0, rows, :].astype(F32)
        k_d = (k_f * jnp.exp(-b)).astype(BF16)
        k_t = (k_f * jnp.exp(b_last - b)).astype(BF16)
        dec = jnp.exp(b_last)
        v_f = v_ref[0, rows, :].astype(F32)
        for h in range(GLA_HEADS):
            ks = slice(h * GLA_DK, (h + 1) * GLA_DK)
            vs = slice(h * GLA_DV, (h + 1) * GLA_DV)
            att = lax.dot_general(q_d[:, ks], k_d[:, ks], (((1,), (1,)), ((), ())),
                                  preferred_element_type=F32)
            att = jnp.where(causal, att, 0.0).astype(BF16)
            v_h = v_f[:, vs]
            o_h = jnp.dot(att, v_h.astype(BF16), preferred_element_type=F32)
            st = state_ref[h]
            o_h = o_h + lax.dot_general(q_d[:, ks], st.astype(BF16), (((1,), (1,)), ((), ())),
                                        preferred_element_type=F32)
            kv_t = jnp.dot(v_h.T.astype(BF16), k_t[:, ks], preferred_element_type=F32)
            state_ref[h] = st * dec[:, ks] + kv_t
            acc_ref[rows, vs] = o_h
        return carry

    lax.fori_loop(0, q_ref.shape[1] // C, chunk, 0)

    gate = gate_ref[0].astype(F32)
    for h in range(GLA_HEADS):
        vs = slice(h * GLA_DV, (h + 1) * GLA_DV)
        o = _rms_rows(acc_ref[:, vs], gain_ref[...])
        g = gate[:, vs]
        o_ref[0, :, vs] = (o * (g / (1.0 + jnp.exp(-g)))).astype(o_ref.dtype)


def _gla(z3, wa2p, ba, gain):
    b, s, _ = z3.shape
    tq = min(GLA_TILE, s)

    def col(width, start):
        return pl.BlockSpec((1, tq, width), lambda bi, i: (bi, i, start // width))

    return pl.pallas_call(
        _gla_kernel,
        grid=(b, s // tq),
        in_specs=[col(A_QK, EV_AQ), col(A_QK, EV_AK), col(A_V, EV_AV), col(A_V, EV_AG),
                  col(LANES, EV_AR), _const_spec((LANES, A_QK)), _const_spec((1, A_QK)),
                  _const_spec((1, GLA_DV))],
        out_specs=pl.BlockSpec((1, tq, A_V), lambda bi, i: (bi, i, 0)),
        out_shape=jax.ShapeDtypeStruct((b, s, A_V), BF16),
        scratch_shapes=[pltpu.VMEM((tq, A_QK), F32), pltpu.VMEM((tq, A_V), F32),
                        pltpu.VMEM((GLA_HEADS, GLA_DV, GLA_DK), F32)],
        compiler_params=_params(("parallel", "arbitrary")),
        name="gla",
    )(z3, z3, z3, z3, z3, wa2p, ba, gain)


DIFF_TILE = 512
DIFF_ROWS = 256
LOG2E = math.log2(math.e)


def _diff_kernel(qi_ref, ki_ref, q_ref, k_ref, v_ref, lq1_ref, lk1_ref, lq2_ref, lk2_ref, gain_ref,
                 o_ref, qv_ref, m_ref, acc_ref, *, lambda_init):
    t = pl.program_id(2)
    qi = qi_ref[t]
    ki = ki_ref[t]
    tq = q_ref.shape[1]
    tk = k_ref.shape[1]
    rc = DIFF_ROWS

    @pl.when(ki == 0)
    def _():
        q = q_ref[0]
        first = lax.broadcasted_iota(jnp.int32, q.shape, 1) < DIFF_DH
        zero = jnp.zeros_like(q)
        qv_ref[0:tq, :] = jnp.where(first, q, zero)
        qv_ref[tq:2 * tq, :] = jnp.where(first, zero, q)
        m_ref[...] = jnp.full_like(m_ref, NEG_BIG)
        acc_ref[...] = jnp.zeros_like(acc_ref)

    def update(masked):
        k = k_ref[0]
        v1 = jnp.concatenate([v_ref[0], jnp.ones((tk, LANES), BF16)], axis=1)
        n_chunks = 2 * tq // rc

        def key_width(c):
            return min((c * rc) % tq + rc, tk) if masked else tk

        def scores(c):
            return lax.dot_general(qv_ref[c * rc:(c + 1) * rc, :], k[0:key_width(c)],
                                   (((1,), (1,)), ((), ())), preferred_element_type=F32)

        s_next = scores(0)
        for c in range(n_chunks):
            rs = slice(c * rc, (c + 1) * rc)
            q0 = (c * rc) % tq
            kw = key_width(c)
            s = s_next
            s_next = scores(c + 1) if c + 1 < n_chunks else None
            if masked:
                r = lax.broadcasted_iota(jnp.int32, (rc, kw), 0) + q0
                col = lax.broadcasted_iota(jnp.int32, (rc, kw), 1)
                s = jnp.where(col <= r, s, NEG_BIG)
            m_prev = m_ref[rs, :]
            m_new = jnp.maximum(m_prev, jnp.max(s, axis=-1, keepdims=True))
            alpha = jnp.exp2(m_prev - m_new)
            p = jnp.exp2(s - jnp.tile(m_new, (1, kw // LANES)))
            pv = jnp.dot(p.astype(BF16), v1[0:kw], preferred_element_type=F32)
            acc_ref[rs, :] = acc_ref[rs, :] * jnp.tile(alpha, (1, 2)) + pv
            m_ref[rs, :] = m_new

    @pl.when(ki < qi)
    def _():
        update(False)

    @pl.when(ki == qi)
    def _():
        update(True)
        lam = (jnp.exp(jnp.sum(lq1_ref[...] * lk1_ref[...], axis=-1, keepdims=True))
               - jnp.exp(jnp.sum(lq2_ref[...] * lk2_ref[...], axis=-1, keepdims=True))
               + lambda_init)
        o1 = acc_ref[0:tq, 0:DIFF_DV] / acc_ref[0:tq, DIFF_DV:]
        o2 = acc_ref[tq:2 * tq, 0:DIFF_DV] / acc_ref[tq:2 * tq, DIFF_DV:]
        o = _rms_rows(o1 - lam * o2, gain_ref[...]) * (1.0 - lambda_init)
        o_ref[0] = o.astype(o_ref.dtype)


def _diff_attention(z3, lq1, lk1, lq2, lk2, gain, lambda_init):
    b, s, _ = z3.shape
    tq = min(DIFF_TILE, s)
    nq = s // tq
    pairs = [(qi, ki) for qi in range(nq) for ki in range(qi + 1)]
    qi_tab = jnp.asarray([p[0] for p in pairs], jnp.int32)
    ki_tab = jnp.asarray([p[1] for p in pairs], jnp.int32)
    qc, kc, vc = EV_BQ // LANES, EV_BK // LANES, EV_BV // LANES

    grid_spec = pltpu.PrefetchScalarGridSpec(
        num_scalar_prefetch=2,
        grid=(b, DIFF_HEADS, len(pairs)),
        in_specs=[pl.BlockSpec((1, tq, LANES), lambda bi, h, t, qt, kt: (bi, qt[t], qc + h)),
                  pl.BlockSpec((1, tq, LANES), lambda bi, h, t, qt, kt: (bi, kt[t], kc + h)),
                  pl.BlockSpec((1, tq, LANES), lambda bi, h, t, qt, kt: (bi, kt[t], vc + h)),
                  _const_spec((1, DIFF_DH)), _const_spec((1, DIFF_DH)),
                  _const_spec((1, DIFF_DH)), _const_spec((1, DIFF_DH)),
                  _const_spec((1, DIFF_DV))],
        out_specs=pl.BlockSpec((1, tq, LANES), lambda bi, h, t, qt, kt: (bi, qt[t], h)),
        scratch_shapes=[pltpu.VMEM((2 * tq, LANES), BF16), pltpu.VMEM((2 * tq, LANES), F32),
                        pltpu.VMEM((2 * tq, 2 * DIFF_DV), F32)],
    )
    return pl.pallas_call(
        functools.partial(_diff_kernel, lambda_init=lambda_init),
        grid_spec=grid_spec,
        out_shape=jax.ShapeDtypeStruct((b, s, B_V), BF16),
        compiler_params=_params(("parallel", "parallel", "arbitrary")),
        name="diff_attention",
    )(qi_tab, ki_tab, z3, z3, z3, lq1, lk1, lq2, lk2, gain)


def _proj2_res_kernel(a_ref, b_ref, w_ref, x_ref, o_ref):
    ka = a_ref.shape[1]
    o_ref[...] = (x_ref[...]
                  + jnp.dot(a_ref[...], w_ref[0:ka, :], preferred_element_type=F32)
                  + jnp.dot(b_ref[...], w_ref[ka:, :], preferred_element_type=F32))


def _proj2_res(a, b, w, x2):
    t, d = x2.shape
    row = lambda width: pl.BlockSpec((ROW_TILE, width), lambda i: (i, 0))
    return pl.pallas_call(
        _proj2_res_kernel,
        grid=(t // ROW_TILE,),
        in_specs=[row(a.shape[1]), row(b.shape[1]), _const_spec(w.shape), row(d)],
        out_specs=row(d),
        out_shape=jax.ShapeDtypeStruct((t, d), F32),
        compiler_params=_params(("parallel",)),
        name="proj2_res",
    )(a, b, w, x2)


DIL_WC = 128
GROUP_COLS = 3 * ODD_MIX


def _odd_proj_kernel(x_ref, g_ref, w_ref, hg_ref, *refs):
    out_refs, h_ref = refs[:-1], refs[-1]
    tm = x_ref.shape[0]
    h = _rms_rows(x_ref[...], g_ref[...])
    n_lane_tiles = h_ref.shape[0]
    for c in range(n_lane_tiles):
        h_ref[c] = h[:, c * LANES:(c + 1) * LANES]
    for gi, o_ref in enumerate(out_refs):
        d = o_ref.shape[1]
        n = tm // d
        if d == 1:
            hb = h.astype(BF16)
        else:
            hb = jnp.concatenate(
                [jnp.concatenate([h_ref[c, pl.ds(r, n, stride=d), :] for r in range(d)], axis=0)
                 for c in range(n_lane_tiles)], axis=1).astype(BF16)
        for c0 in range(0, GROUP_COLS, MXU_COLS):
            zw = jnp.dot(hb, w_ref[:, gi * GROUP_COLS + c0:gi * GROUP_COLS + c0 + MXU_COLS],
                         preferred_element_type=F32)
            for sub in range(MXU_COLS // LANES):
                col = c0 + sub * LANES
                cs = slice(gi * GROUP_COLS + col, gi * GROUP_COLS + col + LANES)
                z = zw[:, sub * LANES:(sub + 1) * LANES]
                if col < 2 * ODD_MIX:
                    ss = jnp.sum(z * z, axis=-1, keepdims=True)
                    z = z * lax.rsqrt(ss * (1.0 / LANES) + EPS) * hg_ref[:, cs]
                z = z.astype(o_ref.dtype)
                for r in range(d):
                    o_ref[0, r, :, col:col + LANES] = z[r * n:(r + 1) * n]


def _odd_proj(x2, g, w, hg, batch, seq):
    t, dm = x2.shape
    tiles = seq // ROW_TILE
    out_specs, out_shapes = [], []
    for _, d in DIL_PATTERNS:
        out_specs.append(pl.BlockSpec((1, d, ROW_TILE // d, GROUP_COLS),
                                      lambda i: (i // tiles, 0, i % tiles, 0)))
        out_shapes.append(jax.ShapeDtypeStruct((batch, d, seq // d, GROUP_COLS), BF16))
    return pl.pallas_call(
        _odd_proj_kernel,
        grid=(t // ROW_TILE,),
        in_specs=[pl.BlockSpec((ROW_TILE, dm), lambda i: (i, 0)),
                  _const_spec((1, dm)), _const_spec(w.shape), _const_spec(hg.shape)],
        out_specs=out_specs,
        out_shape=out_shapes,
        scratch_shapes=[pltpu.VMEM((dm // LANES, ROW_TILE, LANES), F32)],
        compiler_params=_params(("parallel",)),
        name="odd_proj",
    )(x2, g, w, hg)


def _dil_kernel(q_ref, kp_ref, kc_ref, vp_ref, vc_ref, o_ref, lse_ref):
    n = pl.program_id(2)
    rows = q_ref.shape[2]
    wc = DIL_WC
    i = lax.broadcasted_iota(jnp.int32, (wc, 2 * wc), 0)
    j = lax.broadcasted_iota(jnp.int32, (wc, 2 * wc), 1)
    dist = i + wc - j
    band = (dist >= 0) & (dist <= wc)
    band_first = band & ((j >= wc) | (n > 0))
    ones = jnp.ones((2 * wc, DIL_DH), BF16)
    for jb in range(rows // wc):
        rs = slice(jb * wc, (jb + 1) * wc)
        for h in range(DIL_HEADS):
            hs = slice(h * DIL_DH, (h + 1) * DIL_DH)
            if jb == 0:
                k = jnp.concatenate([kp_ref[0, 0, :, hs], kc_ref[0, 0, 0:wc, hs]], axis=0)
                v = jnp.concatenate([vp_ref[0, 0, :, hs], vc_ref[0, 0, 0:wc, hs]], axis=0)
                valid = band_first
            else:
                k = kc_ref[0, 0, (jb - 1) * wc:(jb + 1) * wc, hs]
                v = vc_ref[0, 0, (jb - 1) * wc:(jb + 1) * wc, hs]
                valid = band
            s = lax.dot_general(q_ref[0, 0, rs, hs], k, (((1,), (1,)), ((), ())),
                                preferred_element_type=F32) * (DIL_DH ** -0.5)
            s = jnp.where(valid, s, NEG_BIG)
            m = jnp.max(s, axis=-1, keepdims=True)
            p = jnp.exp(s - m)
            pv = jnp.dot(p.astype(BF16), jnp.concatenate([v, ones], axis=1),
                         preferred_element_type=F32)
            l = pv[:, DIL_DH:]
            o_ref[0, 0, rs, hs] = (pv[:, 0:DIL_DH] / l).astype(o_ref.dtype)
            lse_ref[0, 0, rs, hs] = m + jnp.log(l)


def _dilated_group(zg):
    b, d, length, _ = zg.shape
    rows = min(length, 512)
    per_row = rows // DIL_WC

    def cur(which):
        return pl.BlockSpec((1, 1, rows, ODD_MIX), lambda bi, r, n: (bi, r, n, which))

    def prev(which):
        return pl.BlockSpec((1, 1, DIL_WC, ODD_MIX),
                            lambda bi, r, n: (bi, r, jnp.maximum(n * per_row - 1, 0), which))

    out_spec = pl.BlockSpec((1, 1, rows, ODD_MIX), lambda bi, r, n: (bi, r, n, 0))
    return pl.pallas_call(
        _dil_kernel,
        grid=(b, d, length // rows),
        in_specs=[cur(0), prev(1), cur(1), prev(2), cur(2)],
        out_specs=[out_spec, out_spec],
        out_shape=[jax.ShapeDtypeStruct((b, d, length, ODD_MIX), BF16),
                   jax.ShapeDtypeStruct((b, d, length, ODD_MIX), F32)],
        compiler_params=_params(("parallel", "parallel", "arbitrary")),
        name=f"dilated_d{d}",
    )(zg, zg, zg, zg, zg)


def _combine_proj_res_kernel(*refs):
    n_groups = len(DIL_PATTERNS)
    o_refs, l_refs = refs[:n_groups], refs[n_groups:2 * n_groups]
    w_ref, x_ref, out_ref, stage_ref = refs[2 * n_groups:]
    tm = x_ref.shape[0]

    def natural(ref, slot):
        d = ref.shape[1]
        if d == 1:
            return ref[0, 0].astype(F32)
        n = tm // d
        n_lane_tiles = stage_ref.shape[1]
        for r in range(d):
            blk = ref[0, r].astype(F32)
            for c in range(n_lane_tiles):
                stage_ref[slot, c, pl.ds(r, n, stride=d), :] = blk[:, c * LANES:(c + 1) * LANES]
        return jnp.concatenate([stage_ref[slot, c] for c in range(n_lane_tiles)], axis=1)

    outs = [natural(r, 2 * gi) for gi, r in enumerate(o_refs)]
    lses = [natural(r, 2 * gi + 1) for gi, r in enumerate(l_refs)]
    m = functools.reduce(jnp.maximum, lses)
    es = [jnp.exp(l - m) for l in lses]
    mix = sum(e * o for e, o in zip(es, outs)) / sum(es)
    out_ref[...] = x_ref[...] + jnp.dot(mix.astype(BF16), w_ref[...], preferred_element_type=F32)


def _combine_proj_res(os_, lses, w, x2, seq):
    t, dm = x2.shape
    tiles = seq // ROW_TILE
    row = pl.BlockSpec((ROW_TILE, dm), lambda i: (i, 0))
    cls = [pl.BlockSpec((1, o.shape[1], ROW_TILE // o.shape[1], ODD_MIX),
                        lambda i: (i // tiles, 0, i % tiles, 0)) for o in os_]
    return pl.pallas_call(
        _combine_proj_res_kernel,
        grid=(t // ROW_TILE,),
        in_specs=cls + cls + [_const_spec(w.shape), row],
        out_specs=row,
        out_shape=jax.ShapeDtypeStruct((t, dm), F32),
        scratch_shapes=[pltpu.VMEM((2 * len(os_), ODD_MIX // LANES, ROW_TILE, LANES), F32)],
        compiler_params=_params(("parallel",)),
        name="combine_proj_res",
    )(*os_, *lses, w, x2)


FFN_CHUNK = 256
SUBLANES = 8
CARRY_ROWS = 2 * SUBLANES
STAGE_PAD_ROWS = 8


def _ffn_kernel(x_ref, g_ref, wup_ref, cw_ref, wdn_ref, o_ref, carry_ref, hs_ref, acc_ref, *, tiles_per_seq):
    tm, dm = x_ref.shape
    ng = tm // SUBLANES
    n_chunks = D_FF // FFN_CHUNK
    n_lane_tiles = dm // LANES

    h = _rms_rows(x_ref[...], g_ref[...])
    pitch = hs_ref.shape[1] // SUBLANES
    for c in range(n_lane_tiles):
        for s in range(SUBLANES):
            hs_ref[c, s * pitch:s * pitch + ng, :] = h[s * ng:(s + 1) * ng, c * LANES:(c + 1) * LANES]
    hb = jnp.concatenate(
        [jnp.concatenate([hs_ref[c, pl.ds(j, SUBLANES, stride=pitch), :] for j in range(ng)], axis=0)
         for c in range(n_lane_tiles)], axis=1).astype(BF16)

    @pl.when(pl.program_id(0) % tiles_per_seq == 0)
    def _():
        carry_ref[...] = jnp.zeros_like(carry_ref)

    last_sublane = lax.broadcasted_iota(jnp.int32, (SUBLANES, FFN_CHUNK), 0) == SUBLANES - 1

    def cols_of(c):
        return (slice(c * FFN_CHUNK, (c + 1) * FFN_CHUNK),
                slice(D_FF + c * FFN_CHUNK, D_FF + (c + 1) * FFN_CHUNK))

    def up(c):
        return [jnp.dot(hb, wup_ref[:, cols], preferred_element_type=F32) for cols in cols_of(c)]

    def wrap(cur, prev):
        return pltpu.roll(jnp.where(last_sublane, prev, cur), 1, 0)

    def conv(u, cols):
        prev = carry_ref[:, cols]
        w2 = wrap(u[tm - 2 * SUBLANES:tm - SUBLANES], prev[0:SUBLANES])
        w1 = wrap(u[tm - SUBLANES:], prev[SUBLANES:])
        u1 = jnp.concatenate([w1, u[:tm - SUBLANES]], axis=0)
        u2 = jnp.concatenate([w2, w1, u[:tm - 2 * SUBLANES]], axis=0)
        carry_ref[:, cols] = u[tm - CARRY_ROWS:]
        cw = cw_ref[:, cols]
        return u2 * cw[0:1] + u1 * cw[1:2] + u * cw[2:3] + cw[3:4]

    u_cur = up(0)
    for c in range(n_chunks):
        u_next = up(c + 1) if c + 1 < n_chunks else None
        gate, val = [conv(u, cols) for u, cols in zip(u_cur, cols_of(c))]
        act = (gate / (1.0 + jnp.exp(-gate)) * val).astype(BF16)
        d = jnp.dot(act, wdn_ref[c * FFN_CHUNK:(c + 1) * FFN_CHUNK, :], preferred_element_type=F32)
        for lt in range(n_lane_tiles):
            piece = d[:, lt * LANES:(lt + 1) * LANES]
            acc_ref[lt] = piece if c == 0 else acc_ref[lt] + piece
        u_cur = u_next

    for lt in range(n_lane_tiles):
        cs = slice(lt * LANES, (lt + 1) * LANES)
        for s in range(SUBLANES):
            rs = slice(s * ng, (s + 1) * ng)
            o_ref[rs, cs] = x_ref[rs, cs] + acc_ref[lt, pl.ds(s, ng, stride=SUBLANES), :]


def _ffn(x2, g, w_up, conv_tab, w_down, seq_len):
    t, d = x2.shape
    row = pl.BlockSpec((ROW_TILE, d), lambda i: (i, 0))
    return pl.pallas_call(
        functools.partial(_ffn_kernel, tiles_per_seq=seq_len // ROW_TILE),
        grid=(t // ROW_TILE,),
        in_specs=[row, _const_spec((1, d)), _const_spec(w_up.shape), _const_spec(conv_tab.shape),
                  _const_spec(w_down.shape)],
        out_specs=row,
        out_shape=jax.ShapeDtypeStruct((t, d), F32),
        scratch_shapes=[pltpu.VMEM((CARRY_ROWS, 2 * D_FF), F32),
                        pltpu.VMEM((d // LANES, ROW_TILE + STAGE_PAD_ROWS * SUBLANES, LANES), F32),
                        pltpu.VMEM((d // LANES, ROW_TILE, LANES), F32)],
        compiler_params=_params(("arbitrary",)),
        name="conv_ffn",
    )(x2, g, w_up, conv_tab, w_down)


def _even_weights(w_in, w_a2, b_a, dq_gain, dk_gain):
    sizes = [A_QK, A_QK, A_V, A_V, GLA_RANK, B_QK, B_QK, B_V]
    offs = [0]
    for c in sizes:
        offs.append(offs[-1] + c)
    aq, ak, av, ag, ar, bq, bk, bv = [w_in[:, offs[i]:offs[i + 1]] for i in range(8)]
    pad = jnp.zeros((w_in.shape[0], LANES - GLA_RANK), w_in.dtype)
    w = jnp.concatenate([aq, ak, av, ag, bq, bk, bv, ar, pad], axis=1).astype(BF16)
    hg = jnp.ones((EV_COLS,), F32)
    hg = hg.at[EV_BQ:EV_BK].set(jnp.tile(dq_gain.astype(F32), B_QK // DIFF_DH) * (DIFF_DH ** -0.5 * LOG2E))
    hg = hg.at[EV_BK:EV_BV].set(jnp.tile(dk_gain.astype(F32), B_QK // DIFF_DH))
    norm_group = tuple(DIFF_DH if EV_BQ <= c * LANES < EV_BV else 0 for c in range(EV_COLS // LANES))
    wa2p = jnp.concatenate([w_a2, jnp.zeros((LANES - GLA_RANK, A_QK), w_a2.dtype)], axis=0).astype(BF16)
    return w, hg.reshape(1, EV_COLS), norm_group, wa2p, b_a.astype(F32).reshape(1, A_QK)


def _odd_gains(q_gain, k_gain):
    hg = jnp.ones((3, 3, ODD_MIX), F32)
    hg = hg.at[:, 0].set(jnp.tile(q_gain.astype(F32), DIL_HEADS))
    hg = hg.at[:, 1].set(jnp.tile(k_gain.astype(F32), DIL_HEADS))
    return hg.reshape(1, ODD_IN)


def kernel(x, norm_mix, norm_ffn, ev_w_in, ev_w_a2, ev_b_a, ev_gla_gain, ev_dq_gain, ev_dk_gain,
           ev_lq1, ev_lk1, ev_lq2, ev_lk2, ev_diff_gain, ev_w_out, od_w_in, od_q_gain, od_k_gain,
           od_w_out, ffn_w_up, ffn_conv_w, ffn_conv_b, ffn_w_down):
    b, s, d = x.shape
    depth = norm_mix.shape[0]
    x2 = x.reshape(b * s, d).astype(F32)
    row = lambda v: v.astype(F32).reshape(1, -1)
    for i in range(depth):
        g_mix = row(norm_mix[i])
        if i % 2 == 0:
            e = i // 2
            lambda_init = 0.8 - 0.6 * math.exp(-0.3 * i)
            w, hg, norm_group, wa2p, ba = _even_weights(ev_w_in[e], ev_w_a2[e], ev_b_a[e],
                                                        ev_dq_gain[e], ev_dk_gain[e])
            z3 = _norm_proj(x2, g_mix, w, hg, norm_group).reshape(b, s, EV_COLS)
            oa = _gla(z3, wa2p, ba, row(ev_gla_gain[e]))
            ob = _diff_attention(z3, row(ev_lq1[e]), row(ev_lk1[e]), row(ev_lq2[e]), row(ev_lk2[e]),
                                 row(ev_diff_gain[e]), lambda_init)
            x2 = _proj2_res(oa.reshape(b * s, A_V), ob.reshape(b * s, B_V),
                            ev_w_out[e].astype(BF16), x2)
        else:
            o = i // 2
            zs = _odd_proj(x2, g_mix, od_w_in[o].astype(BF16), _odd_gains(od_q_gain[o], od_k_gain[o]), b, s)
            outs = [_dilated_group(zg) for zg in zs]
            x2 = _combine_proj_res([o_ for o_, _ in outs], [l_ for _, l_ in outs],
                                   od_w_out[o].astype(BF16), x2, s)
        conv_tab = jnp.concatenate([ffn_conv_w[i].reshape(3, 2 * D_FF), ffn_conv_b[i].reshape(1, 2 * D_FF),
                                    jnp.zeros((4, 2 * D_FF), F32)], axis=0).astype(F32)
        x2 = _ffn(x2, row(norm_ffn[i]), ffn_w_up[i].astype(BF16), conv_tab,
                  ffn_w_down[i].astype(BF16), s)
    return x2.reshape(b, s, d).astype(x.dtype)
```

```python
import functools
import math

import jax
import jax.numpy as jnp
from jax import lax
from jax.experimental import pallas as pl
from jax.experimental.pallas import tpu as pltpu

F32 = jnp.float32
BF16 = jnp.bfloat16

EPS = 1e-6
LANES = 128
MXU_COLS = 256
VMEM_LIMIT = 56 * 1024 * 1024
NEG_BIG = -1e30

D_MODEL = 1024
GLA_HEADS, GLA_DK, GLA_DV, GLA_RANK, GLA_TAU, GLA_CHUNK = 4, 64, 128, 16, 16.0, 64
DIFF_HEADS, DIFF_DH, DIFF_DV = 4, 64, 128
A_QK, A_V, B_QK, B_V = 256, 512, 512, 512
DIL_PATTERNS = ((128, 1), (512, 4), (2048, 16))
DIL_HEADS, DIL_DH = 4, 128
ODD_MIX = DIL_HEADS * DIL_DH
ODD_IN = 3 * 3 * ODD_MIX
D_FF = 2816

EV_AQ, EV_AK, EV_AV, EV_AG, EV_BQ, EV_BK, EV_BV, EV_AR = 0, 256, 512, 1024, 1536, 2048, 2560, 3072
EV_COLS = 3200

ROW_TILE = 512


def _const_spec(shape):
    nd = len(shape)
    return pl.BlockSpec(shape, lambda *_: (0,) * nd)


def _params(sem):
    return pltpu.CompilerParams(dimension_semantics=sem, vmem_limit_bytes=VMEM_LIMIT)


def _rms_rows(x, g):
    return x * lax.rsqrt(jnp.mean(x * x, axis=-1, keepdims=True) + EPS) * g


def _norm_proj_kernel(x_ref, g_ref, w_ref, hg_ref, o_ref, *, norm_group):
    hb = _rms_rows(x_ref[...], g_ref[...]).astype(BF16)
    n_cols = o_ref.shape[1]
    for c0 in range(0, n_cols, MXU_COLS):
        width = min(MXU_COLS, n_cols - c0)
        zw = jnp.dot(hb, w_ref[:, c0:c0 + width], preferred_element_type=F32)
        for sub in range(width // LANES):
            cs = slice(c0 + sub * LANES, c0 + (sub + 1) * LANES)
            z = zw[:, sub * LANES:(sub + 1) * LANES]
            grp = norm_group[cs.start // LANES]
            if grp == LANES:
                ss = jnp.sum(z * z, axis=-1, keepdims=True)
                z = z * lax.rsqrt(ss * (1.0 / LANES) + EPS) * hg_ref[:, cs]
            elif grp == LANES // 2:
                lo = lax.broadcasted_iota(jnp.int32, z.shape, 1) < grp
                sq = z * z
                s_lo = jnp.sum(jnp.where(lo, sq, 0.0), axis=-1, keepdims=True)
                s_hi = jnp.sum(jnp.where(lo, 0.0, sq), axis=-1, keepdims=True)
                ss = jnp.where(lo, s_lo, s_hi)
                z = z * lax.rsqrt(ss * (1.0 / grp) + EPS) * hg_ref[:, cs]
            o_ref[:, cs] = z.astype(o_ref.dtype)


def _norm_proj(x2, g, w, hg, norm_group):
    t, d = x2.shape
    n = w.shape[1]
    return pl.pallas_call(
        functools.partial(_norm_proj_kernel, norm_group=norm_group),
        grid=(t // ROW_TILE,),
        in_specs=[pl.BlockSpec((ROW_TILE, d), lambda i: (i, 0)),
                  _const_spec((1, d)), _const_spec((d, n)), _const_spec((1, n))],
        out_specs=pl.BlockSpec((ROW_TILE, n), lambda i: (i, 0)),
        out_shape=jax.ShapeDtypeStruct((t, n), BF16),
        compiler_params=_params(("parallel",)),
        name="norm_proj",
    )(x2, g, w, hg)


GLA_TILE = 512


def _split3(x):
    hi = x.astype(BF16)
    r1 = x - hi.astype(F32)
    mid = r1.astype(BF16)
    lo = (r1 - mid.astype(F32)).astype(BF16)
    return hi, mid, lo


def _gla_kernel(q_ref, k_ref, v_ref, gate_ref, ar_ref, wa2_ref, ba_ref, gain_ref, o_ref, state_ref):
    C = GLA_CHUNK
    tq = q_ref.shape[1]
    n_chunks = tq // C
    shift = C.bit_length() - 1

    @pl.when(pl.program_id(1) == 0)
    def _():
        state_ref[...] = jnp.zeros_like(state_ref)

    logits = jnp.dot(ar_ref[0], wa2_ref[...], preferred_element_type=F32) + ba_ref[...]
    log_sig = jnp.minimum(logits, 0.0) - jnp.log(1.0 + jnp.exp(-jnp.abs(logits)))
    la = log_sig * (1.0 / GLA_TAU)

    row = lax.broadcasted_iota(jnp.int32, (tq, tq), 0)
    col = lax.broadcasted_iota(jnp.int32, (tq, tq), 1)
    same = (row >> shift) == (col >> shift)
    incl = same & (col <= row)
    t_incl = jnp.where(incl, 1.0, 0.0).astype(BF16)
    b = sum(jnp.dot(t_incl, p, preferred_element_type=F32) for p in _split3(la))
    b_last = jnp.concatenate([jnp.broadcast_to(b[(c + 1) * C - 1:(c + 1) * C, :], (C, b.shape[1]))
                              for c in range(n_chunks)], axis=0)

    q_d = (q_ref[0].astype(F32) * (GLA_DK ** -0.5) * jnp.exp(b)).astype(BF16)
    k_f = k_ref[0].astype(F32)
    k_d = (k_f * jnp.exp(-b)).astype(BF16)
    k_tt = (k_f * jnp.exp(b_last - b)).T
    b_last_t = b_last.T

    gate = gate_ref[0].astype(F32)
    for h in range(GLA_HEADS):
        ks = slice(h * GLA_DK, (h + 1) * GLA_DK)
        vs = slice(h * GLA_DV, (h + 1) * GLA_DV)
        v_h = v_ref[0, :, vs]
        att = lax.dot_general(q_d[:, ks], k_d[:, ks], (((1,), (1,)), ((), ())),
                              preferred_element_type=F32)
        att = jnp.where(incl, att, 0.0).astype(BF16)
        o_h = jnp.dot(att, v_h, preferred_element_type=F32)
        k_blk = jnp.where(same, jnp.tile(k_tt[ks, :], (n_chunks, 1)), 0.0).astype(BF16)
        kv = jnp.dot(k_blk, v_h, preferred_element_type=F32)
        st = state_ref[h]
        inter = []
        for c in range(n_chunks):
            rows = slice(c * C, (c + 1) * C)
            inter.append(jnp.dot(q_d[rows, ks], st.astype(BF16), preferred_element_type=F32))
            dec = jnp.exp(b_last_t[ks, c * C:c * C + 1])
            st = st * dec + kv[c * GLA_DK:(c + 1) * GLA_DK, :]
        state_ref[h] = st
        o_h = o_h + jnp.concatenate(inter, axis=0)
        g = gate[:, vs]
        o_ref[0, :, vs] = (_rms_rows(o_h, gain_ref[...]) * (g / (1.0 + jnp.exp(-g)))).astype(o_ref.dtype)


def _gla(z3, wa2p, ba, gain):
    b, s, _ = z3.shape
    tq = min(GLA_TILE, s)

    def col(width, start):
        return pl.BlockSpec((1, tq, width), lambda bi, i: (bi, i, start // width))

    return pl.pallas_call(
        _gla_kernel,
        grid=(b, s // tq),
        in_specs=[col(A_QK, EV_AQ), col(A_QK, EV_AK), col(A_V, EV_AV), col(A_V, EV_AG),
                  col(LANES, EV_AR), _const_spec((LANES, A_QK)), _const_spec((1, A_QK)),
                  _const_spec((1, GLA_DV))],
        out_specs=pl.BlockSpec((1, tq, A_V), lambda bi, i: (bi, i, 0)),
        out_shape=jax.ShapeDtypeStruct((b, s, A_V), BF16),
        scratch_shapes=[pltpu.VMEM((GLA_HEADS, GLA_DK, GLA_DV), F32)],
        compiler_params=_params(("parallel", "arbitrary")),
        name="gla",
    )(z3, z3, z3, z3, z3, wa2p, ba, gain)


DIFF_TILE_Q = 1024
DIFF_TILE_K = 512
DIFF_ROWS = 256
LOG2E = math.log2(math.e)


def _diff_kernel(qi_ref, ki_ref, q_ref, k_ref, v_ref, lq1_ref, lk1_ref, lq2_ref, lk2_ref, gain_ref,
                 o_ref, qv_ref, m_ref, acc_ref, *, lambda_init):
    t = pl.program_id(2)
    qi = qi_ref[t]
    ki = ki_ref[t]
    tq = q_ref.shape[1]
    tk = k_ref.shape[1]
    rc = DIFF_ROWS

    @pl.when(ki == 0)
    def _():
        q = q_ref[0]
        first = lax.broadcasted_iota(jnp.int32, q.shape, 1) < DIFF_DH
        zero = jnp.zeros_like(q)
        qv_ref[0:tq, :] = jnp.where(first, q, zero)
        qv_ref[tq:2 * tq, :] = jnp.where(first, zero, q)
        m_ref[...] = jnp.full_like(m_ref, NEG_BIG)
        acc_ref[...] = jnp.zeros_like(acc_ref)

    def update(diag):
        k = k_ref[0]
        v1 = jnp.concatenate([v_ref[0], jnp.ones((tk, LANES), BF16)], axis=1)

        def key_width(c):
            if diag is None:
                return tk
            return max(min((c * rc) % tq + rc - diag * tk, tk), 0)

        chunks = [c for c in range(2 * tq // rc) if key_width(c) > 0]

        def scores(c):
            return lax.dot_general(qv_ref[c * rc:(c + 1) * rc, :], k[0:key_width(c)],
                                   (((1,), (1,)), ((), ())), preferred_element_type=F32)

        s_next = scores(chunks[0])
        for idx, c in enumerate(chunks):
            rs = slice(c * rc, (c + 1) * rc)
            q0 = (c * rc) % tq
            kw = key_width(c)
            s = s_next
            s_next = scores(chunks[idx + 1]) if idx + 1 < len(chunks) else None
            if diag is not None:
                r = lax.broadcasted_iota(jnp.int32, (rc, kw), 0) + (q0 - diag * tk)
                col = lax.broadcasted_iota(jnp.int32, (rc, kw), 1)
                s = jnp.where(col <= r, s, NEG_BIG)
            m_prev = m_ref[rs, :]
            m_new = jnp.maximum(m_prev, jnp.max(s, axis=-1, keepdims=True))
            alpha = jnp.exp2(m_prev - m_new)
            p = jnp.exp2(s - jnp.tile(m_new, (1, kw // LANES)))
            pv = jnp.dot(p.astype(BF16), v1[0:kw], preferred_element_type=F32)
            acc_ref[rs, :] = acc_ref[rs, :] * jnp.tile(alpha, (1, 2)) + pv
            m_ref[rs, :] = m_new

    span = tq // tk
    ki_rel = ki - qi * span

    @pl.when(ki_rel < 0)
    def _():
        update(None)

    for diag in range(span - 1):
        @pl.when(ki_rel == diag)
        def _(diag=diag):
            update(diag)

    @pl.when(ki_rel == span - 1)
    def _():
        update(span - 1)
        lam = (jnp.exp(jnp.sum(lq1_ref[...] * lk1_ref[...], axis=-1, keepdims=True))
               - jnp.exp(jnp.sum(lq2_ref[...] * lk2_ref[...], axis=-1, keepdims=True))
               + lambda_init)
        o1 = acc_ref[0:tq, 0:DIFF_DV] / acc_ref[0:tq, DIFF_DV:]
        o2 = acc_ref[tq:2 * tq, 0:DIFF_DV] / acc_ref[tq:2 * tq, DIFF_DV:]
        o = _rms_rows(o1 - lam * o2, gain_ref[...]) * (1.0 - lambda_init)
        o_ref[0] = o.astype(o_ref.dtype)


def _diff_attention(z3, lq1, lk1, lq2, lk2, gain, lambda_init):
    b, s, _ = z3.shape
    tq = min(DIFF_TILE_Q, s)
    tk = min(DIFF_TILE_K, s)
    span = tq // tk
    pairs = [(qi, ki) for qi in range(s // tq) for ki in range(span * (qi + 1))]
    qi_tab = jnp.asarray([p[0] for p in pairs], jnp.int32)
    ki_tab = jnp.asarray([p[1] for p in pairs], jnp.int32)
    qc, kc, vc = EV_BQ // LANES, EV_BK // LANES, EV_BV // LANES

    grid_spec = pltpu.PrefetchScalarGridSpec(
        num_scalar_prefetch=2,
        grid=(b, DIFF_HEADS, len(pairs)),
        in_specs=[pl.BlockSpec((1, tq, LANES), lambda bi, h, t, qt, kt: (bi, qt[t], qc + h)),
                  pl.BlockSpec((1, tk, LANES), lambda bi, h, t, qt, kt: (bi, kt[t], kc + h)),
                  pl.BlockSpec((1, tk, LANES), lambda bi, h, t, qt, kt: (bi, kt[t], vc + h)),
                  _const_spec((1, DIFF_DH)), _const_spec((1, DIFF_DH)),
                  _const_spec((1, DIFF_DH)), _const_spec((1, DIFF_DH)),
                  _const_spec((1, DIFF_DV))],
        out_specs=pl.BlockSpec((1, tq, LANES), lambda bi, h, t, qt, kt: (bi, qt[t], h)),
        scratch_shapes=[pltpu.VMEM((2 * tq, LANES), BF16), pltpu.VMEM((2 * tq, LANES), F32),
                        pltpu.VMEM((2 * tq, 2 * DIFF_DV), F32)],
    )
    return pl.pallas_call(
        functools.partial(_diff_kernel, lambda_init=lambda_init),
        grid_spec=grid_spec,
        out_shape=jax.ShapeDtypeStruct((b, s, B_V), BF16),
        compiler_params=_params(("parallel", "parallel", "arbitrary")),
        name="diff_attention",
    )(qi_tab, ki_tab, z3, z3, z3, lq1, lk1, lq2, lk2, gain)


def _proj2_res_kernel(a_ref, b_ref, w_ref, x_ref, o_ref):
    ka = a_ref.shape[1]
    o_ref[...] = (x_ref[...]
                  + jnp.dot(a_ref[...], w_ref[0:ka, :], preferred_element_type=F32)
                  + jnp.dot(b_ref[...], w_ref[ka:, :], preferred_element_type=F32))


def _proj2_res(a, b, w, x2):
    t, d = x2.shape
    row = lambda width: pl.BlockSpec((ROW_TILE, width), lambda i: (i, 0))
    return pl.pallas_call(
        _proj2_res_kernel,
        grid=(t // ROW_TILE,),
        in_specs=[row(a.shape[1]), row(b.shape[1]), _const_spec(w.shape), row(d)],
        out_specs=row(d),
        out_shape=jax.ShapeDtypeStruct((t, d), F32),
        compiler_params=_params(("parallel",)),
        name="proj2_res",
    )(a, b, w, x2)


DIL_WC = 128
GROUP_COLS = 3 * ODD_MIX


def _odd_proj_kernel(x_ref, g_ref, w_ref, hg_ref, *refs):
    out_refs, h_ref = refs[:-1], refs[-1]
    tm = x_ref.shape[0]
    h = _rms_rows(x_ref[...], g_ref[...])
    n_lane_tiles = h_ref.shape[0]
    for c in range(n_lane_tiles):
        h_ref[c] = h[:, c * LANES:(c + 1) * LANES]
    for gi, o_ref in enumerate(out_refs):
        d = o_ref.shape[1]
        n = tm // d
        if d == 1:
            hb = h.astype(BF16)
        else:
            hb = jnp.concatenate(
                [jnp.concatenate([h_ref[c, pl.ds(r, n, stride=d), :] for r in range(d)], axis=0)
                 for c in range(n_lane_tiles)], axis=1).astype(BF16)
        for c0 in range(0, GROUP_COLS, MXU_COLS):
            zw = jnp.dot(hb, w_ref[:, gi * GROUP_COLS + c0:gi * GROUP_COLS + c0 + MXU_COLS],
                         preferred_element_type=F32)
            for sub in range(MXU_COLS // LANES):
                col = c0 + sub * LANES
                cs = slice(gi * GROUP_COLS + col, gi * GROUP_COLS + col + LANES)
                z = zw[:, sub * LANES:(sub + 1) * LANES]
                if col < 2 * ODD_MIX:
                    ss = jnp.sum(z * z, axis=-1, keepdims=True)
                    z = z * lax.rsqrt(ss * (1.0 / LANES) + EPS) * hg_ref[:, cs]
                z = z.astype(o_ref.dtype)
                for r in range(d):
                    o_ref[0, r, :, col:col + LANES] = z[r * n:(r + 1) * n]


def _odd_proj(x2, g, w, hg, batch, seq):
    t, dm = x2.shape
    tiles = seq // ROW_TILE
    out_specs, out_shapes = [], []
    for _, d in DIL_PATTERNS:
        out_specs.append(pl.BlockSpec((1, d, ROW_TILE // d, GROUP_COLS),
                                      lambda i: (i // tiles, 0, i % tiles, 0)))
        out_shapes.append(jax.ShapeDtypeStruct((batch, d, seq // d, GROUP_COLS), BF16))
    return pl.pallas_call(
        _odd_proj_kernel,
        grid=(t // ROW_TILE,),
        in_specs=[pl.BlockSpec((ROW_TILE, dm), lambda i: (i, 0)),
                  _const_spec((1, dm)), _const_spec(w.shape), _const_spec(hg.shape)],
        out_specs=out_specs,
        out_shape=out_shapes,
        scratch_shapes=[pltpu.VMEM((dm // LANES, ROW_TILE, LANES), F32)],
        compiler_params=_params(("parallel",)),
        name="odd_proj",
    )(x2, g, w, hg)


def _dil_kernel(q_ref, kp_ref, kc_ref, vp_ref, vc_ref, o_ref, lse_ref):
    n = pl.program_id(2)
    rows = q_ref.shape[2]
    wc = DIL_WC
    i = lax.broadcasted_iota(jnp.int32, (wc, 2 * wc), 0)
    j = lax.broadcasted_iota(jnp.int32, (wc, 2 * wc), 1)
    dist = i + wc - j
    band = (dist >= 0) & (dist <= wc)
    band_first = band & ((j >= wc) | (n > 0))
    ones = jnp.ones((2 * wc, DIL_DH), BF16)
    for jb in range(rows // wc):
        rs = slice(jb * wc, (jb + 1) * wc)
        for h in range(DIL_HEADS):
            hs = slice(h * DIL_DH, (h + 1) * DIL_DH)
            if jb == 0:
                k = jnp.concatenate([kp_ref[0, 0, :, hs], kc_ref[0, 0, 0:wc, hs]], axis=0)
                v = jnp.concatenate([vp_ref[0, 0, :, hs], vc_ref[0, 0, 0:wc, hs]], axis=0)
                valid = band_first
            else:
                k = kc_ref[0, 0, (jb - 1) * wc:(jb + 1) * wc, hs]
                v = vc_ref[0, 0, (jb - 1) * wc:(jb + 1) * wc, hs]
                valid = band
            s = lax.dot_general(q_ref[0, 0, rs, hs], k, (((1,), (1,)), ((), ())),
                                preferred_element_type=F32) * (DIL_DH ** -0.5)
            s = jnp.where(valid, s, NEG_BIG)
            m = jnp.max(s, axis=-1, keepdims=True)
            p = jnp.exp(s - m)
            pv = jnp.dot(p.astype(BF16), jnp.concatenate([v, ones], axis=1),
                         preferred_element_type=F32)
            l = pv[:, DIL_DH:]
            o_ref[0, 0, rs, hs] = (pv[:, 0:DIL_DH] / l).astype(o_ref.dtype)
            lse_ref[0, 0, rs, hs] = m + jnp.log(l)


def _dilated_group(zg):
    b, d, length, _ = zg.shape
    rows = min(length, 512)
    per_row = rows // DIL_WC

    def cur(which):
        return pl.BlockSpec((1, 1, rows, ODD_MIX), lambda bi, r, n: (bi, r, n, which))

    def prev(which):
        return pl.BlockSpec((1, 1, DIL_WC, ODD_MIX),
                            lambda bi, r, n: (bi, r, jnp.maximum(n * per_row - 1, 0), which))

    out_spec = pl.BlockSpec((1, 1, rows, ODD_MIX), lambda bi, r, n: (bi, r, n, 0))
    return pl.pallas_call(
        _dil_kernel,
        grid=(b, d, length // rows),
        in_specs=[cur(0), prev(1), cur(1), prev(2), cur(2)],
        out_specs=[out_spec, out_spec],
        out_shape=[jax.ShapeDtypeStruct((b, d, length, ODD_MIX), BF16),
                   jax.ShapeDtypeStruct((b, d, length, ODD_MIX), F32)],
        compiler_params=_params(("parallel", "parallel", "arbitrary")),
        name=f"dilated_d{d}",
    )(zg, zg, zg, zg, zg)


def _combine_proj_res_kernel(*refs):
    n_groups = len(DIL_PATTERNS)
    o_refs, l_refs = refs[:n_groups], refs[n_groups:2 * n_groups]
    w_ref, x_ref, out_ref, stage_ref = refs[2 * n_groups:]
    tm = x_ref.shape[0]

    def natural(ref, slot):
        d = ref.shape[1]
        if d == 1:
            return ref[0, 0].astype(F32)
        n = tm // d
        n_lane_tiles = stage_ref.shape[1]
        for r in range(d):
            blk = ref[0, r].astype(F32)
            for c in range(n_lane_tiles):
                stage_ref[slot, c, pl.ds(r, n, stride=d), :] = blk[:, c * LANES:(c + 1) * LANES]
        return jnp.concatenate([stage_ref[slot, c] for c in range(n_lane_tiles)], axis=1)

    outs = [natural(r, 2 * gi) for gi, r in enumerate(o_refs)]
    lses = [natural(r, 2 * gi + 1) for gi, r in enumerate(l_refs)]
    m = functools.reduce(jnp.maximum, lses)
    es = [jnp.exp(l - m) for l in lses]
    mix = sum(e * o for e, o in zip(es, outs)) / sum(es)
    out_ref[...] = x_ref[...] + jnp.dot(mix.astype(BF16), w_ref[...], preferred_element_type=F32)


def _combine_proj_res(os_, lses, w, x2, seq):
    t, dm = x2.shape
    tiles = seq // ROW_TILE
    row = pl.BlockSpec((ROW_TILE, dm), lambda i: (i, 0))
    cls = [pl.BlockSpec((1, o.shape[1], ROW_TILE // o.shape[1], ODD_MIX),
                        lambda i: (i // tiles, 0, i % tiles, 0)) for o in os_]
    return pl.pallas_call(
        _combine_proj_res_kernel,
        grid=(t // ROW_TILE,),
        in_specs=cls + cls + [_const_spec(w.shape), row],
        out_specs=row,
        out_shape=jax.ShapeDtypeStruct((t, dm), F32),
        scratch_shapes=[pltpu.VMEM((2 * len(os_), ODD_MIX // LANES, ROW_TILE, LANES), F32)],
        compiler_params=_params(("parallel",)),
        name="combine_proj_res",
    )(*os_, *lses, w, x2)


FFN_CHUNK = 256
SUBLANES = 8
CARRY_ROWS = 2 * SUBLANES
STAGE_PAD_ROWS = 8


def _ffn_kernel(x_ref, g_ref, wup_ref, cw_ref, wdn_ref, o_ref, carry_ref, hs_ref, acc_ref, *, tiles_per_seq):
    tm, dm = x_ref.shape
    ng = tm // SUBLANES
    n_chunks = D_FF // FFN_CHUNK
    n_lane_tiles = dm // LANES

    h = _rms_rows(x_ref[...], g_ref[...])
    pitch = hs_ref.shape[1] // SUBLANES
    for c in range(n_lane_tiles):
        for s in range(SUBLANES):
            hs_ref[c, s * pitch:s * pitch + ng, :] = h[s * ng:(s + 1) * ng, c * LANES:(c + 1) * LANES]
    hb = jnp.concatenate(
        [jnp.concatenate([hs_ref[c, pl.ds(j, SUBLANES, stride=pitch), :] for j in range(ng)], axis=0)
         for c in range(n_lane_tiles)], axis=1).astype(BF16)

    @pl.when(pl.program_id(0) % tiles_per_seq == 0)
    def _():
        carry_ref[...] = jnp.zeros_like(carry_ref)

    last_sublane = lax.broadcasted_iota(jnp.int32, (SUBLANES, FFN_CHUNK), 0) == SUBLANES - 1

    def cols_of(c):
        return (slice(c * FFN_CHUNK, (c + 1) * FFN_CHUNK),
                slice(D_FF + c * FFN_CHUNK, D_FF + (c + 1) * FFN_CHUNK))

    def up(c):
        return [jnp.dot(hb, wup_ref[:, cols], preferred_element_type=F32) for cols in cols_of(c)]

    def wrap(cur, prev):
        return pltpu.roll(jnp.where(last_sublane, prev, cur), 1, 0)

    def conv(u, cols):
        prev = carry_ref[:, cols]
        w2 = wrap(u[tm - 2 * SUBLANES:tm - SUBLANES], prev[0:SUBLANES])
        w1 = wrap(u[tm - SUBLANES:], prev[SUBLANES:])
        u1 = jnp.concatenate([w1, u[:tm - SUBLANES]], axis=0)
        u2 = jnp.concatenate([w2, w1, u[:tm - 2 * SUBLANES]], axis=0)
        carry_ref[:, cols] = u[tm - CARRY_ROWS:]
        cw = cw_ref[:, cols]
        return u2 * cw[0:1] + u1 * cw[1:2] + u * cw[2:3] + cw[3:4]

    u_cur = up(0)
    for c in range(n_chunks):
        u_next = up(c + 1) if c + 1 < n_chunks else None
        gate, val = [conv(u, cols) for u, cols in zip(u_cur, cols_of(c))]
        act = (gate / (1.0 + jnp.exp(-gate)) * val).astype(BF16)
        d = jnp.dot(act, wdn_ref[c * FFN_CHUNK:(c + 1) * FFN_CHUNK, :], preferred_element_type=F32)
        for lt in range(n_lane_tiles):
            piece = d[:, lt * LANES:(lt + 1) * LANES]
            acc_ref[lt] = piece if c == 0 else acc_ref[lt] + piece
        u_cur = u_next

    for lt in range(n_lane_tiles):
        cs = slice(lt * LANES, (lt + 1) * LANES)
        for s in range(SUBLANES):
            rs = slice(s * ng, (s + 1) * ng)
            o_ref[rs, cs] = x_ref[rs, cs] + acc_ref[lt, pl.ds(s, ng, stride=SUBLANES), :]


def _ffn(x2, g, w_up, conv_tab, w_down, seq_len):
    t, d = x2.shape
    row = pl.BlockSpec((ROW_TILE, d), lambda i: (i, 0))
    return pl.pallas_call(
        functools.partial(_ffn_kernel, tiles_per_seq=seq_len // ROW_TILE),
        grid=(t // ROW_TILE,),
        in_specs=[row, _const_spec((1, d)), _const_spec(w_up.shape), _const_spec(conv_tab.shape),
                  _const_spec(w_down.shape)],
        out_specs=row,
        out_shape=jax.ShapeDtypeStruct((t, d), F32),
        scratch_shapes=[pltpu.VMEM((CARRY_ROWS, 2 * D_FF), F32),
                        pltpu.VMEM((d // LANES, ROW_TILE + STAGE_PAD_ROWS * SUBLANES, LANES), F32),
                        pltpu.VMEM((d // LANES, ROW_TILE, LANES), F32)],
        compiler_params=_params(("arbitrary",)),
        name="conv_ffn",
    )(x2, g, w_up, conv_tab, w_down)


def _even_weights(w_in, w_a2, b_a, dq_gain, dk_gain):
    sizes = [A_QK, A_QK, A_V, A_V, GLA_RANK, B_QK, B_QK, B_V]
    offs = [0]
    for c in sizes:
        offs.append(offs[-1] + c)
    aq, ak, av, ag, ar, bq, bk, bv = [w_in[:, offs[i]:offs[i + 1]] for i in range(8)]
    pad = jnp.zeros((w_in.shape[0], LANES - GLA_RANK), w_in.dtype)
    w = jnp.concatenate([aq, ak, av, ag, bq, bk, bv, ar, pad], axis=1).astype(BF16)
    hg = jnp.ones((EV_COLS,), F32)
    hg = hg.at[EV_BQ:EV_BK].set(jnp.tile(dq_gain.astype(F32), B_QK // DIFF_DH) * (DIFF_DH ** -0.5 * LOG2E))
    hg = hg.at[EV_BK:EV_BV].set(jnp.tile(dk_gain.astype(F32), B_QK // DIFF_DH))
    norm_group = tuple(DIFF_DH if EV_BQ <= c * LANES < EV_BV else 0 for c in range(EV_COLS // LANES))
    wa2p = jnp.concatenate([w_a2, jnp.zeros((LANES - GLA_RANK, A_QK), w_a2.dtype)], axis=0).astype(BF16)
    return w, hg.reshape(1, EV_COLS), norm_group, wa2p, b_a.astype(F32).reshape(1, A_QK)


def _odd_gains(q_gain, k_gain):
    hg = jnp.ones((3, 3, ODD_MIX), F32)
    hg = hg.at[:, 0].set(jnp.tile(q_gain.astype(F32), DIL_HEADS))
    hg = hg.at[:, 1].set(jnp.tile(k_gain.astype(F32), DIL_HEADS))
    return hg.reshape(1, ODD_IN)


def kernel(x, norm_mix, norm_ffn, ev_w_in, ev_w_a2, ev_b_a, ev_gla_gain, ev_dq_gain, ev_dk_gain,
           ev_lq1, ev_lk1, ev_lq2, ev_lk2, ev_diff_gain, ev_w_out, od_w_in, od_q_gain, od_k_gain,
           od_w_out, ffn_w_up, ffn_conv_w, ffn_conv_b, ffn_w_down):
    b, s, d = x.shape
    depth = norm_mix.shape[0]
    x2 = x.reshape(b * s, d).astype(F32)
    row = lambda v: v.astype(F32).reshape(1, -1)
    for i in range(depth):
        g_mix = row(norm_mix[i])
        if i % 2 == 0:
            e = i // 2
            lambda_init = 0.8 - 0.6 * math.exp(-0.3 * i)
            w, hg, norm_group, wa2p, ba = _even_weights(ev_w_in[e], ev_w_a2[e], ev_b_a[e],
                                                        ev_dq_gain[e], ev_dk_gain[e])
            z3 = _norm_proj(x2, g_mix, w, hg, norm_group).reshape(b, s, EV_COLS)
            oa = _gla(z3, wa2p, ba, row(ev_gla_gain[e]))
            ob = _diff_attention(z3, row(ev_lq1[e]), row(ev_lk1[e]), row(ev_lq2[e]), row(ev_lk2[e]),
                                 row(ev_diff_gain[e]), lambda_init)
            x2 = _proj2_res(oa.reshape(b * s, A_V), ob.reshape(b * s, B_V),
                            ev_w_out[e].astype(BF16), x2)
        else:
            o = i // 2
            zs = _odd_proj(x2, g_mix, od_w_in[o].astype(BF16), _odd_gains(od_q_gain[o], od_k_gain[o]), b, s)
            outs = [_dilated_group(zg) for zg in zs]
            x2 = _combine_proj_res([o_ for o_, _ in outs], [l_ for _, l_ in outs],
                                   od_w_out[o].astype(BF16), x2, s)
        conv_tab = jnp.concatenate([ffn_conv_w[i].reshape(3, 2 * D_FF), ffn_conv_b[i].reshape(1, 2 * D_FF),
                                    jnp.zeros((4, 2 * D_FF), F32)], axis=0).astype(F32)
        x2 = _ffn(x2, row(norm_ffn[i]), ffn_w_up[i].astype(BF16), conv_tab,
                  ffn_w_down[i].astype(BF16), s)
    return x2.reshape(b, s, d).astype(x.dtype)
```

```python
import functools
import math

import jax
import jax.numpy as jnp
from jax import lax
from jax.experimental import pallas as pl
from jax.experimental.pallas import tpu as pltpu

F32 = jnp.float32
BF16 = jnp.bfloat16

EPS = 1e-6
LANES = 128
MXU_COLS = 256
VMEM_LIMIT = 56 * 1024 * 1024
NEG_BIG = -1e30

D_MODEL = 1024
GLA_HEADS, GLA_DK, GLA_DV, GLA_RANK, GLA_TAU, GLA_CHUNK = 4, 64, 128, 16, 16.0, 64
DIFF_HEADS, DIFF_DH, DIFF_DV = 4, 64, 128
A_QK, A_V, B_QK, B_V = 256, 512, 512, 512
DIL_PATTERNS = ((128, 1), (512, 4), (2048, 16))
DIL_HEADS, DIL_DH = 4, 128
ODD_MIX = DIL_HEADS * DIL_DH
ODD_IN = 3 * 3 * ODD_MIX
D_FF = 2816

EV_AQ, EV_AK, EV_AV, EV_AG, EV_BQ, EV_BK, EV_BV, EV_AR = 0, 256, 512, 1024, 1536, 2048, 2560, 3072
EV_COLS = 3200

ROW_TILE = 512


def _const_spec(shape):
    nd = len(shape)
    return pl.BlockSpec(shape, lambda *_: (0,) * nd)


def _params(sem):
    return pltpu.CompilerParams(dimension_semantics=sem, vmem_limit_bytes=VMEM_LIMIT)


def _rms_rows(x, g):
    return x * lax.rsqrt(jnp.mean(x * x, axis=-1, keepdims=True) + EPS) * g


def _norm_proj_kernel(x_ref, g_ref, w_ref, hg_ref, o_ref, *, norm_group):
    hb = _rms_rows(x_ref[...], g_ref[...]).astype(BF16)
    n_cols = o_ref.shape[1]
    for c0 in range(0, n_cols, MXU_COLS):
        width = min(MXU_COLS, n_cols - c0)
        zw = jnp.dot(hb, w_ref[:, c0:c0 + width], preferred_element_type=F32)
        for sub in range(width // LANES):
            cs = slice(c0 + sub * LANES, c0 + (sub + 1) * LANES)
            z = zw[:, sub * LANES:(sub + 1) * LANES]
            grp = norm_group[cs.start // LANES]
            if grp == LANES:
                ss = jnp.sum(z * z, axis=-1, keepdims=True)
                z = z * lax.rsqrt(ss * (1.0 / LANES) + EPS) * hg_ref[:, cs]
            elif grp == LANES // 2:
                lo = lax.broadcasted_iota(jnp.int32, z.shape, 1) < grp
                sq = z * z
                s_lo = jnp.sum(jnp.where(lo, sq, 0.0), axis=-1, keepdims=True)
                s_hi = jnp.sum(jnp.where(lo, 0.0, sq), axis=-1, keepdims=True)
                ss = jnp.where(lo, s_lo, s_hi)
                z = z * lax.rsqrt(ss * (1.0 / grp) + EPS) * hg_ref[:, cs]
            o_ref[:, cs] = z.astype(o_ref.dtype)


def _norm_proj(x2, g, w, hg, norm_group):
    t, d = x2.shape
    n = w.shape[1]
    return pl.pallas_call(
        functools.partial(_norm_proj_kernel, norm_group=norm_group),
        grid=(t // ROW_TILE,),
        in_specs=[pl.BlockSpec((ROW_TILE, d), lambda i: (i, 0)),
                  _const_spec((1, d)), _const_spec((d, n)), _const_spec((1, n))],
        out_specs=pl.BlockSpec((ROW_TILE, n), lambda i: (i, 0)),
        out_shape=jax.ShapeDtypeStruct((t, n), BF16),
        compiler_params=_params(("parallel",)),
        name="norm_proj",
    )(x2, g, w, hg)


GLA_TILE = 512


def _split3(x):
    hi = x.astype(BF16)
    r1 = x - hi.astype(F32)
    mid = r1.astype(BF16)
    lo = (r1 - mid.astype(F32)).astype(BF16)
    return hi, mid, lo


def _gla_kernel(q_ref, k_ref, v_ref, gate_ref, ar_ref, wa2_ref, ba_ref, gain_ref, o_ref, state_ref):
    C = GLA_CHUNK
    tq = q_ref.shape[1]
    n_chunks = tq // C
    shift = C.bit_length() - 1

    @pl.when(pl.program_id(1) == 0)
    def _():
        state_ref[...] = jnp.zeros_like(state_ref)

    logits = jnp.dot(ar_ref[0], wa2_ref[...], preferred_element_type=F32) + ba_ref[...]
    log_sig = jnp.minimum(logits, 0.0) - jnp.log(1.0 + jnp.exp(-jnp.abs(logits)))
    la = log_sig * (1.0 / GLA_TAU)

    row = lax.broadcasted_iota(jnp.int32, (tq, tq), 0)
    col = lax.broadcasted_iota(jnp.int32, (tq, tq), 1)
    same = (row >> shift) == (col >> shift)
    incl = same & (col <= row)
    t_incl = jnp.where(incl, 1.0, 0.0).astype(BF16)
    b = sum(jnp.dot(t_incl, p, preferred_element_type=F32) for p in _split3(la))
    b_last = jnp.concatenate([jnp.broadcast_to(b[(c + 1) * C - 1:(c + 1) * C, :], (C, b.shape[1]))
                              for c in range(n_chunks)], axis=0)

    q_d = (q_ref[0].astype(F32) * (GLA_DK ** -0.5) * jnp.exp(b)).astype(BF16)
    k_f = k_ref[0].astype(F32)
    k_d = (k_f * jnp.exp(-b)).astype(BF16)
    k_tt = (k_f * jnp.exp(b_last - b)).T
    b_last_t = b_last.T

    gate = gate_ref[0].astype(F32)
    for h in range(GLA_HEADS):
        ks = slice(h * GLA_DK, (h + 1) * GLA_DK)
        vs = slice(h * GLA_DV, (h + 1) * GLA_DV)
        v_h = v_ref[0, :, vs]
        att = lax.dot_general(q_d[:, ks], k_d[:, ks], (((1,), (1,)), ((), ())),
                              preferred_element_type=F32)
        att = jnp.where(incl, att, 0.0).astype(BF16)
        o_h = jnp.dot(att, v_h, preferred_element_type=F32)
        k_blk = jnp.where(same, jnp.tile(k_tt[ks, :], (n_chunks, 1)), 0.0).astype(BF16)
        kv = jnp.dot(k_blk, v_h, preferred_element_type=F32)
        st = state_ref[h]
        inter = []
        for c in range(n_chunks):
            rows = slice(c * C, (c + 1) * C)
            inter.append(jnp.dot(q_d[rows, ks], st.astype(BF16), preferred_element_type=F32))
            dec = jnp.exp(b_last_t[ks, c * C:c * C + 1])
            st = st * dec + kv[c * GLA_DK:(c + 1) * GLA_DK, :]
        state_ref[h] = st
        o_h = o_h + jnp.concatenate(inter, axis=0)
        g = gate[:, vs]
        o_ref[0, :, vs] = (_rms_rows(o_h, gain_ref[...]) * (g / (1.0 + jnp.exp(-g)))).astype(o_ref.dtype)


def _gla(z3, wa2p, ba, gain):
    b, s, _ = z3.shape
    tq = min(GLA_TILE, s)

    def col(width, start):
        return pl.BlockSpec((1, tq, width), lambda bi, i: (bi, i, start // width))

    return pl.pallas_call(
        _gla_kernel,
        grid=(b, s // tq),
        in_specs=[col(A_QK, EV_AQ), col(A_QK, EV_AK), col(A_V, EV_AV), col(A_V, EV_AG),
                  col(LANES, EV_AR), _const_spec((LANES, A_QK)), _const_spec((1, A_QK)),
                  _const_spec((1, GLA_DV))],
        out_specs=pl.BlockSpec((1, tq, A_V), lambda bi, i: (bi, i, 0)),
        out_shape=jax.ShapeDtypeStruct((b, s, A_V), BF16),
        scratch_shapes=[pltpu.VMEM((GLA_HEADS, GLA_DK, GLA_DV), F32)],
        compiler_params=_params(("parallel", "arbitrary")),
        name="gla",
    )(z3, z3, z3, z3, z3, wa2p, ba, gain)


DIFF_TILE_Q = 1024
DIFF_TILE_K = 512
DIFF_ROWS = 256
LOG2E = math.log2(math.e)


def _diff_kernel(q_ref, k_ref, v_ref, lq1_ref, lk1_ref, lq2_ref, lk2_ref, gain_ref,
                 o_ref, qv_ref, m_ref, acc_ref, *, lambda_init, tk):
    qi = pl.program_id(2)
    tq = q_ref.shape[1]
    rc = DIFF_ROWS
    span = tq // tk

    q = q_ref[0]
    first = lax.broadcasted_iota(jnp.int32, q.shape, 1) < DIFF_DH
    zero = jnp.zeros_like(q)
    qv_ref[0:tq, :] = jnp.where(first, q, zero)
    qv_ref[tq:2 * tq, :] = jnp.where(first, zero, q)
    m_ref[...] = jnp.full_like(m_ref, NEG_BIG)
    acc_ref[...] = jnp.zeros_like(acc_ref)

    def update(block, diag):
        rows = pl.ds(pl.multiple_of(block * tk, tk), tk)
        k = k_ref[0, rows, :]
        v1 = jnp.concatenate([v_ref[0, rows, :], jnp.ones((tk, LANES), BF16)], axis=1)

        def key_width(c):
            if diag is None:
                return tk
            return max(min((c * rc) % tq + rc - diag * tk, tk), 0)

        chunks = [c for c in range(2 * tq // rc) if key_width(c) > 0]

        def scores(c):
            return lax.dot_general(qv_ref[c * rc:(c + 1) * rc, :], k[0:key_width(c)],
                                   (((1,), (1,)), ((), ())), preferred_element_type=F32)

        s_next = scores(chunks[0])
        for idx, c in enumerate(chunks):
            rs = slice(c * rc, (c + 1) * rc)
            q0 = (c * rc) % tq
            kw = key_width(c)
            s = s_next
            s_next = scores(chunks[idx + 1]) if idx + 1 < len(chunks) else None
            if diag is not None:
                r = lax.broadcasted_iota(jnp.int32, (rc, kw), 0) + (q0 - diag * tk)
                col = lax.broadcasted_iota(jnp.int32, (rc, kw), 1)
                s = jnp.where(col <= r, s, NEG_BIG)
            m_prev = m_ref[rs, :]
            m_new = jnp.maximum(m_prev, jnp.max(s, axis=-1, keepdims=True))
            alpha = jnp.exp2(m_prev - m_new)
            p = jnp.exp2(s - jnp.tile(m_new, (1, kw // LANES)))
            pv = jnp.dot(p.astype(BF16), v1[0:kw], preferred_element_type=F32)
            acc_ref[rs, :] = acc_ref[rs, :] * jnp.tile(alpha, (1, 2)) + pv
            m_ref[rs, :] = m_new

    def below_diagonal(block, carry):
        update(block, None)
        return carry

    lax.fori_loop(0, qi * span, below_diagonal, 0)
    for diag in range(span):
        update(qi * span + diag, diag)

    lam = (jnp.exp(jnp.sum(lq1_ref[...] * lk1_ref[...], axis=-1, keepdims=True))
           - jnp.exp(jnp.sum(lq2_ref[...] * lk2_ref[...], axis=-1, keepdims=True))
           + lambda_init)
    o1 = acc_ref[0:tq, 0:DIFF_DV] / acc_ref[0:tq, DIFF_DV:]
    o2 = acc_ref[tq:2 * tq, 0:DIFF_DV] / acc_ref[tq:2 * tq, DIFF_DV:]
    o = _rms_rows(o1 - lam * o2, gain_ref[...]) * (1.0 - lambda_init)
    o_ref[0] = o.astype(o_ref.dtype)


def _diff_attention(z3, lq1, lk1, lq2, lk2, gain, lambda_init):
    b, s, _ = z3.shape
    tq = min(DIFF_TILE_Q, s)
    tk = min(DIFF_TILE_K, s)
    qc, kc, vc = EV_BQ // LANES, EV_BK // LANES, EV_BV // LANES
    return pl.pallas_call(
        functools.partial(_diff_kernel, lambda_init=lambda_init, tk=tk),
        grid=(b, DIFF_HEADS, s // tq),
        in_specs=[pl.BlockSpec((1, tq, LANES), lambda bi, h, qi: (bi, qi, qc + h)),
                  pl.BlockSpec((1, s, LANES), lambda bi, h, qi: (bi, 0, kc + h)),
                  pl.BlockSpec((1, s, LANES), lambda bi, h, qi: (bi, 0, vc + h)),
                  _const_spec((1, DIFF_DH)), _const_spec((1, DIFF_DH)),
                  _const_spec((1, DIFF_DH)), _const_spec((1, DIFF_DH)),
                  _const_spec((1, DIFF_DV))],
        out_specs=pl.BlockSpec((1, tq, LANES), lambda bi, h, qi: (bi, qi, h)),
        out_shape=jax.ShapeDtypeStruct((b, s, B_V), BF16),
        scratch_shapes=[pltpu.VMEM((2 * tq, LANES), BF16), pltpu.VMEM((2 * tq, LANES), F32),
                        pltpu.VMEM((2 * tq, 2 * DIFF_DV), F32)],
        compiler_params=_params(("parallel", "parallel", "arbitrary")),
        name="diff_attention",
    )(z3, z3, z3, lq1, lk1, lq2, lk2, gain)


def _proj2_res_kernel(a_ref, b_ref, w_ref, x_ref, o_ref):
    ka = a_ref.shape[1]
    o_ref[...] = (x_ref[...]
                  + jnp.dot(a_ref[...], w_ref[0:ka, :], preferred_element_type=F32)
                  + jnp.dot(b_ref[...], w_ref[ka:, :], preferred_element_type=F32))


def _proj2_res(a, b, w, x2):
    t, d = x2.shape
    row = lambda width: pl.BlockSpec((ROW_TILE, width), lambda i: (i, 0))
    return pl.pallas_call(
        _proj2_res_kernel,
        grid=(t // ROW_TILE,),
        in_specs=[row(a.shape[1]), row(b.shape[1]), _const_spec(w.shape), row(d)],
        out_specs=row(d),
        out_shape=jax.ShapeDtypeStruct((t, d), F32),
        compiler_params=_params(("parallel",)),
        name="proj2_res",
    )(a, b, w, x2)


DIL_WC = 128
GROUP_COLS = 3 * ODD_MIX


def _odd_proj_kernel(x_ref, g_ref, w_ref, hg_ref, *refs):
    out_refs, h_ref = refs[:-1], refs[-1]
    tm = x_ref.shape[0]
    h = _rms_rows(x_ref[...], g_ref[...])
    n_lane_tiles = h_ref.shape[0]
    for c in range(n_lane_tiles):
        h_ref[c] = h[:, c * LANES:(c + 1) * LANES]
    for gi, o_ref in enumerate(out_refs):
        d = o_ref.shape[1]
        n = tm // d
        if d == 1:
            hb = h.astype(BF16)
        else:
            hb = jnp.concatenate(
                [jnp.concatenate([h_ref[c, pl.ds(r, n, stride=d), :] for r in range(d)], axis=0)
                 for c in range(n_lane_tiles)], axis=1).astype(BF16)
        for c0 in range(0, GROUP_COLS, MXU_COLS):
            zw = jnp.dot(hb, w_ref[:, gi * GROUP_COLS + c0:gi * GROUP_COLS + c0 + MXU_COLS],
                         preferred_element_type=F32)
            for sub in range(MXU_COLS // LANES):
                col = c0 + sub * LANES
                cs = slice(gi * GROUP_COLS + col, gi * GROUP_COLS + col + LANES)
                z = zw[:, sub * LANES:(sub + 1) * LANES]
                if col < 2 * ODD_MIX:
                    ss = jnp.sum(z * z, axis=-1, keepdims=True)
                    z = z * lax.rsqrt(ss * (1.0 / LANES) + EPS) * hg_ref[:, cs]
                z = z.astype(o_ref.dtype)
                for r in range(d):
                    o_ref[0, r, :, col:col + LANES] = z[r * n:(r + 1) * n]


def _odd_proj(x2, g, w, hg, batch, seq):
    t, dm = x2.shape
    tiles = seq // ROW_TILE
    out_specs, out_shapes = [], []
    for _, d in DIL_PATTERNS:
        out_specs.append(pl.BlockSpec((1, d, ROW_TILE // d, GROUP_COLS),
                                      lambda i: (i // tiles, 0, i % tiles, 0)))
        out_shapes.append(jax.ShapeDtypeStruct((batch, d, seq // d, GROUP_COLS), BF16))
    return pl.pallas_call(
        _odd_proj_kernel,
        grid=(t // ROW_TILE,),
        in_specs=[pl.BlockSpec((ROW_TILE, dm), lambda i: (i, 0)),
                  _const_spec((1, dm)), _const_spec(w.shape), _const_spec(hg.shape)],
        out_specs=out_specs,
        out_shape=out_shapes,
        scratch_shapes=[pltpu.VMEM((dm // LANES, ROW_TILE, LANES), F32)],
        compiler_params=_params(("parallel",)),
        name="odd_proj",
    )(x2, g, w, hg)


def _dil_kernel(q_ref, kp_ref, kc_ref, vp_ref, vc_ref, o_ref, lse_ref):
    n = pl.program_id(2)
    rows = q_ref.shape[2]
    wc = DIL_WC
    i = lax.broadcasted_iota(jnp.int32, (wc, 2 * wc), 0)
    j = lax.broadcasted_iota(jnp.int32, (wc, 2 * wc), 1)
    dist = i + wc - j
    band = (dist >= 0) & (dist <= wc)
    band_first = band & ((j >= wc) | (n > 0))
    ones = jnp.ones((2 * wc, DIL_DH), BF16)
    for jb in range(rows // wc):
        rs = slice(jb * wc, (jb + 1) * wc)
        for h in range(DIL_HEADS):
            hs = slice(h * DIL_DH, (h + 1) * DIL_DH)
            if jb == 0:
                k = jnp.concatenate([kp_ref[0, 0, :, hs], kc_ref[0, 0, 0:wc, hs]], axis=0)
                v = jnp.concatenate([vp_ref[0, 0, :, hs], vc_ref[0, 0, 0:wc, hs]], axis=0)
                valid = band_first
            else:
                k = kc_ref[0, 0, (jb - 1) * wc:(jb + 1) * wc, hs]
                v = vc_ref[0, 0, (jb - 1) * wc:(jb + 1) * wc, hs]
                valid = band
            s = lax.dot_general(q_ref[0, 0, rs, hs], k, (((1,), (1,)), ((), ())),
                                preferred_element_type=F32) * (DIL_DH ** -0.5)
            s = jnp.where(valid, s, NEG_BIG)
            m = jnp.max(s, axis=-1, keepdims=True)
            p = jnp.exp(s - m)
            pv = jnp.dot(p.astype(BF16), jnp.concatenate([v, ones], axis=1),
                         preferred_element_type=F32)
            l = pv[:, DIL_DH:]
            o_ref[0, 0, rs, hs] = (pv[:, 0:DIL_DH] / l).astype(o_ref.dtype)
            lse_ref[0, 0, rs, hs] = m + jnp.log(l)


def _dilated_group(zg):
    b, d, length, _ = zg.shape
    rows = min(length, 512)
    per_row = rows // DIL_WC

    def cur(which):
        return pl.BlockSpec((1, 1, rows, ODD_MIX), lambda bi, r, n: (bi, r, n, which))

    def prev(which):
        return pl.BlockSpec((1, 1, DIL_WC, ODD_MIX),
                            lambda bi, r, n: (bi, r, jnp.maximum(n * per_row - 1, 0), which))

    out_spec = pl.BlockSpec((1, 1, rows, ODD_MIX), lambda bi, r, n: (bi, r, n, 0))
    return pl.pallas_call(
        _dil_kernel,
        grid=(b, d, length // rows),
        in_specs=[cur(0), prev(1), cur(1), prev(2), cur(2)],
        out_specs=[out_spec, out_spec],
        out_shape=[jax.ShapeDtypeStruct((b, d, length, ODD_MIX), BF16),
                   jax.ShapeDtypeStruct((b, d, length, ODD_MIX), F32)],
        compiler_params=_params(("parallel", "parallel", "arbitrary")),
        name=f"dilated_d{d}",
    )(zg, zg, zg, zg, zg)


def _combine_proj_res_kernel(*refs):
    n_groups = len(DIL_PATTERNS)
    o_refs, l_refs = refs[:n_groups], refs[n_groups:2 * n_groups]
    w_ref, x_ref, out_ref, stage_ref = refs[2 * n_groups:]
    tm = x_ref.shape[0]

    def natural(ref, slot):
        d = ref.shape[1]
        if d == 1:
            return ref[0, 0].astype(F32)
        n = tm // d
        n_lane_tiles = stage_ref.shape[1]
        for r in range(d):
            blk = ref[0, r].astype(F32)
            for c in range(n_lane_tiles):
                stage_ref[slot, c, pl.ds(r, n, stride=d), :] = blk[:, c * LANES:(c + 1) * LANES]
        return jnp.concatenate([stage_ref[slot, c] for c in range(n_lane_tiles)], axis=1)

    outs = [natural(r, 2 * gi) for gi, r in enumerate(o_refs)]
    lses = [natural(r, 2 * gi + 1) for gi, r in enumerate(l_refs)]
    m = functools.reduce(jnp.maximum, lses)
    es = [jnp.exp(l - m) for l in lses]
    mix = sum(e * o for e, o in zip(es, outs)) / sum(es)
    out_ref[...] = x_ref[...] + jnp.dot(mix.astype(BF16), w_ref[...], preferred_element_type=F32)


def _combine_proj_res(os_, lses, w, x2, seq):
    t, dm = x2.shape
    tiles = seq // ROW_TILE
    row = pl.BlockSpec((ROW_TILE, dm), lambda i: (i, 0))
    cls = [pl.BlockSpec((1, o.shape[1], ROW_TILE // o.shape[1], ODD_MIX),
                        lambda i: (i // tiles, 0, i % tiles, 0)) for o in os_]
    return pl.pallas_call(
        _combine_proj_res_kernel,
        grid=(t // ROW_TILE,),
        in_specs=cls + cls + [_const_spec(w.shape), row],
        out_specs=row,
        out_shape=jax.ShapeDtypeStruct((t, dm), F32),
        scratch_shapes=[pltpu.VMEM((2 * len(os_), ODD_MIX // LANES, ROW_TILE, LANES), F32)],
        compiler_params=_params(("parallel",)),
        name="combine_proj_res",
    )(*os_, *lses, w, x2)


FFN_CHUNK = 256
SUBLANES = 8
CARRY_ROWS = 2 * SUBLANES
STAGE_PAD_ROWS = 8


def _ffn_kernel(x_ref, g_ref, wup_ref, cw_ref, wdn_ref, o_ref, carry_ref, hs_ref, acc_ref, *, tiles_per_seq):
    tm, dm = x_ref.shape
    ng = tm // SUBLANES
    n_chunks = D_FF // FFN_CHUNK
    n_lane_tiles = dm // LANES

    h = _rms_rows(x_ref[...], g_ref[...])
    pitch = hs_ref.shape[1] // SUBLANES
    for c in range(n_lane_tiles):
        for s in range(SUBLANES):
            hs_ref[c, s * pitch:s * pitch + ng, :] = h[s * ng:(s + 1) * ng, c * LANES:(c + 1) * LANES]
    hb = jnp.concatenate(
        [jnp.concatenate([hs_ref[c, pl.ds(j, SUBLANES, stride=pitch), :] for j in range(ng)], axis=0)
         for c in range(n_lane_tiles)], axis=1).astype(BF16)

    @pl.when(pl.program_id(0) % tiles_per_seq == 0)
    def _():
        carry_ref[...] = jnp.zeros_like(carry_ref)

    last_sublane = lax.broadcasted_iota(jnp.int32, (SUBLANES, FFN_CHUNK), 0) == SUBLANES - 1

    def cols_of(c):
        return (slice(c * FFN_CHUNK, (c + 1) * FFN_CHUNK),
                slice(D_FF + c * FFN_CHUNK, D_FF + (c + 1) * FFN_CHUNK))

    def up(c):
        return [jnp.dot(hb, wup_ref[:, cols], preferred_element_type=F32) for cols in cols_of(c)]

    def wrap(cur, prev):
        return pltpu.roll(jnp.where(last_sublane, prev, cur), 1, 0)

    def conv(u, cols):
        prev = carry_ref[:, cols]
        w2 = wrap(u[tm - 2 * SUBLANES:tm - SUBLANES], prev[0:SUBLANES])
        w1 = wrap(u[tm - SUBLANES:], prev[SUBLANES:])
        u1 = jnp.concatenate([w1, u[:tm - SUBLANES]], axis=0)
        u2 = jnp.concatenate([w2, w1, u[:tm - 2 * SUBLANES]], axis=0)
        carry_ref[:, cols] = u[tm - CARRY_ROWS:]
        cw = cw_ref[:, cols]
        return u2 * cw[0:1] + u1 * cw[1:2] + u * cw[2:3] + cw[3:4]

    u_cur = up(0)
    for c in range(n_chunks):
        u_next = up(c + 1) if c + 1 < n_chunks else None
        gate, val = [conv(u, cols) for u, cols in zip(u_cur, cols_of(c))]
        act = (gate / (1.0 + jnp.exp(-gate)) * val).astype(BF16)
        d = jnp.dot(act, wdn_ref[c * FFN_CHUNK:(c + 1) * FFN_CHUNK, :], preferred_element_type=F32)
        for lt in range(n_lane_tiles):
            piece = d[:, lt * LANES:(lt + 1) * LANES]
            acc_ref[lt] = piece if c == 0 else acc_ref[lt] + piece
        u_cur = u_next

    for lt in range(n_lane_tiles):
        cs = slice(lt * LANES, (lt + 1) * LANES)
        for s in range(SUBLANES):
            rs = slice(s * ng, (s + 1) * ng)
            o_ref[rs, cs] = x_ref[rs, cs] + acc_ref[lt, pl.ds(s, ng, stride=SUBLANES), :]


def _ffn(x2, g, w_up, conv_tab, w_down, seq_len):
    t, d = x2.shape
    row = pl.BlockSpec((ROW_TILE, d), lambda i: (i, 0))
    return pl.pallas_call(
        functools.partial(_ffn_kernel, tiles_per_seq=seq_len // ROW_TILE),
        grid=(t // ROW_TILE,),
        in_specs=[row, _const_spec((1, d)), _const_spec(w_up.shape), _const_spec(conv_tab.shape),
                  _const_spec(w_down.shape)],
        out_specs=row,
        out_shape=jax.ShapeDtypeStruct((t, d), F32),
        scratch_shapes=[pltpu.VMEM((CARRY_ROWS, 2 * D_FF), F32),
                        pltpu.VMEM((d // LANES, ROW_TILE + STAGE_PAD_ROWS * SUBLANES, LANES), F32),
                        pltpu.VMEM((d // LANES, ROW_TILE, LANES), F32)],
        compiler_params=_params(("arbitrary",)),
        name="conv_ffn",
    )(x2, g, w_up, conv_tab, w_down)


def _even_weights(w_in, w_a2, b_a, dq_gain, dk_gain):
    sizes = [A_QK, A_QK, A_V, A_V, GLA_RANK, B_QK, B_QK, B_V]
    offs = [0]
    for c in sizes:
        offs.append(offs[-1] + c)
    aq, ak, av, ag, ar, bq, bk, bv = [w_in[:, offs[i]:offs[i + 1]] for i in range(8)]
    pad = jnp.zeros((w_in.shape[0], LANES - GLA_RANK), w_in.dtype)
    w = jnp.concatenate([aq, ak, av, ag, bq, bk, bv, ar, pad], axis=1).astype(BF16)
    hg = jnp.ones((EV_COLS,), F32)
    hg = hg.at[EV_BQ:EV_BK].set(jnp.tile(dq_gain.astype(F32), B_QK // DIFF_DH) * (DIFF_DH ** -0.5 * LOG2E))
    hg = hg.at[EV_BK:EV_BV].set(jnp.tile(dk_gain.astype(F32), B_QK // DIFF_DH))
    norm_group = tuple(DIFF_DH if EV_BQ <= c * LANES < EV_BV else 0 for c in range(EV_COLS // LANES))
    wa2p = jnp.concatenate([w_a2, jnp.zeros((LANES - GLA_RANK, A_QK), w_a2.dtype)], axis=0).astype(BF16)
    return w, hg.reshape(1, EV_COLS), norm_group, wa2p, b_a.astype(F32).reshape(1, A_QK)


def _odd_gains(q_gain, k_gain):
    hg = jnp.ones((3, 3, ODD_MIX), F32)
    hg = hg.at[:, 0].set(jnp.tile(q_gain.astype(F32), DIL_HEADS))
    hg = hg.at[:, 1].set(jnp.tile(k_gain.astype(F32), DIL_HEADS))
    return hg.reshape(1, ODD_IN)


def kernel(x, norm_mix, norm_ffn, ev_w_in, ev_w_a2, ev_b_a, ev_gla_gain, ev_dq_gain, ev_dk_gain,
           ev_lq1, ev_lk1, ev_lq2, ev_lk2, ev_diff_gain, ev_w_out, od_w_in, od_q_gain, od_k_gain,
           od_w_out, ffn_w_up, ffn_conv_w, ffn_conv_b, ffn_w_down):
    b, s, d = x.shape
    depth = norm_mix.shape[0]
    x2 = x.reshape(b * s, d).astype(F32)
    row = lambda v: v.astype(F32).reshape(1, -1)
    for i in range(depth):
        g_mix = row(norm_mix[i])
        if i % 2 == 0:
            e = i // 2
            lambda_init = 0.8 - 0.6 * math.exp(-0.3 * i)
            w, hg, norm_group, wa2p, ba = _even_weights(ev_w_in[e], ev_w_a2[e], ev_b_a[e],
                                                        ev_dq_gain[e], ev_dk_gain[e])
            z3 = _norm_proj(x2, g_mix, w, hg, norm_group).reshape(b, s, EV_COLS)
            oa = _gla(z3, wa2p, ba, row(ev_gla_gain[e]))
            ob = _diff_attention(z3, row(ev_lq1[e]), row(ev_lk1[e]), row(ev_lq2[e]), row(ev_lk2[e]),
                                 row(ev_diff_gain[e]), lambda_init)
            x2 = _proj2_res(oa.reshape(b * s, A_V), ob.reshape(b * s, B_V),
                            ev_w_out[e].astype(BF16), x2)
        else:
            o = i // 2
            zs = _odd_proj(x2, g_mix, od_w_in[o].astype(BF16), _odd_gains(od_q_gain[o], od_k_gain[o]), b, s)
            outs = [_dilated_group(zg) for zg in zs]
            x2 = _combine_proj_res([o_ for o_, _ in outs], [l_ for _, l_ in outs],
                                   od_w_out[o].astype(BF16), x2, s)
        conv_tab = jnp.concatenate([ffn_conv_w[i].reshape(3, 2 * D_FF), ffn_conv_b[i].reshape(1, 2 * D_FF),
                                    jnp.zeros((4, 2 * D_FF), F32)], axis=0).astype(F32)
        x2 = _ffn(x2, row(norm_ffn[i]), ffn_w_up[i].astype(BF16), conv_tab,
                  ffn_w_down[i].astype(BF16), s)
    return x2.reshape(b, s, d).astype(x.dtype)
```

```python
import functools
import math

import jax
import jax.numpy as jnp
from jax import lax
from jax.experimental import pallas as pl
from jax.experimental.pallas import tpu as pltpu

F32 = jnp.float32
BF16 = jnp.bfloat16

EPS = 1e-6
LANES = 128
MXU_COLS = 256
VMEM_LIMIT = 56 * 1024 * 1024
NEG_BIG = -1e30

D_MODEL = 1024
GLA_HEADS, GLA_DK, GLA_DV, GLA_RANK, GLA_TAU, GLA_CHUNK = 4, 64, 128, 16, 16.0, 64
DIFF_HEADS, DIFF_DH, DIFF_DV = 4, 64, 128
A_QK, A_V, B_QK, B_V = 256, 512, 512, 512
DIL_PATTERNS = ((128, 1), (512, 4), (2048, 16))
DIL_HEADS, DIL_DH = 4, 128
ODD_MIX = DIL_HEADS * DIL_DH
ODD_IN = 3 * 3 * ODD_MIX
D_FF = 2816

EV_AQ, EV_AK, EV_AV, EV_AG, EV_BQ, EV_BK, EV_BV, EV_AR = 0, 256, 512, 1024, 1536, 2048, 2560, 3072
EV_COLS = 3200

ROW_TILE = 512


def _const_spec(shape):
    nd = len(shape)
    return pl.BlockSpec(shape, lambda *_: (0,) * nd)


def _params(sem):
    return pltpu.CompilerParams(dimension_semantics=sem, vmem_limit_bytes=VMEM_LIMIT)


def _rms_rows(x, g):
    return x * lax.rsqrt(jnp.mean(x * x, axis=-1, keepdims=True) + EPS) * g


def _norm_proj_kernel(x_ref, g_ref, w_ref, hg_ref, o_ref, *, norm_group):
    hb = _rms_rows(x_ref[...], g_ref[...]).astype(BF16)
    n_cols = o_ref.shape[1]
    for c0 in range(0, n_cols, MXU_COLS):
        width = min(MXU_COLS, n_cols - c0)
        zw = jnp.dot(hb, w_ref[:, c0:c0 + width], preferred_element_type=F32)
        for sub in range(width // LANES):
            cs = slice(c0 + sub * LANES, c0 + (sub + 1) * LANES)
            z = zw[:, sub * LANES:(sub + 1) * LANES]
            grp = norm_group[cs.start // LANES]
            if grp == LANES:
                ss = jnp.sum(z * z, axis=-1, keepdims=True)
                z = z * lax.rsqrt(ss * (1.0 / LANES) + EPS) * hg_ref[:, cs]
            elif grp == LANES // 2:
                lo = lax.broadcasted_iota(jnp.int32, z.shape, 1) < grp
                sq = z * z
                s_lo = jnp.sum(jnp.where(lo, sq, 0.0), axis=-1, keepdims=True)
                s_hi = jnp.sum(jnp.where(lo, 0.0, sq), axis=-1, keepdims=True)
                ss = jnp.where(lo, s_lo, s_hi)
                z = z * lax.rsqrt(ss * (1.0 / grp) + EPS) * hg_ref[:, cs]
            o_ref[:, cs] = z.astype(o_ref.dtype)


def _norm_proj(x2, g, w, hg, norm_group):
    t, d = x2.shape
    n = w.shape[1]
    return pl.pallas_call(
        functools.partial(_norm_proj_kernel, norm_group=norm_group),
        grid=(t // ROW_TILE,),
        in_specs=[pl.BlockSpec((ROW_TILE, d), lambda i: (i, 0)),
                  _const_spec((1, d)), _const_spec((d, n)), _const_spec((1, n))],
        out_specs=pl.BlockSpec((ROW_TILE, n), lambda i: (i, 0)),
        out_shape=jax.ShapeDtypeStruct((t, n), BF16),
        compiler_params=_params(("parallel",)),
        name="norm_proj",
    )(x2, g, w, hg)


GLA_TILE = 512


def _split3(x):
    hi = x.astype(BF16)
    r1 = x - hi.astype(F32)
    mid = r1.astype(BF16)
    lo = (r1 - mid.astype(F32)).astype(BF16)
    return hi, mid, lo


def _gla_kernel(q_ref, k_ref, v_ref, gate_ref, ar_ref, wa2_ref, ba_ref, gain_ref, o_ref, state_ref):
    C = GLA_CHUNK
    tq = q_ref.shape[1]
    n_chunks = tq // C
    shift = C.bit_length() - 1

    @pl.when(pl.program_id(1) == 0)
    def _():
        state_ref[...] = jnp.zeros_like(state_ref)

    logits = jnp.dot(ar_ref[0], wa2_ref[...], preferred_element_type=F32) + ba_ref[...]
    log_sig = jnp.minimum(logits, 0.0) - jnp.log(1.0 + jnp.exp(-jnp.abs(logits)))
    la = log_sig * (1.0 / GLA_TAU)

    row = lax.broadcasted_iota(jnp.int32, (tq, tq), 0)
    col = lax.broadcasted_iota(jnp.int32, (tq, tq), 1)
    same = (row >> shift) == (col >> shift)
    incl = same & (col <= row)
    t_incl = jnp.where(incl, 1.0, 0.0).astype(BF16)
    b = sum(jnp.dot(t_incl, p, preferred_element_type=F32) for p in _split3(la))
    b_last = jnp.concatenate([jnp.broadcast_to(b[(c + 1) * C - 1:(c + 1) * C, :], (C, b.shape[1]))
                              for c in range(n_chunks)], axis=0)

    q_d = (q_ref[0].astype(F32) * (GLA_DK ** -0.5) * jnp.exp(b)).astype(BF16)
    k_f = k_ref[0].astype(F32)
    k_d = (k_f * jnp.exp(-b)).astype(BF16)
    k_tt = (k_f * jnp.exp(b_last - b)).T
    b_last_t = b_last.T

    gate = gate_ref[0].astype(F32)
    for h in range(GLA_HEADS):
        ks = slice(h * GLA_DK, (h + 1) * GLA_DK)
        vs = slice(h * GLA_DV, (h + 1) * GLA_DV)
        v_h = v_ref[0, :, vs]
        att = lax.dot_general(q_d[:, ks], k_d[:, ks], (((1,), (1,)), ((), ())),
                              preferred_element_type=F32)
        att = jnp.where(incl, att, 0.0).astype(BF16)
        o_h = jnp.dot(att, v_h, preferred_element_type=F32)
        k_blk = jnp.where(same, jnp.tile(k_tt[ks, :], (n_chunks, 1)), 0.0).astype(BF16)
        kv = jnp.dot(k_blk, v_h, preferred_element_type=F32)
        st = state_ref[h]
        inter = []
        for c in range(n_chunks):
            rows = slice(c * C, (c + 1) * C)
            inter.append(jnp.dot(q_d[rows, ks], st.astype(BF16), preferred_element_type=F32))
            dec = jnp.exp(b_last_t[ks, c * C:c * C + 1])
            st = st * dec + kv[c * GLA_DK:(c + 1) * GLA_DK, :]
        state_ref[h] = st
        o_h = o_h + jnp.concatenate(inter, axis=0)
        g = gate[:, vs]
        o_ref[0, :, vs] = (_rms_rows(o_h, gain_ref[...]) * (g / (1.0 + jnp.exp(-g)))).astype(o_ref.dtype)


def _gla(z3, wa2p, ba, gain):
    b, s, _ = z3.shape
    tq = min(GLA_TILE, s)

    def col(width, start):
        return pl.BlockSpec((1, tq, width), lambda bi, i: (bi, i, start // width))

    return pl.pallas_call(
        _gla_kernel,
        grid=(b, s // tq),
        in_specs=[col(A_QK, EV_AQ), col(A_QK, EV_AK), col(A_V, EV_AV), col(A_V, EV_AG),
                  col(LANES, EV_AR), _const_spec((LANES, A_QK)), _const_spec((1, A_QK)),
                  _const_spec((1, GLA_DV))],
        out_specs=pl.BlockSpec((1, tq, A_V), lambda bi, i: (bi, i, 0)),
        out_shape=jax.ShapeDtypeStruct((b, s, A_V), BF16),
        scratch_shapes=[pltpu.VMEM((GLA_HEADS, GLA_DK, GLA_DV), F32)],
        compiler_params=_params(("parallel", "arbitrary")),
        name="gla",
    )(z3, z3, z3, z3, z3, wa2p, ba, gain)


DIFF_TILE_Q = 1024
DIFF_TILE_K = 512
DIFF_ROWS = 256
LOG2E = math.log2(math.e)


def _diff_kernel(q_ref, k_ref, v_ref, lq1_ref, lk1_ref, lq2_ref, lk2_ref, gain_ref,
                 o_ref, qv_ref, m_ref, acc_ref, *, lambda_init, tk):
    qi = pl.program_id(2)
    tq = q_ref.shape[1]
    rc = DIFF_ROWS
    span = tq // tk

    q = q_ref[0]
    first = lax.broadcasted_iota(jnp.int32, q.shape, 1) < DIFF_DH
    zero = jnp.zeros_like(q)
    qv_ref[0:tq, :] = jnp.where(first, q, zero)
    qv_ref[tq:2 * tq, :] = jnp.where(first, zero, q)
    m_ref[...] = jnp.full_like(m_ref, NEG_BIG)
    acc_ref[...] = jnp.zeros_like(acc_ref)

    def update(block, diag):
        rows = pl.ds(pl.multiple_of(block * tk, tk), tk)
        k = k_ref[0, rows, :]
        v1 = jnp.concatenate([v_ref[0, rows, :], jnp.ones((tk, LANES), BF16)], axis=1)

        def key_width(c):
            if diag is None:
                return tk
            return max(min((c * rc) % tq + rc - diag * tk, tk), 0)

        chunks = [c for c in range(2 * tq // rc) if key_width(c) > 0]

        def scores(c):
            return lax.dot_general(qv_ref[c * rc:(c + 1) * rc, :], k[0:key_width(c)],
                                   (((1,), (1,)), ((), ())), preferred_element_type=F32)

        s_next = scores(chunks[0])
        for idx, c in enumerate(chunks):
            rs = slice(c * rc, (c + 1) * rc)
            q0 = (c * rc) % tq
            kw = key_width(c)
            s = s_next
            s_next = scores(chunks[idx + 1]) if idx + 1 < len(chunks) else None
            if diag is not None:
                r = lax.broadcasted_iota(jnp.int32, (rc, kw), 0) + (q0 - diag * tk)
                col = lax.broadcasted_iota(jnp.int32, (rc, kw), 1)
                s = jnp.where(col <= r, s, NEG_BIG)
            m_prev = m_ref[rs, :]
            m_new = jnp.maximum(m_prev, jnp.max(s, axis=-1, keepdims=True))
            alpha = jnp.exp2(m_prev - m_new)
            p = jnp.exp2(s - jnp.tile(m_new, (1, kw // LANES)))
            pv = jnp.dot(p.astype(BF16), v1[0:kw], preferred_element_type=F32)
            acc_ref[rs, :] = acc_ref[rs, :] * jnp.tile(alpha, (1, 2)) + pv
            m_ref[rs, :] = m_new

    def below_diagonal(block, carry):
        update(block, None)
        return carry

    lax.fori_loop(0, qi * span, below_diagonal, 0)
    for diag in range(span):
        update(qi * span + diag, diag)

    lam = (jnp.exp(jnp.sum(lq1_ref[...] * lk1_ref[...], axis=-1, keepdims=True))
           - jnp.exp(jnp.sum(lq2_ref[...] * lk2_ref[...], axis=-1, keepdims=True))
           + lambda_init)
    o1 = acc_ref[0:tq, 0:DIFF_DV] / acc_ref[0:tq, DIFF_DV:]
    o2 = acc_ref[tq:2 * tq, 0:DIFF_DV] / acc_ref[tq:2 * tq, DIFF_DV:]
    o = _rms_rows(o1 - lam * o2, gain_ref[...]) * (1.0 - lambda_init)
    o_ref[0] = o.astype(o_ref.dtype)


def _diff_attention(z3, lq1, lk1, lq2, lk2, gain, lambda_init):
    b, s, _ = z3.shape
    tq = min(DIFF_TILE_Q, s)
    tk = min(DIFF_TILE_K, s)
    qc, kc, vc = EV_BQ // LANES, EV_BK // LANES, EV_BV // LANES
    return pl.pallas_call(
        functools.partial(_diff_kernel, lambda_init=lambda_init, tk=tk),
        grid=(b, DIFF_HEADS, s // tq),
        in_specs=[pl.BlockSpec((1, tq, LANES), lambda bi, h, qi: (bi, qi, qc + h)),
                  pl.BlockSpec((1, s, LANES), lambda bi, h, qi: (bi, 0, kc + h)),
                  pl.BlockSpec((1, s, LANES), lambda bi, h, qi: (bi, 0, vc + h)),
                  _const_spec((1, DIFF_DH)), _const_spec((1, DIFF_DH)),
                  _const_spec((1, DIFF_DH)), _const_spec((1, DIFF_DH)),
                  _const_spec((1, DIFF_DV))],
        out_specs=pl.BlockSpec((1, tq, LANES), lambda bi, h, qi: (bi, qi, h)),
        out_shape=jax.ShapeDtypeStruct((b, s, B_V), BF16),
        scratch_shapes=[pltpu.VMEM((2 * tq, LANES), BF16), pltpu.VMEM((2 * tq, LANES), F32),
                        pltpu.VMEM((2 * tq, 2 * DIFF_DV), F32)],
        compiler_params=_params(("parallel", "parallel", "arbitrary")),
        name="diff_attention",
    )(z3, z3, z3, lq1, lk1, lq2, lk2, gain)


def _proj2_res_kernel(a_ref, b_ref, w_ref, x_ref, o_ref):
    ka = a_ref.shape[1]
    o_ref[...] = (x_ref[...]
                  + jnp.dot(a_ref[...], w_ref[0:ka, :], preferred_element_type=F32)
                  + jnp.dot(b_ref[...], w_ref[ka:, :], preferred_element_type=F32))


def _proj2_res(a, b, w, x2):
    t, d = x2.shape
    row = lambda width: pl.BlockSpec((ROW_TILE, width), lambda i: (i, 0))
    return pl.pallas_call(
        _proj2_res_kernel,
        grid=(t // ROW_TILE,),
        in_specs=[row(a.shape[1]), row(b.shape[1]), _const_spec(w.shape), row(d)],
        out_specs=row(d),
        out_shape=jax.ShapeDtypeStruct((t, d), F32),
        compiler_params=_params(("parallel",)),
        name="proj2_res",
    )(a, b, w, x2)


DIL_WC = 128
LSE_LANES = LANES // DIL_HEADS
GROUP_COLS = 3 * ODD_MIX


def _odd_proj_kernel(x_ref, g_ref, w_ref, hg_ref, *refs):
    out_refs, h_ref = refs[:-1], refs[-1]
    tm = x_ref.shape[0]
    h = _rms_rows(x_ref[...], g_ref[...])
    n_lane_tiles = h_ref.shape[0]
    for c in range(n_lane_tiles):
        h_ref[c] = h[:, c * LANES:(c + 1) * LANES]
    for gi, o_ref in enumerate(out_refs):
        d = o_ref.shape[1]
        n = tm // d
        if d == 1:
            hb = h.astype(BF16)
        else:
            hb = jnp.concatenate(
                [jnp.concatenate([h_ref[c, pl.ds(r, n, stride=d), :] for r in range(d)], axis=0)
                 for c in range(n_lane_tiles)], axis=1).astype(BF16)
        for c0 in range(0, GROUP_COLS, MXU_COLS):
            zw = jnp.dot(hb, w_ref[:, gi * GROUP_COLS + c0:gi * GROUP_COLS + c0 + MXU_COLS],
                         preferred_element_type=F32)
            for sub in range(MXU_COLS // LANES):
                col = c0 + sub * LANES
                cs = slice(gi * GROUP_COLS + col, gi * GROUP_COLS + col + LANES)
                z = zw[:, sub * LANES:(sub + 1) * LANES]
                if col < 2 * ODD_MIX:
                    ss = jnp.sum(z * z, axis=-1, keepdims=True)
                    z = z * lax.rsqrt(ss * (1.0 / LANES) + EPS) * hg_ref[:, cs]
                z = z.astype(o_ref.dtype)
                for r in range(d):
                    o_ref[0, r, :, col:col + LANES] = z[r * n:(r + 1) * n]


def _odd_proj(x2, g, w, hg, batch, seq):
    t, dm = x2.shape
    tiles = seq // ROW_TILE
    out_specs, out_shapes = [], []
    for _, d in DIL_PATTERNS:
        out_specs.append(pl.BlockSpec((1, d, ROW_TILE // d, GROUP_COLS),
                                      lambda i: (i // tiles, 0, i % tiles, 0)))
        out_shapes.append(jax.ShapeDtypeStruct((batch, d, seq // d, GROUP_COLS), BF16))
    return pl.pallas_call(
        _odd_proj_kernel,
        grid=(t // ROW_TILE,),
        in_specs=[pl.BlockSpec((ROW_TILE, dm), lambda i: (i, 0)),
                  _const_spec((1, dm)), _const_spec(w.shape), _const_spec(hg.shape)],
        out_specs=out_specs,
        out_shape=out_shapes,
        scratch_shapes=[pltpu.VMEM((dm // LANES, ROW_TILE, LANES), F32)],
        compiler_params=_params(("parallel",)),
        name="odd_proj",
    )(x2, g, w, hg)


def _dil_kernel(q_ref, kp_ref, kc_ref, vp_ref, vc_ref, o_ref, lse_ref):
    n = pl.program_id(2)
    rows = q_ref.shape[2]
    wc = DIL_WC
    i = lax.broadcasted_iota(jnp.int32, (wc, 2 * wc), 0)
    j = lax.broadcasted_iota(jnp.int32, (wc, 2 * wc), 1)
    dist = i + wc - j
    band = (dist >= 0) & (dist <= wc)
    band_first = band & ((j >= wc) | (n > 0))
    ones = jnp.ones((2 * wc, DIL_DH), BF16)
    lane_head = lax.broadcasted_iota(jnp.int32, (wc, LANES), 1) // LSE_LANES
    for jb in range(rows // wc):
        rs = slice(jb * wc, (jb + 1) * wc)
        lse = None
        for h in range(DIL_HEADS):
            hs = slice(h * DIL_DH, (h + 1) * DIL_DH)
            if jb == 0:
                k = jnp.concatenate([kp_ref[0, 0, :, hs], kc_ref[0, 0, 0:wc, hs]], axis=0)
                v = jnp.concatenate([vp_ref[0, 0, :, hs], vc_ref[0, 0, 0:wc, hs]], axis=0)
                valid = band_first
            else:
                k = kc_ref[0, 0, (jb - 1) * wc:(jb + 1) * wc, hs]
                v = vc_ref[0, 0, (jb - 1) * wc:(jb + 1) * wc, hs]
                valid = band
            s = lax.dot_general(q_ref[0, 0, rs, hs], k, (((1,), (1,)), ((), ())),
                                preferred_element_type=F32) * (DIL_DH ** -0.5)
            s = jnp.where(valid, s, NEG_BIG)
            m = jnp.max(s, axis=-1, keepdims=True)
            p = jnp.exp(s - m)
            pv = jnp.dot(p.astype(BF16), jnp.concatenate([v, ones], axis=1),
                         preferred_element_type=F32)
            l = pv[:, DIL_DH:]
            o_ref[0, 0, rs, hs] = (pv[:, 0:DIL_DH] / l).astype(o_ref.dtype)
            lse_h = m + jnp.log(l)
            lse = lse_h if lse is None else jnp.where(lane_head == h, lse_h, lse)
        lse_ref[0, 0, rs, :] = lse


def _dilated_group(zg):
    b, d, length, _ = zg.shape
    rows = min(length, 512)
    per_row = rows // DIL_WC

    def cur(which):
        return pl.BlockSpec((1, 1, rows, ODD_MIX), lambda bi, r, n: (bi, r, n, which))

    def prev(which):
        return pl.BlockSpec((1, 1, DIL_WC, ODD_MIX),
                            lambda bi, r, n: (bi, r, jnp.maximum(n * per_row - 1, 0), which))

    out_spec = pl.BlockSpec((1, 1, rows, ODD_MIX), lambda bi, r, n: (bi, r, n, 0))
    lse_spec = pl.BlockSpec((1, 1, rows, LANES), lambda bi, r, n: (bi, r, n, 0))
    return pl.pallas_call(
        _dil_kernel,
        grid=(b, d, length // rows),
        in_specs=[cur(0), prev(1), cur(1), prev(2), cur(2)],
        out_specs=[out_spec, lse_spec],
        out_shape=[jax.ShapeDtypeStruct((b, d, length, ODD_MIX), BF16),
                   jax.ShapeDtypeStruct((b, d, length, LANES), F32)],
        compiler_params=_params(("parallel", "parallel", "arbitrary")),
        name=f"dilated_d{d}",
    )(zg, zg, zg, zg, zg)


def _combine_proj_res_kernel(*refs):
    n_groups = len(DIL_PATTERNS)
    o_refs, l_refs = refs[:n_groups], refs[n_groups:2 * n_groups]
    w_ref, x_ref, out_ref, stage_ref = refs[2 * n_groups:]
    tm = x_ref.shape[0]

    def natural(ref, slot):
        d = ref.shape[1]
        if d == 1:
            return ref[0, 0].astype(F32)
        n = tm // d
        n_lane_tiles = ref.shape[3] // LANES
        for r in range(d):
            blk = ref[0, r].astype(F32)
            for c in range(n_lane_tiles):
                stage_ref[slot, c, pl.ds(r, n, stride=d), :] = blk[:, c * LANES:(c + 1) * LANES]
        return jnp.concatenate([stage_ref[slot, c] for c in range(n_lane_tiles)], axis=1)

    outs = [natural(r, 2 * gi) for gi, r in enumerate(o_refs)]
    lses = [natural(r, 2 * gi + 1) for gi, r in enumerate(l_refs)]
    m = functools.reduce(jnp.maximum, lses)
    es = [jnp.exp(l - m) for l in lses]
    inv = 1.0 / sum(es)
    sel_row = lax.broadcasted_iota(jnp.int32, (LANES, ODD_MIX), 0)
    sel_col = lax.broadcasted_iota(jnp.int32, (LANES, ODD_MIX), 1)
    select = jnp.where(sel_row == (sel_col // DIL_DH) * LSE_LANES, 1.0, 0.0).astype(BF16)
    mix = sum(sum(jnp.dot(p, select, preferred_element_type=F32) for p in _split3(e * inv)) * o
              for e, o in zip(es, outs))
    out_ref[...] = x_ref[...] + jnp.dot(mix.astype(BF16), w_ref[...], preferred_element_type=F32)


def _combine_proj_res(os_, lses, w, x2, seq):
    t, dm = x2.shape
    tiles = seq // ROW_TILE
    row = pl.BlockSpec((ROW_TILE, dm), lambda i: (i, 0))
    cls = [pl.BlockSpec((1, a.shape[1], ROW_TILE // a.shape[1], a.shape[3]),
                        lambda i: (i // tiles, 0, i % tiles, 0)) for a in list(os_) + list(lses)]
    return pl.pallas_call(
        _combine_proj_res_kernel,
        grid=(t // ROW_TILE,),
        in_specs=cls + [_const_spec(w.shape), row],
        out_specs=row,
        out_shape=jax.ShapeDtypeStruct((t, dm), F32),
        scratch_shapes=[pltpu.VMEM((2 * len(os_), ODD_MIX // LANES, ROW_TILE, LANES), F32)],
        compiler_params=_params(("parallel",)),
        name="combine_proj_res",
    )(*os_, *lses, w, x2)


FFN_CHUNK = 256
FFN_DOWN_GROUP = 6
SUBLANES = 8
CARRY_ROWS = 2 * SUBLANES
STAGE_PAD_ROWS = 8


def _ffn_kernel(x_ref, g_ref, wup_ref, cw_ref, wdn_ref, o_ref, carry_ref, hs_ref, acc_ref, *, tiles_per_seq):
    tm, dm = x_ref.shape
    ng = tm // SUBLANES
    n_chunks = D_FF // FFN_CHUNK
    n_lane_tiles = dm // LANES

    h = _rms_rows(x_ref[...], g_ref[...])
    pitch = hs_ref.shape[1] // SUBLANES
    for c in range(n_lane_tiles):
        for s in range(SUBLANES):
            hs_ref[c, s * pitch:s * pitch + ng, :] = h[s * ng:(s + 1) * ng, c * LANES:(c + 1) * LANES]
    hb = jnp.concatenate(
        [jnp.concatenate([hs_ref[c, pl.ds(j, SUBLANES, stride=pitch), :] for j in range(ng)], axis=0)
         for c in range(n_lane_tiles)], axis=1).astype(BF16)

    @pl.when(pl.program_id(0) % tiles_per_seq == 0)
    def _():
        carry_ref[...] = jnp.zeros_like(carry_ref)

    last_sublane = lax.broadcasted_iota(jnp.int32, (SUBLANES, FFN_CHUNK), 0) == SUBLANES - 1

    def cols_of(c):
        return (slice(c * FFN_CHUNK, (c + 1) * FFN_CHUNK),
                slice(D_FF + c * FFN_CHUNK, D_FF + (c + 1) * FFN_CHUNK))

    def up(c):
        return [jnp.dot(hb, wup_ref[:, cols], preferred_element_type=F32) for cols in cols_of(c)]

    def wrap(cur, prev):
        return pltpu.roll(jnp.where(last_sublane, prev, cur), 1, 0)

    def conv(u, cols):
        prev = carry_ref[:, cols]
        w2 = wrap(u[tm - 2 * SUBLANES:tm - SUBLANES], prev[0:SUBLANES])
        w1 = wrap(u[tm - SUBLANES:], prev[SUBLANES:])
        u1 = jnp.concatenate([w1, u[:tm - SUBLANES]], axis=0)
        u2 = jnp.concatenate([w2, w1, u[:tm - 2 * SUBLANES]], axis=0)
        carry_ref[:, cols] = u[tm - CARRY_ROWS:]
        cw = cw_ref[:, cols]
        return u2 * cw[0:1] + u1 * cw[1:2] + u * cw[2:3] + cw[3:4]

    u_cur = up(0)
    pending = []
    for c in range(n_chunks):
        u_next = up(c + 1) if c + 1 < n_chunks else None
        gate, val = [conv(u, cols) for u, cols in zip(u_cur, cols_of(c))]
        pending.append((gate / (1.0 + jnp.exp(-gate)) * val).astype(BF16))
        if len(pending) == FFN_DOWN_GROUP or c + 1 == n_chunks:
            first = c + 1 - len(pending)
            act = pending[0] if len(pending) == 1 else jnp.concatenate(pending, axis=1)
            d = jnp.dot(act, wdn_ref[first * FFN_CHUNK:(c + 1) * FFN_CHUNK, :], preferred_element_type=F32)
            for lt in range(n_lane_tiles):
                piece = d[:, lt * LANES:(lt + 1) * LANES]
                acc_ref[lt] = piece if first == 0 else acc_ref[lt] + piece
            pending = []
        u_cur = u_next

    for lt in range(n_lane_tiles):
        cs = slice(lt * LANES, (lt + 1) * LANES)
        for s in range(SUBLANES):
            rs = slice(s * ng, (s + 1) * ng)
            o_ref[rs, cs] = x_ref[rs, cs] + acc_ref[lt, pl.ds(s, ng, stride=SUBLANES), :]


def _ffn(x2, g, w_up, conv_tab, w_down, seq_len):
    t, d = x2.shape
    row = pl.BlockSpec((ROW_TILE, d), lambda i: (i, 0))
    return pl.pallas_call(
        functools.partial(_ffn_kernel, tiles_per_seq=seq_len // ROW_TILE),
        grid=(t // ROW_TILE,),
        in_specs=[row, _const_spec((1, d)), _const_spec(w_up.shape), _const_spec(conv_tab.shape),
                  _const_spec(w_down.shape)],
        out_specs=row,
        out_shape=jax.ShapeDtypeStruct((t, d), F32),
        scratch_shapes=[pltpu.VMEM((CARRY_ROWS, 2 * D_FF), F32),
                        pltpu.VMEM((d // LANES, ROW_TILE + STAGE_PAD_ROWS * SUBLANES, LANES), F32),
                        pltpu.VMEM((d // LANES, ROW_TILE, LANES), F32)],
        compiler_params=_params(("arbitrary",)),
        name="conv_ffn",
    )(x2, g, w_up, conv_tab, w_down)


def _even_weights(w_in, w_a2, b_a, dq_gain, dk_gain):
    sizes = [A_QK, A_QK, A_V, A_V, GLA_RANK, B_QK, B_QK, B_V]
    offs = [0]
    for c in sizes:
        offs.append(offs[-1] + c)
    aq, ak, av, ag, ar, bq, bk, bv = [w_in[:, offs[i]:offs[i + 1]] for i in range(8)]
    pad = jnp.zeros((w_in.shape[0], LANES - GLA_RANK), w_in.dtype)
    w = jnp.concatenate([aq, ak, av, ag, bq, bk, bv, ar, pad], axis=1).astype(BF16)
    hg = jnp.ones((EV_COLS,), F32)
    hg = hg.at[EV_BQ:EV_BK].set(jnp.tile(dq_gain.astype(F32), B_QK // DIFF_DH) * (DIFF_DH ** -0.5 * LOG2E))
    hg = hg.at[EV_BK:EV_BV].set(jnp.tile(dk_gain.astype(F32), B_QK // DIFF_DH))
    norm_group = tuple(DIFF_DH if EV_BQ <= c * LANES < EV_BV else 0 for c in range(EV_COLS // LANES))
    wa2p = jnp.concatenate([w_a2, jnp.zeros((LANES - GLA_RANK, A_QK), w_a2.dtype)], axis=0).astype(BF16)
    return w, hg.reshape(1, EV_COLS), norm_group, wa2p, b_a.astype(F32).reshape(1, A_QK)


def _odd_gains(q_gain, k_gain):
    hg = jnp.ones((3, 3, ODD_MIX), F32)
    hg = hg.at[:, 0].set(jnp.tile(q_gain.astype(F32), DIL_HEADS))
    hg = hg.at[:, 1].set(jnp.tile(k_gain.astype(F32), DIL_HEADS))
    return hg.reshape(1, ODD_IN)


def kernel(x, norm_mix, norm_ffn, ev_w_in, ev_w_a2, ev_b_a, ev_gla_gain, ev_dq_gain, ev_dk_gain,
           ev_lq1, ev_lk1, ev_lq2, ev_lk2, ev_diff_gain, ev_w_out, od_w_in, od_q_gain, od_k_gain,
           od_w_out, ffn_w_up, ffn_conv_w, ffn_conv_b, ffn_w_down):
    b, s, d = x.shape
    depth = norm_mix.shape[0]
    x2 = x.reshape(b * s, d).astype(F32)
    row = lambda v: v.astype(F32).reshape(1, -1)
    for i in range(depth):
        g_mix = row(norm_mix[i])
        if i % 2 == 0:
            e = i // 2
            lambda_init = 0.8 - 0.6 * math.exp(-0.3 * i)
            w, hg, norm_group, wa2p, ba = _even_weights(ev_w_in[e], ev_w_a2[e], ev_b_a[e],
                                                        ev_dq_gain[e], ev_dk_gain[e])
            z3 = _norm_proj(x2, g_mix, w, hg, norm_group).reshape(b, s, EV_COLS)
            oa = _gla(z3, wa2p, ba, row(ev_gla_gain[e]))
            ob = _diff_attention(z3, row(ev_lq1[e]), row(ev_lk1[e]), row(ev_lq2[e]), row(ev_lk2[e]),
                                 row(ev_diff_gain[e]), lambda_init)
            x2 = _proj2_res(oa.reshape(b * s, A_V), ob.reshape(b * s, B_V),
                            ev_w_out[e].astype(BF16), x2)
        else:
            o = i // 2
            zs = _odd_proj(x2, g_mix, od_w_in[o].astype(BF16), _odd_gains(od_q_gain[o], od_k_gain[o]), b, s)
            outs = [_dilated_group(zg) for zg in zs]
            x2 = _combine_proj_res([o_ for o_, _ in outs], [l_ for _, l_ in outs],
                                   od_w_out[o].astype(BF16), x2, s)
        conv_tab = jnp.concatenate([ffn_conv_w[i].reshape(3, 2 * D_FF), ffn_conv_b[i].reshape(1, 2 * D_FF),
                                    jnp.zeros((4, 2 * D_FF), F32)], axis=0).astype(F32)
        x2 = _ffn(x2, row(norm_ffn[i]), ffn_w_up[i].astype(BF16), conv_tab,
                  ffn_w_down[i].astype(BF16), s)
    return x2.reshape(b, s, d).astype(x.dtype)
```

```python
import functools
import math

import jax
import jax.numpy as jnp
from jax import lax
from jax.experimental import pallas as pl
from jax.experimental.pallas import tpu as pltpu

F32 = jnp.float32
BF16 = jnp.bfloat16

EPS = 1e-6
LANES = 128
MXU_COLS = 256
VMEM_LIMIT = 56 * 1024 * 1024
NEG_BIG = -1e30

D_MODEL = 1024
GLA_HEADS, GLA_DK, GLA_DV, GLA_RANK, GLA_TAU, GLA_CHUNK = 4, 64, 128, 16, 16.0, 64
DIFF_HEADS, DIFF_DH, DIFF_DV = 4, 64, 128
A_QK, A_V, B_QK, B_V = 256, 512, 512, 512
DIL_PATTERNS = ((128, 1), (512, 4), (2048, 16))
DIL_HEADS, DIL_DH = 4, 128
ODD_MIX = DIL_HEADS * DIL_DH
ODD_IN = 3 * 3 * ODD_MIX
D_FF = 2816

EV_AQ, EV_AK, EV_AV, EV_AG, EV_BQ, EV_BK, EV_BV, EV_AR = 0, 256, 512, 1024, 1536, 2048, 2560, 3072
EV_COLS = 3200

ROW_TILE = 512


def _const_spec(shape):
    nd = len(shape)
    return pl.BlockSpec(shape, lambda *_: (0,) * nd)


def _params(sem):
    return pltpu.CompilerParams(dimension_semantics=sem, vmem_limit_bytes=VMEM_LIMIT)


def _rms_rows(x, g):
    return x * lax.rsqrt(jnp.mean(x * x, axis=-1, keepdims=True) + EPS) * g


def _norm_proj_kernel(x_ref, g_ref, w_ref, hg_ref, o_ref, *, norm_group):
    hb = _rms_rows(x_ref[...], g_ref[...]).astype(BF16)
    n_cols = o_ref.shape[1]
    for c0 in range(0, n_cols, MXU_COLS):
        width = min(MXU_COLS, n_cols - c0)
        zw = jnp.dot(hb, w_ref[:, c0:c0 + width], preferred_element_type=F32)
        for sub in range(width // LANES):
            cs = slice(c0 + sub * LANES, c0 + (sub + 1) * LANES)
            z = zw[:, sub * LANES:(sub + 1) * LANES]
            grp = norm_group[cs.start // LANES]
            if grp == LANES:
                ss = jnp.sum(z * z, axis=-1, keepdims=True)
                z = z * lax.rsqrt(ss * (1.0 / LANES) + EPS) * hg_ref[:, cs]
            elif grp == LANES // 2:
                lo = lax.broadcasted_iota(jnp.int32, z.shape, 1) < grp
                sq = z * z
                s_lo = jnp.sum(jnp.where(lo, sq, 0.0), axis=-1, keepdims=True)
                s_hi = jnp.sum(jnp.where(lo, 0.0, sq), axis=-1, keepdims=True)
                ss = jnp.where(lo, s_lo, s_hi)
                z = z * lax.rsqrt(ss * (1.0 / grp) + EPS) * hg_ref[:, cs]
            o_ref[:, cs] = z.astype(o_ref.dtype)


def _norm_proj(x2, g, w, hg, norm_group):
    t, d = x2.shape
    n = w.shape[1]
    return pl.pallas_call(
        functools.partial(_norm_proj_kernel, norm_group=norm_group),
        grid=(t // ROW_TILE,),
        in_specs=[pl.BlockSpec((ROW_TILE, d), lambda i: (i, 0)),
                  _const_spec((1, d)), _const_spec((d, n)), _const_spec((1, n))],
        out_specs=pl.BlockSpec((ROW_TILE, n), lambda i: (i, 0)),
        out_shape=jax.ShapeDtypeStruct((t, n), BF16),
        compiler_params=_params(("parallel",)),
        name="norm_proj",
    )(x2, g, w, hg)


GLA_TILE = 512
GLA_BATCH = 2


def _split3(x):
    hi = x.astype(BF16)
    r1 = x - hi.astype(F32)
    mid = r1.astype(BF16)
    lo = (r1 - mid.astype(F32)).astype(BF16)
    return hi, mid, lo


def _gla_kernel(q_ref, k_ref, v_ref, gate_ref, ar_ref, wa2_ref, ba_ref, gain_ref, o_ref, state_ref):
    C = GLA_CHUNK
    tq = q_ref.shape[1]
    n_chunks = tq // C
    shift = C.bit_length() - 1

    @pl.when(pl.program_id(1) == 0)
    def _():
        state_ref[...] = jnp.zeros_like(state_ref)

    row = lax.broadcasted_iota(jnp.int32, (tq, tq), 0)
    col = lax.broadcasted_iota(jnp.int32, (tq, tq), 1)
    same = (row >> shift) == (col >> shift)
    incl = same & (col <= row)
    t_incl = jnp.where(incl, 1.0, 0.0).astype(BF16)

    for bb in range(q_ref.shape[0]):
        logits = jnp.dot(ar_ref[bb], wa2_ref[...], preferred_element_type=F32) + ba_ref[...]
        log_sig = jnp.minimum(logits, 0.0) - jnp.log(1.0 + jnp.exp(-jnp.abs(logits)))
        la = log_sig * (1.0 / GLA_TAU)
        b = sum(jnp.dot(t_incl, p, preferred_element_type=F32) for p in _split3(la))
        b_last = jnp.concatenate([jnp.broadcast_to(b[(c + 1) * C - 1:(c + 1) * C, :], (C, b.shape[1]))
                                  for c in range(n_chunks)], axis=0)

        q_d = (q_ref[bb].astype(F32) * (GLA_DK ** -0.5) * jnp.exp(b)).astype(BF16)
        k_f = k_ref[bb].astype(F32)
        k_d = (k_f * jnp.exp(-b)).astype(BF16)
        k_tt = (k_f * jnp.exp(b_last - b)).T
        b_last_t = b_last.T

        gate = gate_ref[bb].astype(F32)
        for h in range(GLA_HEADS):
            ks = slice(h * GLA_DK, (h + 1) * GLA_DK)
            vs = slice(h * GLA_DV, (h + 1) * GLA_DV)
            v_h = v_ref[bb, :, vs]
            att = lax.dot_general(q_d[:, ks], k_d[:, ks], (((1,), (1,)), ((), ())),
                                  preferred_element_type=F32)
            att = jnp.where(incl, att, 0.0).astype(BF16)
            o_h = jnp.dot(att, v_h, preferred_element_type=F32)
            k_blk = jnp.where(same, jnp.tile(k_tt[ks, :], (n_chunks, 1)), 0.0).astype(BF16)
            kv = jnp.dot(k_blk, v_h, preferred_element_type=F32)
            st = state_ref[bb, h]
            inter = []
            for c in range(n_chunks):
                rows = slice(c * C, (c + 1) * C)
                inter.append(jnp.dot(q_d[rows, ks], st.astype(BF16), preferred_element_type=F32))
                dec = jnp.exp(b_last_t[ks, c * C:c * C + 1])
                st = st * dec + kv[c * GLA_DK:(c + 1) * GLA_DK, :]
            state_ref[bb, h] = st
            o_h = o_h + jnp.concatenate(inter, axis=0)
            g = gate[:, vs]
            o_ref[bb, :, vs] = (_rms_rows(o_h, gain_ref[...]) * (g / (1.0 + jnp.exp(-g)))).astype(o_ref.dtype)


def _gla(z3, wa2p, ba, gain):
    b, s, _ = z3.shape
    tq = min(GLA_TILE, s)
    nb = GLA_BATCH if b % GLA_BATCH == 0 else 1

    def col(width, start):
        return pl.BlockSpec((nb, tq, width), lambda bi, i: (bi, i, start // width))

    return pl.pallas_call(
        _gla_kernel,
        grid=(b // nb, s // tq),
        in_specs=[col(A_QK, EV_AQ), col(A_QK, EV_AK), col(A_V, EV_AV), col(A_V, EV_AG),
                  col(LANES, EV_AR), _const_spec((LANES, A_QK)), _const_spec((1, A_QK)),
                  _const_spec((1, GLA_DV))],
        out_specs=pl.BlockSpec((nb, tq, A_V), lambda bi, i: (bi, i, 0)),
        out_shape=jax.ShapeDtypeStruct((b, s, A_V), BF16),
        scratch_shapes=[pltpu.VMEM((nb, GLA_HEADS, GLA_DK, GLA_DV), F32)],
        compiler_params=_params(("parallel", "arbitrary")),
        name="gla",
    )(z3, z3, z3, z3, z3, wa2p, ba, gain)


DIFF_TILE_Q = 1024
DIFF_TILE_K = 512
DIFF_ROWS = 256
LOG2E = math.log2(math.e)


def _diff_kernel(q_ref, k_ref, v_ref, lq1_ref, lk1_ref, lq2_ref, lk2_ref, gain_ref,
                 o_ref, qv_ref, m_ref, acc_ref, *, lambda_init, tk):
    qi = pl.program_id(2)
    tq = q_ref.shape[1]
    rc = DIFF_ROWS
    span = tq // tk

    q = q_ref[0]
    first = lax.broadcasted_iota(jnp.int32, q.shape, 1) < DIFF_DH
    zero = jnp.zeros_like(q)
    qv_ref[0:tq, :] = jnp.where(first, q, zero)
    qv_ref[tq:2 * tq, :] = jnp.where(first, zero, q)
    m_ref[...] = jnp.full_like(m_ref, NEG_BIG)
    acc_ref[...] = jnp.zeros_like(acc_ref)

    def process(blocks):
        tasks = []
        for block, diag in blocks:
            rows = pl.ds(pl.multiple_of(block * tk, tk), tk)
            k = k_ref[0, rows, :]
            v1 = jnp.concatenate([v_ref[0, rows, :], jnp.ones((tk, LANES), BF16)], axis=1)
            for c in range(2 * tq // rc):
                kw = tk if diag is None else max(min((c * rc) % tq + rc - diag * tk, tk), 0)
                if kw > 0:
                    tasks.append((k, v1, c, kw, diag))

        def scores(task):
            k, _, c, kw, _ = task
            return lax.dot_general(qv_ref[c * rc:(c + 1) * rc, :], k[0:kw],
                                   (((1,), (1,)), ((), ())), preferred_element_type=F32)

        s_next = scores(tasks[0])
        for idx, (_, v1, c, kw, diag) in enumerate(tasks):
            rs = slice(c * rc, (c + 1) * rc)
            s = s_next
            s_next = scores(tasks[idx + 1]) if idx + 1 < len(tasks) else None
            if diag is not None:
                r = lax.broadcasted_iota(jnp.int32, (rc, kw), 0) + ((c * rc) % tq - diag * tk)
                col = lax.broadcasted_iota(jnp.int32, (rc, kw), 1)
                s = jnp.where(col <= r, s, NEG_BIG)
            m_prev = m_ref[rs, :]
            m_new = jnp.maximum(m_prev, jnp.max(s, axis=-1, keepdims=True))
            alpha = jnp.exp2(m_prev - m_new)
            p = jnp.exp2(s - jnp.tile(m_new, (1, kw // LANES)))
            pv = jnp.dot(p.astype(BF16), v1[0:kw], preferred_element_type=F32)
            acc_ref[rs, :] = acc_ref[rs, :] * jnp.tile(alpha, (1, 2)) + pv
            m_ref[rs, :] = m_new

    def below_diagonal(j, carry):
        process([(j * span + d, None) for d in range(span)])
        return carry

    lax.fori_loop(0, qi, below_diagonal, 0)
    process([(qi * span + d, d) for d in range(span)])

    lam = (jnp.exp(jnp.sum(lq1_ref[...] * lk1_ref[...], axis=-1, keepdims=True))
           - jnp.exp(jnp.sum(lq2_ref[...] * lk2_ref[...], axis=-1, keepdims=True))
           + lambda_init)
    o1 = acc_ref[0:tq, 0:DIFF_DV] / acc_ref[0:tq, DIFF_DV:]
    o2 = acc_ref[tq:2 * tq, 0:DIFF_DV] / acc_ref[tq:2 * tq, DIFF_DV:]
    o = _rms_rows(o1 - lam * o2, gain_ref[...]) * (1.0 - lambda_init)
    o_ref[0] = o.astype(o_ref.dtype)


def _diff_attention(z3, lq1, lk1, lq2, lk2, gain, lambda_init):
    b, s, _ = z3.shape
    tq = min(DIFF_TILE_Q, s)
    tk = min(DIFF_TILE_K, s)
    qc, kc, vc = EV_BQ // LANES, EV_BK // LANES, EV_BV // LANES
    return pl.pallas_call(
        functools.partial(_diff_kernel, lambda_init=lambda_init, tk=tk),
        grid=(b, DIFF_HEADS, s // tq),
        in_specs=[pl.BlockSpec((1, tq, LANES), lambda bi, h, qi: (bi, qi, qc + h)),
                  pl.BlockSpec((1, s, LANES), lambda bi, h, qi: (bi, 0, kc + h)),
                  pl.BlockSpec((1, s, LANES), lambda bi, h, qi: (bi, 0, vc + h)),
                  _const_spec((1, DIFF_DH)), _const_spec((1, DIFF_DH)),
                  _const_spec((1, DIFF_DH)), _const_spec((1, DIFF_DH)),
                  _const_spec((1, DIFF_DV))],
        out_specs=pl.BlockSpec((1, tq, LANES), lambda bi, h, qi: (bi, qi, h)),
        out_shape=jax.ShapeDtypeStruct((b, s, B_V), BF16),
        scratch_shapes=[pltpu.VMEM((2 * tq, LANES), BF16), pltpu.VMEM((2 * tq, LANES), F32),
                        pltpu.VMEM((2 * tq, 2 * DIFF_DV), F32)],
        compiler_params=_params(("parallel", "parallel", "arbitrary")),
        name="diff_attention",
    )(z3, z3, z3, lq1, lk1, lq2, lk2, gain)


def _proj2_res_kernel(a_ref, b_ref, w_ref, x_ref, o_ref):
    ka = a_ref.shape[1]
    o_ref[...] = (x_ref[...]
                  + jnp.dot(a_ref[...], w_ref[0:ka, :], preferred_element_type=F32)
                  + jnp.dot(b_ref[...], w_ref[ka:, :], preferred_element_type=F32))


def _proj2_res(a, b, w, x2):
    t, d = x2.shape
    row = lambda width: pl.BlockSpec((ROW_TILE, width), lambda i: (i, 0))
    return pl.pallas_call(
        _proj2_res_kernel,
        grid=(t // ROW_TILE,),
        in_specs=[row(a.shape[1]), row(b.shape[1]), _const_spec(w.shape), row(d)],
        out_specs=row(d),
        out_shape=jax.ShapeDtypeStruct((t, d), F32),
        compiler_params=_params(("parallel",)),
        name="proj2_res",
    )(a, b, w, x2)


DIL_WC = 128
LSE_LANES = LANES // DIL_HEADS
GROUP_COLS = 3 * ODD_MIX


def _odd_proj_kernel(x_ref, g_ref, w_ref, hg_ref, *refs):
    out_refs, h_ref = refs[:-1], refs[-1]
    tm = x_ref.shape[0]
    h = _rms_rows(x_ref[...], g_ref[...])
    n_lane_tiles = h_ref.shape[0]
    for c in range(n_lane_tiles):
        h_ref[c] = h[:, c * LANES:(c + 1) * LANES]
    for gi, o_ref in enumerate(out_refs):
        d = o_ref.shape[1]
        n = tm // d
        if d == 1:
            hb = h.astype(BF16)
        else:
            hb = jnp.concatenate(
                [jnp.concatenate([h_ref[c, pl.ds(r, n, stride=d), :] for r in range(d)], axis=0)
                 for c in range(n_lane_tiles)], axis=1).astype(BF16)
        for c0 in range(0, GROUP_COLS, MXU_COLS):
            zw = jnp.dot(hb, w_ref[:, gi * GROUP_COLS + c0:gi * GROUP_COLS + c0 + MXU_COLS],
                         preferred_element_type=F32)
            for sub in range(MXU_COLS // LANES):
                col = c0 + sub * LANES
                cs = slice(gi * GROUP_COLS + col, gi * GROUP_COLS + col + LANES)
                z = zw[:, sub * LANES:(sub + 1) * LANES]
                if col < 2 * ODD_MIX:
                    ss = jnp.sum(z * z, axis=-1, keepdims=True)
                    z = z * lax.rsqrt(ss * (1.0 / LANES) + EPS) * hg_ref[:, cs]
                z = z.astype(o_ref.dtype)
                for r in range(d):
                    o_ref[0, r, :, col:col + LANES] = z[r * n:(r + 1) * n]


def _odd_proj(x2, g, w, hg, batch, seq):
    t, dm = x2.shape
    tiles = seq // ROW_TILE
    out_specs, out_shapes = [], []
    for _, d in DIL_PATTERNS:
        out_specs.append(pl.BlockSpec((1, d, ROW_TILE // d, GROUP_COLS),
                                      lambda i: (i // tiles, 0, i % tiles, 0)))
        out_shapes.append(jax.ShapeDtypeStruct((batch, d, seq // d, GROUP_COLS), BF16))
    return pl.pallas_call(
        _odd_proj_kernel,
        grid=(t // ROW_TILE,),
        in_specs=[pl.BlockSpec((ROW_TILE, dm), lambda i: (i, 0)),
                  _const_spec((1, dm)), _const_spec(w.shape), _const_spec(hg.shape)],
        out_specs=out_specs,
        out_shape=out_shapes,
        scratch_shapes=[pltpu.VMEM((dm // LANES, ROW_TILE, LANES), F32)],
        compiler_params=_params(("parallel",)),
        name="odd_proj",
    )(x2, g, w, hg)


def _dil_kernel(q_ref, kp_ref, kc_ref, vp_ref, vc_ref, o_ref, lse_ref):
    n = pl.program_id(2)
    rows = q_ref.shape[2]
    wc = DIL_WC
    i = lax.broadcasted_iota(jnp.int32, (wc, 2 * wc), 0)
    j = lax.broadcasted_iota(jnp.int32, (wc, 2 * wc), 1)
    dist = i + wc - j
    band = (dist >= 0) & (dist <= wc)
    band_first = band & ((j >= wc) | (n > 0))
    ones = jnp.ones((2 * wc, DIL_DH), BF16)
    lane_head = lax.broadcasted_iota(jnp.int32, (wc, LANES), 1) // LSE_LANES
    for jb in range(rows // wc):
        rs = slice(jb * wc, (jb + 1) * wc)
        lse = None
        for h in range(DIL_HEADS):
            hs = slice(h * DIL_DH, (h + 1) * DIL_DH)
            if jb == 0:
                k = jnp.concatenate([kp_ref[0, 0, :, hs], kc_ref[0, 0, 0:wc, hs]], axis=0)
                v = jnp.concatenate([vp_ref[0, 0, :, hs], vc_ref[0, 0, 0:wc, hs]], axis=0)
                valid = band_first
            else:
                k = kc_ref[0, 0, (jb - 1) * wc:(jb + 1) * wc, hs]
                v = vc_ref[0, 0, (jb - 1) * wc:(jb + 1) * wc, hs]
                valid = band
            s = lax.dot_general(q_ref[0, 0, rs, hs], k, (((1,), (1,)), ((), ())),
                                preferred_element_type=F32)
            s = jnp.where(valid, s, NEG_BIG)
            m = jnp.max(s, axis=-1, keepdims=True)
            p = jnp.exp2(s - m)
            pv = jnp.dot(p.astype(BF16), jnp.concatenate([v, ones], axis=1),
                         preferred_element_type=F32)
            l = pv[:, DIL_DH:]
            o_ref[0, 0, rs, hs] = (pv[:, 0:DIL_DH] / l).astype(o_ref.dtype)
            lse_h = m * (1.0 / LOG2E) + jnp.log(l)
            lse = lse_h if lse is None else jnp.where(lane_head == h, lse_h, lse)
        lse_ref[0, 0, rs, :] = lse


def _dilated_group(zg):
    b, d, length, _ = zg.shape
    rows = min(length, 512)
    per_row = rows // DIL_WC

    def cur(which):
        return pl.BlockSpec((1, 1, rows, ODD_MIX), lambda bi, r, n: (bi, r, n, which))

    def prev(which):
        return pl.BlockSpec((1, 1, DIL_WC, ODD_MIX),
                            lambda bi, r, n: (bi, r, jnp.maximum(n * per_row - 1, 0), which))

    out_spec = pl.BlockSpec((1, 1, rows, ODD_MIX), lambda bi, r, n: (bi, r, n, 0))
    lse_spec = pl.BlockSpec((1, 1, rows, LANES), lambda bi, r, n: (bi, r, n, 0))
    return pl.pallas_call(
        _dil_kernel,
        grid=(b, d, length // rows),
        in_specs=[cur(0), prev(1), cur(1), prev(2), cur(2)],
        out_specs=[out_spec, lse_spec],
        out_shape=[jax.ShapeDtypeStruct((b, d, length, ODD_MIX), BF16),
                   jax.ShapeDtypeStruct((b, d, length, LANES), F32)],
        compiler_params=_params(("parallel", "parallel", "arbitrary")),
        name=f"dilated_d{d}",
    )(zg, zg, zg, zg, zg)


def _combine_proj_res_kernel(*refs):
    n_groups = len(DIL_PATTERNS)
    o_refs, l_refs = refs[:n_groups], refs[n_groups:2 * n_groups]
    w_ref, x_ref, out_ref, stage_ref = refs[2 * n_groups:]
    tm = x_ref.shape[0]

    def natural(ref, slot):
        d = ref.shape[1]
        if d == 1:
            return ref[0, 0].astype(F32)
        n = tm // d
        n_lane_tiles = ref.shape[3] // LANES
        for r in range(d):
            blk = ref[0, r].astype(F32)
            for c in range(n_lane_tiles):
                stage_ref[slot, c, pl.ds(r, n, stride=d), :] = blk[:, c * LANES:(c + 1) * LANES]
        return jnp.concatenate([stage_ref[slot, c] for c in range(n_lane_tiles)], axis=1)

    outs = [natural(r, 2 * gi) for gi, r in enumerate(o_refs)]
    lses = [natural(r, 2 * gi + 1) for gi, r in enumerate(l_refs)]
    m = functools.reduce(jnp.maximum, lses)
    es = [jnp.exp(l - m) for l in lses]
    inv = 1.0 / sum(es)
    sel_row = lax.broadcasted_iota(jnp.int32, (LANES, ODD_MIX), 0)
    sel_col = lax.broadcasted_iota(jnp.int32, (LANES, ODD_MIX), 1)
    select = jnp.where(sel_row == (sel_col // DIL_DH) * LSE_LANES, 1.0, 0.0).astype(BF16)
    mix = sum(sum(jnp.dot(p, select, preferred_element_type=F32) for p in _split3(e * inv)) * o
              for e, o in zip(es, outs))
    out_ref[...] = x_ref[...] + jnp.dot(mix.astype(BF16), w_ref[...], preferred_element_type=F32)


def _combine_proj_res(os_, lses, w, x2, seq):
    t, dm = x2.shape
    tiles = seq // ROW_TILE
    row = pl.BlockSpec((ROW_TILE, dm), lambda i: (i, 0))
    cls = [pl.BlockSpec((1, a.shape[1], ROW_TILE // a.shape[1], a.shape[3]),
                        lambda i: (i // tiles, 0, i % tiles, 0)) for a in list(os_) + list(lses)]
    return pl.pallas_call(
        _combine_proj_res_kernel,
        grid=(t // ROW_TILE,),
        in_specs=cls + [_const_spec(w.shape), row],
        out_specs=row,
        out_shape=jax.ShapeDtypeStruct((t, dm), F32),
        scratch_shapes=[pltpu.VMEM((2 * len(os_), ODD_MIX // LANES, ROW_TILE, LANES), F32)],
        compiler_params=_params(("parallel",)),
        name="combine_proj_res",
    )(*os_, *lses, w, x2)


FFN_CHUNK = 256
FFN_TILE = 512
FFN_DOWN_GROUP = 6
SUBLANES = 8
CARRY_ROWS = 2 * SUBLANES
STAGE_PAD_ROWS = 8


def _ffn_kernel(x_ref, g_ref, wup_ref, cw_ref, wdn_ref, o_ref, carry_ref, hs_ref, acc_ref, *, tiles_per_seq):
    tm, dm = x_ref.shape
    ng = tm // SUBLANES
    n_chunks = D_FF // FFN_CHUNK
    n_lane_tiles = dm // LANES

    h = _rms_rows(x_ref[...], g_ref[...])
    pitch = hs_ref.shape[1] // SUBLANES
    for c in range(n_lane_tiles):
        for s in range(SUBLANES):
            hs_ref[c, s * pitch:s * pitch + ng, :] = h[s * ng:(s + 1) * ng, c * LANES:(c + 1) * LANES]
    hb = jnp.concatenate(
        [jnp.concatenate([hs_ref[c, pl.ds(j, SUBLANES, stride=pitch), :] for j in range(ng)], axis=0)
         for c in range(n_lane_tiles)], axis=1).astype(BF16)

    @pl.when(pl.program_id(0) % tiles_per_seq == 0)
    def _():
        carry_ref[...] = jnp.zeros_like(carry_ref)

    last_sublane = lax.broadcasted_iota(jnp.int32, (SUBLANES, FFN_CHUNK), 0) == SUBLANES - 1

    def cols_of(c):
        return (slice(c * FFN_CHUNK, (c + 1) * FFN_CHUNK),
                slice(D_FF + c * FFN_CHUNK, D_FF + (c + 1) * FFN_CHUNK))

    def up(c):
        return [jnp.dot(hb, wup_ref[:, cols], preferred_element_type=F32) for cols in cols_of(c)]

    def wrap(cur, prev):
        return pltpu.roll(jnp.where(last_sublane, prev, cur), 1, 0)

    def conv(u, cols):
        prev = carry_ref[:, cols]
        w2 = wrap(u[tm - 2 * SUBLANES:tm - SUBLANES], prev[0:SUBLANES])
        w1 = wrap(u[tm - SUBLANES:], prev[SUBLANES:])
        u1 = jnp.concatenate([w1, u[:tm - SUBLANES]], axis=0)
        u2 = jnp.concatenate([w2, w1, u[:tm - 2 * SUBLANES]], axis=0)
        carry_ref[:, cols] = u[tm - CARRY_ROWS:]
        cw = cw_ref[:, cols]
        return u2 * cw[0:1] + u1 * cw[1:2] + u * cw[2:3] + cw[3:4]

    u_cur = up(0)
    pending = []
    for c in range(n_chunks):
        u_next = up(c + 1) if c + 1 < n_chunks else None
        gate, val = [conv(u, cols) for u, cols in zip(u_cur, cols_of(c))]
        pending.append((gate / (1.0 + jnp.exp(-gate)) * val).astype(BF16))
        if len(pending) == FFN_DOWN_GROUP or c + 1 == n_chunks:
            first = c + 1 - len(pending)
            act = pending[0] if len(pending) == 1 else jnp.concatenate(pending, axis=1)
            d = jnp.dot(act, wdn_ref[first * FFN_CHUNK:(c + 1) * FFN_CHUNK, :], preferred_element_type=F32)
            for lt in range(n_lane_tiles):
                piece = d[:, lt * LANES:(lt + 1) * LANES]
                acc_ref[lt] = piece if first == 0 else acc_ref[lt] + piece
            pending = []
        u_cur = u_next

    for lt in range(n_lane_tiles):
        cs = slice(lt * LANES, (lt + 1) * LANES)
        for s in range(SUBLANES):
            rs = slice(s * ng, (s + 1) * ng)
            o_ref[rs, cs] = x_ref[rs, cs] + acc_ref[lt, pl.ds(s, ng, stride=SUBLANES), :]


def _ffn(x2, g, w_up, conv_tab, w_down, seq_len):
    t, d = x2.shape
    tile = min(FFN_TILE, seq_len)
    row = pl.BlockSpec((tile, d), lambda i: (i, 0))
    return pl.pallas_call(
        functools.partial(_ffn_kernel, tiles_per_seq=seq_len // tile),
        grid=(t // tile,),
        in_specs=[row, _const_spec((1, d)), _const_spec(w_up.shape), _const_spec(conv_tab.shape),
                  _const_spec(w_down.shape)],
        out_specs=row,
        out_shape=jax.ShapeDtypeStruct((t, d), F32),
        scratch_shapes=[pltpu.VMEM((CARRY_ROWS, 2 * D_FF), F32),
                        pltpu.VMEM((d // LANES, tile + STAGE_PAD_ROWS * SUBLANES, LANES), F32),
                        pltpu.VMEM((d // LANES, tile, LANES), F32)],
        compiler_params=_params(("arbitrary",)),
        name="conv_ffn",
    )(x2, g, w_up, conv_tab, w_down)


def _even_weights(w_in, w_a2, b_a, dq_gain, dk_gain):
    sizes = [A_QK, A_QK, A_V, A_V, GLA_RANK, B_QK, B_QK, B_V]
    offs = [0]
    for c in sizes:
        offs.append(offs[-1] + c)
    aq, ak, av, ag, ar, bq, bk, bv = [w_in[:, offs[i]:offs[i + 1]] for i in range(8)]
    pad = jnp.zeros((w_in.shape[0], LANES - GLA_RANK), w_in.dtype)
    w = jnp.concatenate([aq, ak, av, ag, bq, bk, bv, ar, pad], axis=1).astype(BF16)
    hg = jnp.ones((EV_COLS,), F32)
    hg = hg.at[EV_BQ:EV_BK].set(jnp.tile(dq_gain.astype(F32), B_QK // DIFF_DH) * (DIFF_DH ** -0.5 * LOG2E))
    hg = hg.at[EV_BK:EV_BV].set(jnp.tile(dk_gain.astype(F32), B_QK // DIFF_DH))
    norm_group = tuple(DIFF_DH if EV_BQ <= c * LANES < EV_BV else 0 for c in range(EV_COLS // LANES))
    wa2p = jnp.concatenate([w_a2, jnp.zeros((LANES - GLA_RANK, A_QK), w_a2.dtype)], axis=0).astype(BF16)
    return w, hg.reshape(1, EV_COLS), norm_group, wa2p, b_a.astype(F32).reshape(1, A_QK)


def _odd_gains(q_gain, k_gain):
    hg = jnp.ones((3, 3, ODD_MIX), F32)
    hg = hg.at[:, 0].set(jnp.tile(q_gain.astype(F32), DIL_HEADS) * (DIL_DH ** -0.5 * LOG2E))
    hg = hg.at[:, 1].set(jnp.tile(k_gain.astype(F32), DIL_HEADS))
    return hg.reshape(1, ODD_IN)


def kernel(x, norm_mix, norm_ffn, ev_w_in, ev_w_a2, ev_b_a, ev_gla_gain, ev_dq_gain, ev_dk_gain,
           ev_lq1, ev_lk1, ev_lq2, ev_lk2, ev_diff_gain, ev_w_out, od_w_in, od_q_gain, od_k_gain,
           od_w_out, ffn_w_up, ffn_conv_w, ffn_conv_b, ffn_w_down):
    b, s, d = x.shape
    depth = norm_mix.shape[0]
    x2 = x.reshape(b * s, d).astype(F32)
    row = lambda v: v.astype(F32).reshape(1, -1)
    for i in range(depth):
        g_mix = row(norm_mix[i])
        if i % 2 == 0:
            e = i // 2
            lambda_init = 0.8 - 0.6 * math.exp(-0.3 * i)
            w, hg, norm_group, wa2p, ba = _even_weights(ev_w_in[e], ev_w_a2[e], ev_b_a[e],
                                                        ev_dq_gain[e], ev_dk_gain[e])
            z3 = _norm_proj(x2, g_mix, w, hg, norm_group).reshape(b, s, EV_COLS)
            oa = _gla(z3, wa2p, ba, row(ev_gla_gain[e]))
            ob = _diff_attention(z3, row(ev_lq1[e]), row(ev_lk1[e]), row(ev_lq2[e]), row(ev_lk2[e]),
                                 row(ev_diff_gain[e]), lambda_init)
            x2 = _proj2_res(oa.reshape(b * s, A_V), ob.reshape(b * s, B_V),
                            ev_w_out[e].astype(BF16), x2)
        else:
            o = i // 2
            zs = _odd_proj(x2, g_mix, od_w_in[o].astype(BF16), _odd_gains(od_q_gain[o], od_k_gain[o]), b, s)
            outs = [_dilated_group(zg) for zg in zs]
            x2 = _combine_proj_res([o_ for o_, _ in outs], [l_ for _, l_ in outs],
                                   od_w_out[o].astype(BF16), x2, s)
        conv_tab = jnp.concatenate([ffn_conv_w[i].reshape(3, 2 * D_FF), ffn_conv_b[i].reshape(1, 2 * D_FF),
                                    jnp.zeros((4, 2 * D_FF), F32)], axis=0).astype(F32)
        x2 = _ffn(x2, row(norm_ffn[i]), ffn_w_up[i].astype(BF16), conv_tab,
                  ffn_w_down[i].astype(BF16), s)
    return x2.reshape(b, s, d).astype(x.dtype)
```

```python
import functools
import math

import jax
import jax.numpy as jnp
from jax import lax
from jax.experimental import pallas as pl
from jax.experimental.pallas import tpu as pltpu

F32 = jnp.float32
BF16 = jnp.bfloat16

EPS = 1e-6
LANES = 128
MXU_COLS = 256
VMEM_LIMIT = 56 * 1024 * 1024
NEG_BIG = -1e30

D_MODEL = 1024
GLA_HEADS, GLA_DK, GLA_DV, GLA_RANK, GLA_TAU, GLA_CHUNK = 4, 64, 128, 16, 16.0, 64
DIFF_HEADS, DIFF_DH, DIFF_DV = 4, 64, 128
A_QK, A_V, B_QK, B_V = 256, 512, 512, 512
DIL_PATTERNS = ((128, 1), (512, 4), (2048, 16))
DIL_HEADS, DIL_DH = 4, 128
ODD_MIX = DIL_HEADS * DIL_DH
ODD_IN = 3 * 3 * ODD_MIX
D_FF = 2816

EV_AQ, EV_AK, EV_AV, EV_AG, EV_BQ, EV_BK, EV_BV, EV_AR = 0, 256, 512, 1024, 1536, 2048, 2560, 3072
EV_COLS = 3200

ROW_TILE = 512


def _const_spec(shape):
    nd = len(shape)
    return pl.BlockSpec(shape, lambda *_: (0,) * nd)


def _params(sem):
    return pltpu.CompilerParams(dimension_semantics=sem, vmem_limit_bytes=VMEM_LIMIT)


def _rms_rows(x, g):
    return x * lax.rsqrt(jnp.mean(x * x, axis=-1, keepdims=True) + EPS) * g


def _norm_proj_kernel(x_ref, g_ref, w_ref, hg_ref, o_ref, *, norm_group):
    hb = _rms_rows(x_ref[...], g_ref[...]).astype(BF16)
    n_cols = o_ref.shape[1]
    for c0 in range(0, n_cols, MXU_COLS):
        width = min(MXU_COLS, n_cols - c0)
        zw = jnp.dot(hb, w_ref[:, c0:c0 + width], preferred_element_type=F32)
        for sub in range(width // LANES):
            cs = slice(c0 + sub * LANES, c0 + (sub + 1) * LANES)
            z = zw[:, sub * LANES:(sub + 1) * LANES]
            grp = norm_group[cs.start // LANES]
            if grp == LANES:
                ss = jnp.sum(z * z, axis=-1, keepdims=True)
                z = z * lax.rsqrt(ss * (1.0 / LANES) + EPS) * hg_ref[:, cs]
            elif grp == LANES // 2:
                lo = lax.broadcasted_iota(jnp.int32, z.shape, 1) < grp
                sq = z * z
                s_lo = jnp.sum(jnp.where(lo, sq, 0.0), axis=-1, keepdims=True)
                s_hi = jnp.sum(jnp.where(lo, 0.0, sq), axis=-1, keepdims=True)
                ss = jnp.where(lo, s_lo, s_hi)
                z = z * lax.rsqrt(ss * (1.0 / grp) + EPS) * hg_ref[:, cs]
            o_ref[:, cs] = z.astype(o_ref.dtype)


def _norm_proj(x2, g, w, hg, norm_group):
    t, d = x2.shape
    n = w.shape[1]
    return pl.pallas_call(
        functools.partial(_norm_proj_kernel, norm_group=norm_group),
        grid=(t // ROW_TILE,),
        in_specs=[pl.BlockSpec((ROW_TILE, d), lambda i: (i, 0)),
                  _const_spec((1, d)), _const_spec((d, n)), _const_spec((1, n))],
        out_specs=pl.BlockSpec((ROW_TILE, n), lambda i: (i, 0)),
        out_shape=jax.ShapeDtypeStruct((t, n), BF16),
        compiler_params=_params(("parallel",)),
        name="norm_proj",
    )(x2, g, w, hg)


GLA_TILE = 512
GLA_BATCH = 2


def _split3(x):
    hi = x.astype(BF16)
    r1 = x - hi.astype(F32)
    mid = r1.astype(BF16)
    lo = (r1 - mid.astype(F32)).astype(BF16)
    return hi, mid, lo


def _gla_kernel(q_ref, k_ref, v_ref, gate_ref, ar_ref, wa2_ref, ba_ref, gain_ref, o_ref, state_ref):
    C = GLA_CHUNK
    tq = q_ref.shape[1]
    n_chunks = tq // C
    shift = C.bit_length() - 1

    @pl.when(pl.program_id(1) == 0)
    def _():
        state_ref[...] = jnp.zeros_like(state_ref)

    row = lax.broadcasted_iota(jnp.int32, (tq, tq), 0)
    col = lax.broadcasted_iota(jnp.int32, (tq, tq), 1)
    same = (row >> shift) == (col >> shift)
    incl = same & (col <= row)
    t_incl = jnp.where(incl, 1.0, 0.0).astype(BF16)

    pairs = [(bb, h) for bb in range(q_ref.shape[0]) for h in range(GLA_HEADS)]
    ks = [slice(h * GLA_DK, (h + 1) * GLA_DK) for h in range(GLA_HEADS)]
    vs = [slice(h * GLA_DV, (h + 1) * GLA_DV) for h in range(GLA_HEADS)]

    q_d, k_d, k_tt, b_last_t = {}, {}, {}, {}
    for bb in range(q_ref.shape[0]):
        logits = jnp.dot(ar_ref[bb], wa2_ref[...], preferred_element_type=F32) + ba_ref[...]
        log_sig = jnp.minimum(logits, 0.0) - jnp.log(1.0 + jnp.exp(-jnp.abs(logits)))
        la = log_sig * (1.0 / GLA_TAU)
        b = sum(jnp.dot(t_incl, p, preferred_element_type=F32) for p in _split3(la))
        b_last = jnp.concatenate([jnp.broadcast_to(b[(c + 1) * C - 1:(c + 1) * C, :], (C, b.shape[1]))
                                  for c in range(n_chunks)], axis=0)
        q_d[bb] = (q_ref[bb].astype(F32) * (GLA_DK ** -0.5) * jnp.exp(b)).astype(BF16)
        k_f = k_ref[bb].astype(F32)
        k_d[bb] = (k_f * jnp.exp(-b)).astype(BF16)
        k_tt[bb] = (k_f * jnp.exp(b_last - b)).T
        b_last_t[bb] = b_last.T

    att = {(bb, h): lax.dot_general(q_d[bb][:, ks[h]], k_d[bb][:, ks[h]], (((1,), (1,)), ((), ())),
                                    preferred_element_type=F32) for bb, h in pairs}
    kv = {(bb, h): jnp.dot(jnp.where(same, jnp.tile(k_tt[bb][ks[h], :], (n_chunks, 1)), 0.0).astype(BF16),
                           v_ref[bb, :, vs[h]], preferred_element_type=F32) for bb, h in pairs}
    o_intra = {(bb, h): jnp.dot(jnp.where(incl, att[bb, h], 0.0).astype(BF16), v_ref[bb, :, vs[h]],
                                preferred_element_type=F32) for bb, h in pairs}
    for bb, h in pairs:
        st = state_ref[bb, h]
        inter = []
        for c in range(n_chunks):
            rows = slice(c * C, (c + 1) * C)
            inter.append(jnp.dot(q_d[bb][rows, ks[h]], st.astype(BF16), preferred_element_type=F32))
            dec = jnp.exp(b_last_t[bb][ks[h], c * C:c * C + 1])
            st = st * dec + kv[bb, h][c * GLA_DK:(c + 1) * GLA_DK, :]
        state_ref[bb, h] = st
        o_h = o_intra[bb, h] + jnp.concatenate(inter, axis=0)
        g = gate_ref[bb, :, vs[h]].astype(F32)
        o_ref[bb, :, vs[h]] = (_rms_rows(o_h, gain_ref[...]) * (g / (1.0 + jnp.exp(-g)))).astype(o_ref.dtype)


def _gla(z3, wa2p, ba, gain):
    b, s, _ = z3.shape
    tq = min(GLA_TILE, s)
    nb = GLA_BATCH if b % GLA_BATCH == 0 else 1

    def col(width, start):
        return pl.BlockSpec((nb, tq, width), lambda bi, i: (bi, i, start // width))

    return pl.pallas_call(
        _gla_kernel,
        grid=(b // nb, s // tq),
        in_specs=[col(A_QK, EV_AQ), col(A_QK, EV_AK), col(A_V, EV_AV), col(A_V, EV_AG),
                  col(LANES, EV_AR), _const_spec((LANES, A_QK)), _const_spec((1, A_QK)),
                  _const_spec((1, GLA_DV))],
        out_specs=pl.BlockSpec((nb, tq, A_V), lambda bi, i: (bi, i, 0)),
        out_shape=jax.ShapeDtypeStruct((b, s, A_V), BF16),
        scratch_shapes=[pltpu.VMEM((nb, GLA_HEADS, GLA_DK, GLA_DV), F32)],
        compiler_params=_params(("parallel", "arbitrary")),
        name="gla",
    )(z3, z3, z3, z3, z3, wa2p, ba, gain)


DIFF_TILE_Q = 1024
DIFF_TILE_K = 512
DIFF_ROWS_FULL = 512
DIFF_ROWS_DIAG = 256
DIFF_AHEAD = 2
LOG2E = math.log2(math.e)


def _diff_kernel(q_ref, k_ref, v_ref, lq1_ref, lk1_ref, lq2_ref, lk2_ref, gain_ref,
                 o_ref, qv_ref, m_ref, acc_ref, *, lambda_init, tk):
    qi = pl.program_id(2)
    tq = q_ref.shape[1]
    span = tq // tk

    q = q_ref[0]
    first = lax.broadcasted_iota(jnp.int32, q.shape, 1) < DIFF_DH
    zero = jnp.zeros_like(q)
    qv_ref[0:tq, :] = jnp.where(first, q, zero)
    qv_ref[tq:2 * tq, :] = jnp.where(first, zero, q)
    m_ref[...] = jnp.full_like(m_ref, NEG_BIG)
    acc_ref[...] = jnp.zeros_like(acc_ref)

    def process(blocks, rc):
        tasks = []
        for block, diag in blocks:
            rows = pl.ds(pl.multiple_of(block * tk, tk), tk)
            k = k_ref[0, rows, :]
            v1 = jnp.concatenate([v_ref[0, rows, :], jnp.ones((tk, LANES), BF16)], axis=1)
            for c in range(2 * tq // rc):
                kw = tk if diag is None else max(min((c * rc) % tq + rc - diag * tk, tk), 0)
                if kw > 0:
                    tasks.append((k, v1, c, kw, diag))

        def scores(task):
            k, _, c, kw, _ = task
            return lax.dot_general(qv_ref[c * rc:(c + 1) * rc, :], k[0:kw],
                                   (((1,), (1,)), ((), ())), preferred_element_type=F32)

        ready = [scores(t) for t in tasks[:DIFF_AHEAD]]
        for idx, (_, v1, c, kw, diag) in enumerate(tasks):
            rs = slice(c * rc, (c + 1) * rc)
            s = ready.pop(0)
            if idx + DIFF_AHEAD < len(tasks):
                ready.append(scores(tasks[idx + DIFF_AHEAD]))
            if diag is not None:
                r = lax.broadcasted_iota(jnp.int32, (rc, kw), 0) + ((c * rc) % tq - diag * tk)
                col = lax.broadcasted_iota(jnp.int32, (rc, kw), 1)
                s = jnp.where(col <= r, s, NEG_BIG)
            m_prev = m_ref[rs, :]
            m_new = jnp.maximum(m_prev, jnp.max(s, axis=-1, keepdims=True))
            alpha = jnp.exp2(m_prev - m_new)
            p = jnp.exp2(s - jnp.tile(m_new, (1, kw // LANES)))
            pv = jnp.dot(p.astype(BF16), v1[0:kw], preferred_element_type=F32)
            acc_ref[rs, :] = acc_ref[rs, :] * jnp.tile(alpha, (1, 2)) + pv
            m_ref[rs, :] = m_new

    def below_diagonal(j, carry):
        process([(j * span + d, None) for d in range(span)], DIFF_ROWS_FULL)
        return carry

    lax.fori_loop(0, qi, below_diagonal, 0)
    process([(qi * span + d, d) for d in range(span)], DIFF_ROWS_DIAG)

    lam = (jnp.exp(jnp.sum(lq1_ref[...] * lk1_ref[...], axis=-1, keepdims=True))
           - jnp.exp(jnp.sum(lq2_ref[...] * lk2_ref[...], axis=-1, keepdims=True))
           + lambda_init)
    o1 = acc_ref[0:tq, 0:DIFF_DV] / acc_ref[0:tq, DIFF_DV:]
    o2 = acc_ref[tq:2 * tq, 0:DIFF_DV] / acc_ref[tq:2 * tq, DIFF_DV:]
    o = _rms_rows(o1 - lam * o2, gain_ref[...]) * (1.0 - lambda_init)
    o_ref[0] = o.astype(o_ref.dtype)


def _diff_attention(z3, lq1, lk1, lq2, lk2, gain, lambda_init):
    b, s, _ = z3.shape
    tq = min(DIFF_TILE_Q, s)
    tk = min(DIFF_TILE_K, s)
    qc, kc, vc = EV_BQ // LANES, EV_BK // LANES, EV_BV // LANES
    return pl.pallas_call(
        functools.partial(_diff_kernel, lambda_init=lambda_init, tk=tk),
        grid=(b, DIFF_HEADS, s // tq),
        in_specs=[pl.BlockSpec((1, tq, LANES), lambda bi, h, qi: (bi, qi, qc + h)),
                  pl.BlockSpec((1, s, LANES), lambda bi, h, qi: (bi, 0, kc + h)),
                  pl.BlockSpec((1, s, LANES), lambda bi, h, qi: (bi, 0, vc + h)),
                  _const_spec((1, DIFF_DH)), _const_spec((1, DIFF_DH)),
                  _const_spec((1, DIFF_DH)), _const_spec((1, DIFF_DH)),
                  _const_spec((1, DIFF_DV))],
        out_specs=pl.BlockSpec((1, tq, LANES), lambda bi, h, qi: (bi, qi, h)),
        out_shape=jax.ShapeDtypeStruct((b, s, B_V), BF16),
        scratch_shapes=[pltpu.VMEM((2 * tq, LANES), BF16), pltpu.VMEM((2 * tq, LANES), F32),
                        pltpu.VMEM((2 * tq, 2 * DIFF_DV), F32)],
        compiler_params=_params(("parallel", "parallel", "arbitrary")),
        name="diff_attention",
    )(z3, z3, z3, lq1, lk1, lq2, lk2, gain)


def _proj2_res_kernel(a_ref, b_ref, w_ref, x_ref, o_ref):
    ka = a_ref.shape[1]
    o_ref[...] = (x_ref[...]
                  + jnp.dot(a_ref[...], w_ref[0:ka, :], preferred_element_type=F32)
                  + jnp.dot(b_ref[...], w_ref[ka:, :], preferred_element_type=F32))


def _proj2_res(a, b, w, x2):
    t, d = x2.shape
    row = lambda width: pl.BlockSpec((ROW_TILE, width), lambda i: (i, 0))
    return pl.pallas_call(
        _proj2_res_kernel,
        grid=(t // ROW_TILE,),
        in_specs=[row(a.shape[1]), row(b.shape[1]), _const_spec(w.shape), row(d)],
        out_specs=row(d),
        out_shape=jax.ShapeDtypeStruct((t, d), F32),
        compiler_params=_params(("parallel",)),
        name="proj2_res",
    )(a, b, w, x2)


DIL_WC = 128
LSE_LANES = LANES // DIL_HEADS
GROUP_COLS = 3 * ODD_MIX


def _odd_proj_kernel(x_ref, g_ref, w_ref, hg_ref, *refs):
    out_refs, h_ref = refs[:-1], refs[-1]
    tm = x_ref.shape[0]
    h = _rms_rows(x_ref[...], g_ref[...])
    n_lane_tiles = h_ref.shape[0]
    for c in range(n_lane_tiles):
        h_ref[c] = h[:, c * LANES:(c + 1) * LANES]
    for gi, o_ref in enumerate(out_refs):
        d = o_ref.shape[1]
        n = tm // d
        if d == 1:
            hb = h.astype(BF16)
        else:
            hb = jnp.concatenate(
                [jnp.concatenate([h_ref[c, pl.ds(r, n, stride=d), :] for r in range(d)], axis=0)
                 for c in range(n_lane_tiles)], axis=1).astype(BF16)
        for c0 in range(0, GROUP_COLS, MXU_COLS):
            zw = jnp.dot(hb, w_ref[:, gi * GROUP_COLS + c0:gi * GROUP_COLS + c0 + MXU_COLS],
                         preferred_element_type=F32)
            for sub in range(MXU_COLS // LANES):
                col = c0 + sub * LANES
                cs = slice(gi * GROUP_COLS + col, gi * GROUP_COLS + col + LANES)
                z = zw[:, sub * LANES:(sub + 1) * LANES]
                if col < 2 * ODD_MIX:
                    ss = jnp.sum(z * z, axis=-1, keepdims=True)
                    z = z * lax.rsqrt(ss * (1.0 / LANES) + EPS) * hg_ref[:, cs]
                z = z.astype(o_ref.dtype)
                for r in range(d):
                    o_ref[0, r, :, col:col + LANES] = z[r * n:(r + 1) * n]


def _odd_proj(x2, g, w, hg, batch, seq):
    t, dm = x2.shape
    tiles = seq // ROW_TILE
    out_specs, out_shapes = [], []
    for _, d in DIL_PATTERNS:
        out_specs.append(pl.BlockSpec((1, d, ROW_TILE // d, GROUP_COLS),
                                      lambda i: (i // tiles, 0, i % tiles, 0)))
        out_shapes.append(jax.ShapeDtypeStruct((batch, d, seq // d, GROUP_COLS), BF16))
    return pl.pallas_call(
        _odd_proj_kernel,
        grid=(t // ROW_TILE,),
        in_specs=[pl.BlockSpec((ROW_TILE, dm), lambda i: (i, 0)),
                  _const_spec((1, dm)), _const_spec(w.shape), _const_spec(hg.shape)],
        out_specs=out_specs,
        out_shape=out_shapes,
        scratch_shapes=[pltpu.VMEM((dm // LANES, ROW_TILE, LANES), F32)],
        compiler_params=_params(("parallel",)),
        name="odd_proj",
    )(x2, g, w, hg)


def _dil_kernel(q_ref, kp_ref, kc_ref, vp_ref, vc_ref, o_ref, lse_ref):
    n = pl.program_id(2)
    rows = q_ref.shape[2]
    wc = DIL_WC
    i = lax.broadcasted_iota(jnp.int32, (wc, 2 * wc), 0)
    j = lax.broadcasted_iota(jnp.int32, (wc, 2 * wc), 1)
    dist = i + wc - j
    band = (dist >= 0) & (dist <= wc)
    band_first = band & ((j >= wc) | (n > 0))
    ones = jnp.ones((2 * wc, DIL_DH), BF16)
    lane_head = lax.broadcasted_iota(jnp.int32, (wc, LANES), 1) // LSE_LANES
    tasks = [(jb, h) for jb in range(rows // wc) for h in range(DIL_HEADS)]

    def band_of(ref_prev, ref_cur, jb, hs):
        if jb == 0:
            return jnp.concatenate([ref_prev[0, 0, :, hs], ref_cur[0, 0, 0:wc, hs]], axis=0)
        return ref_cur[0, 0, (jb - 1) * wc:(jb + 1) * wc, hs]

    def scores(task):
        jb, h = task
        hs = slice(h * DIL_DH, (h + 1) * DIL_DH)
        return lax.dot_general(q_ref[0, 0, jb * wc:(jb + 1) * wc, hs], band_of(kp_ref, kc_ref, jb, hs),
                               (((1,), (1,)), ((), ())), preferred_element_type=F32)

    s_next = scores(tasks[0])
    lse = None
    for idx, (jb, h) in enumerate(tasks):
        rs = slice(jb * wc, (jb + 1) * wc)
        hs = slice(h * DIL_DH, (h + 1) * DIL_DH)
        s = s_next
        s_next = scores(tasks[idx + 1]) if idx + 1 < len(tasks) else None
        s = jnp.where(band_first if jb == 0 else band, s, NEG_BIG)
        m = jnp.max(s, axis=-1, keepdims=True)
        p = jnp.exp2(s - m)
        pv = jnp.dot(p.astype(BF16), jnp.concatenate([band_of(vp_ref, vc_ref, jb, hs), ones], axis=1),
                     preferred_element_type=F32)
        l = pv[:, DIL_DH:]
        o_ref[0, 0, rs, hs] = (pv[:, 0:DIL_DH] / l).astype(o_ref.dtype)
        lse_h = m * (1.0 / LOG2E) + jnp.log(l)
        lse = lse_h if h == 0 else jnp.where(lane_head == h, lse_h, lse)
        if h == DIL_HEADS - 1:
            lse_ref[0, 0, rs, :] = lse


def _dilated_group(zg):
    b, d, length, _ = zg.shape
    rows = min(length, 512)
    per_row = rows // DIL_WC

    def cur(which):
        return pl.BlockSpec((1, 1, rows, ODD_MIX), lambda bi, r, n: (bi, r, n, which))

    def prev(which):
        return pl.BlockSpec((1, 1, DIL_WC, ODD_MIX),
                            lambda bi, r, n: (bi, r, jnp.maximum(n * per_row - 1, 0), which))

    out_spec = pl.BlockSpec((1, 1, rows, ODD_MIX), lambda bi, r, n: (bi, r, n, 0))
    lse_spec = pl.BlockSpec((1, 1, rows, LANES), lambda bi, r, n: (bi, r, n, 0))
    return pl.pallas_call(
        _dil_kernel,
        grid=(b, d, length // rows),
        in_specs=[cur(0), prev(1), cur(1), prev(2), cur(2)],
        out_specs=[out_spec, lse_spec],
        out_shape=[jax.ShapeDtypeStruct((b, d, length, ODD_MIX), BF16),
                   jax.ShapeDtypeStruct((b, d, length, LANES), F32)],
        compiler_params=_params(("parallel", "parallel", "arbitrary")),
        name=f"dilated_d{d}",
    )(zg, zg, zg, zg, zg)


def _combine_proj_res_kernel(*refs):
    n_groups = len(DIL_PATTERNS)
    o_refs, l_refs = refs[:n_groups], refs[n_groups:2 * n_groups]
    w_ref, x_ref, out_ref, stage_ref = refs[2 * n_groups:]
    tm = x_ref.shape[0]

    def natural(ref, slot):
        d = ref.shape[1]
        if d == 1:
            return ref[0, 0].astype(F32)
        n = tm // d
        n_lane_tiles = ref.shape[3] // LANES
        for r in range(d):
            blk = ref[0, r].astype(F32)
            for c in range(n_lane_tiles):
                stage_ref[slot, c, pl.ds(r, n, stride=d), :] = blk[:, c * LANES:(c + 1) * LANES]
        return jnp.concatenate([stage_ref[slot, c] for c in range(n_lane_tiles)], axis=1)

    outs = [natural(r, 2 * gi) for gi, r in enumerate(o_refs)]
    lses = [natural(r, 2 * gi + 1) for gi, r in enumerate(l_refs)]
    m = functools.reduce(jnp.maximum, lses)
    es = [jnp.exp(l - m) for l in lses]
    inv = 1.0 / sum(es)
    sel_row = lax.broadcasted_iota(jnp.int32, (LANES, ODD_MIX), 0)
    sel_col = lax.broadcasted_iota(jnp.int32, (LANES, ODD_MIX), 1)
    select = jnp.where(sel_row == (sel_col // DIL_DH) * LSE_LANES, 1.0, 0.0).astype(BF16)
    mix = sum(sum(jnp.dot(p, select, preferred_element_type=F32) for p in _split3(e * inv)) * o
              for e, o in zip(es, outs))
    out_ref[...] = x_ref[...] + jnp.dot(mix.astype(BF16), w_ref[...], preferred_element_type=F32)


def _combine_proj_res(os_, lses, w, x2, seq):
    t, dm = x2.shape
    tiles = seq // ROW_TILE
    row = pl.BlockSpec((ROW_TILE, dm), lambda i: (i, 0))
    cls = [pl.BlockSpec((1, a.shape[1], ROW_TILE // a.shape[1], a.shape[3]),
                        lambda i: (i // tiles, 0, i % tiles, 0)) for a in list(os_) + list(lses)]
    return pl.pallas_call(
        _combine_proj_res_kernel,
        grid=(t // ROW_TILE,),
        in_specs=cls + [_const_spec(w.shape), row],
        out_specs=row,
        out_shape=jax.ShapeDtypeStruct((t, dm), F32),
        scratch_shapes=[pltpu.VMEM((2 * len(os_), ODD_MIX // LANES, ROW_TILE, LANES), F32)],
        compiler_params=_params(("parallel",)),
        name="combine_proj_res",
    )(*os_, *lses, w, x2)


FFN_CHUNK = 256
FFN_TILE = 512
FFN_DOWN_GROUP = 6
SUBLANES = 8
CARRY_ROWS = 2 * SUBLANES
STAGE_PAD_ROWS = 8


def _ffn_kernel(x_ref, g_ref, wup_ref, cw_ref, wdn_ref, o_ref, carry_ref, hs_ref, acc_ref, *, tiles_per_seq):
    tm, dm = x_ref.shape
    ng = tm // SUBLANES
    n_chunks = D_FF // FFN_CHUNK
    n_lane_tiles = dm // LANES

    h = _rms_rows(x_ref[...], g_ref[...])
    pitch = hs_ref.shape[1] // SUBLANES
    for c in range(n_lane_tiles):
        for s in range(SUBLANES):
            hs_ref[c, s * pitch:s * pitch + ng, :] = h[s * ng:(s + 1) * ng, c * LANES:(c + 1) * LANES]
    hb = jnp.concatenate(
        [jnp.concatenate([hs_ref[c, pl.ds(j, SUBLANES, stride=pitch), :] for j in range(ng)], axis=0)
         for c in range(n_lane_tiles)], axis=1).astype(BF16)

    @pl.when(pl.program_id(0) % tiles_per_seq == 0)
    def _():
        carry_ref[...] = jnp.zeros_like(carry_ref)

    last_sublane = lax.broadcasted_iota(jnp.int32, (SUBLANES, FFN_CHUNK), 0) == SUBLANES - 1

    def cols_of(c):
        return (slice(c * FFN_CHUNK, (c + 1) * FFN_CHUNK),
                slice(D_FF + c * FFN_CHUNK, D_FF + (c + 1) * FFN_CHUNK))

    def up(c):
        return [jnp.dot(hb, wup_ref[:, cols], preferred_element_type=F32) for cols in cols_of(c)]

    def wrap(cur, prev):
        return pltpu.roll(jnp.where(last_sublane, prev, cur), 1, 0)

    def conv(u, cols):
        prev = carry_ref[:, cols]
        w2 = wrap(u[tm - 2 * SUBLANES:tm - SUBLANES], prev[0:SUBLANES])
        w1 = wrap(u[tm - SUBLANES:], prev[SUBLANES:])
        u1 = jnp.concatenate([w1, u[:tm - SUBLANES]], axis=0)
        u2 = jnp.concatenate([w2, w1, u[:tm - 2 * SUBLANES]], axis=0)
        carry_ref[:, cols] = u[tm - CARRY_ROWS:]
        cw = cw_ref[:, cols]
        return u2 * cw[0:1] + u1 * cw[1:2] + u * cw[2:3] + cw[3:4]

    u_cur = up(0)
    pending = []
    for c in range(n_chunks):
        u_next = up(c + 1) if c + 1 < n_chunks else None
        gate, val = [conv(u, cols) for u, cols in zip(u_cur, cols_of(c))]
        pending.append((gate / (1.0 + jnp.exp(-gate)) * val).astype(BF16))
        if len(pending) == FFN_DOWN_GROUP or c + 1 == n_chunks:
            first = c + 1 - len(pending)
            act = pending[0] if len(pending) == 1 else jnp.concatenate(pending, axis=1)
            d = jnp.dot(act, wdn_ref[first * FFN_CHUNK:(c + 1) * FFN_CHUNK, :], preferred_element_type=F32)
            for lt in range(n_lane_tiles):
                piece = d[:, lt * LANES:(lt + 1) * LANES]
                acc_ref[lt] = piece if first == 0 else acc_ref[lt] + piece
            pending = []
        u_cur = u_next

    for lt in range(n_lane_tiles):
        cs = slice(lt * LANES, (lt + 1) * LANES)
        for s in range(SUBLANES):
            rs = slice(s * ng, (s + 1) * ng)
            o_ref[rs, cs] = x_ref[rs, cs] + acc_ref[lt, pl.ds(s, ng, stride=SUBLANES), :]


def _ffn(x2, g, w_up, conv_tab, w_down, seq_len):
    t, d = x2.shape
    tile = min(FFN_TILE, seq_len)
    row = pl.BlockSpec((tile, d), lambda i: (i, 0))
    return pl.pallas_call(
        functools.partial(_ffn_kernel, tiles_per_seq=seq_len // tile),
        grid=(t // tile,),
        in_specs=[row, _const_spec((1, d)), _const_spec(w_up.shape), _const_spec(conv_tab.shape),
                  _const_spec(w_down.shape)],
        out_specs=row,
        out_shape=jax.ShapeDtypeStruct((t, d), F32),
        scratch_shapes=[pltpu.VMEM((CARRY_ROWS, 2 * D_FF), F32),
                        pltpu.VMEM((d // LANES, tile + STAGE_PAD_ROWS * SUBLANES, LANES), F32),
                        pltpu.VMEM((d // LANES, tile, LANES), F32)],
        compiler_params=_params(("arbitrary",)),
        name="conv_ffn",
    )(x2, g, w_up, conv_tab, w_down)


def _even_weights(w_in, w_a2, b_a, dq_gain, dk_gain):
    sizes = [A_QK, A_QK, A_V, A_V, GLA_RANK, B_QK, B_QK, B_V]
    offs = [0]
    for c in sizes:
        offs.append(offs[-1] + c)
    aq, ak, av, ag, ar, bq, bk, bv = [w_in[:, offs[i]:offs[i + 1]] for i in range(8)]
    pad = jnp.zeros((w_in.shape[0], LANES - GLA_RANK), w_in.dtype)
    w = jnp.concatenate([aq, ak, av, ag, bq, bk, bv, ar, pad], axis=1).astype(BF16)
    hg = jnp.ones((EV_COLS,), F32)
    hg = hg.at[EV_BQ:EV_BK].set(jnp.tile(dq_gain.astype(F32), B_QK // DIFF_DH) * (DIFF_DH ** -0.5 * LOG2E))
    hg = hg.at[EV_BK:EV_BV].set(jnp.tile(dk_gain.astype(F32), B_QK // DIFF_DH))
    norm_group = tuple(DIFF_DH if EV_BQ <= c * LANES < EV_BV else 0 for c in range(EV_COLS // LANES))
    wa2p = jnp.concatenate([w_a2, jnp.zeros((LANES - GLA_RANK, A_QK), w_a2.dtype)], axis=0).astype(BF16)
    return w, hg.reshape(1, EV_COLS), norm_group, wa2p, b_a.astype(F32).reshape(1, A_QK)


def _odd_gains(q_gain, k_gain):
    hg = jnp.ones((3, 3, ODD_MIX), F32)
    hg = hg.at[:, 0].set(jnp.tile(q_gain.astype(F32), DIL_HEADS) * (DIL_DH ** -0.5 * LOG2E))
    hg = hg.at[:, 1].set(jnp.tile(k_gain.astype(F32), DIL_HEADS))
    return hg.reshape(1, ODD_IN)


def kernel(x, norm_mix, norm_ffn, ev_w_in, ev_w_a2, ev_b_a, ev_gla_gain, ev_dq_gain, ev_dk_gain,
           ev_lq1, ev_lk1, ev_lq2, ev_lk2, ev_diff_gain, ev_w_out, od_w_in, od_q_gain, od_k_gain,
           od_w_out, ffn_w_up, ffn_conv_w, ffn_conv_b, ffn_w_down):
    b, s, d = x.shape
    depth = norm_mix.shape[0]
    x2 = x.reshape(b * s, d).astype(F32)
    row = lambda v: v.astype(F32).reshape(1, -1)
    for i in range(depth):
        g_mix = row(norm_mix[i])
        if i % 2 == 0:
            e = i // 2
            lambda_init = 0.8 - 0.6 * math.exp(-0.3 * i)
            w, hg, norm_group, wa2p, ba = _even_weights(ev_w_in[e], ev_w_a2[e], ev_b_a[e],
                                                        ev_dq_gain[e], ev_dk_gain[e])
            z3 = _norm_proj(x2, g_mix, w, hg, norm_group).reshape(b, s, EV_COLS)
            oa = _gla(z3, wa2p, ba, row(ev_gla_gain[e]))
            ob = _diff_attention(z3, row(ev_lq1[e]), row(ev_lk1[e]), row(ev_lq2[e]), row(ev_lk2[e]),
                                 row(ev_diff_gain[e]), lambda_init)
            x2 = _proj2_res(oa.reshape(b * s, A_V), ob.reshape(b * s, B_V),
                            ev_w_out[e].astype(BF16), x2)
        else:
            o = i // 2
            zs = _odd_proj(x2, g_mix, od_w_in[o].astype(BF16), _odd_gains(od_q_gain[o], od_k_gain[o]), b, s)
            outs = [_dilated_group(zg) for zg in zs]
            x2 = _combine_proj_res([o_ for o_, _ in outs], [l_ for _, l_ in outs],
                                   od_w_out[o].astype(BF16), x2, s)
        conv_tab = jnp.concatenate([ffn_conv_w[i].reshape(3, 2 * D_FF), ffn_conv_b[i].reshape(1, 2 * D_FF),
                                    jnp.zeros((4, 2 * D_FF), F32)], axis=0).astype(F32)
        x2 = _ffn(x2, row(norm_ffn[i]), ffn_w_up[i].astype(BF16), conv_tab,
                  ffn_w_down[i].astype(BF16), s)
    return x2.reshape(b, s, d).astype(x.dtype)
```

```python
import functools
import math

import jax
import jax.numpy as jnp
from jax import lax
from jax.experimental import pallas as pl
from jax.experimental.pallas import tpu as pltpu

F32 = jnp.float32
BF16 = jnp.bfloat16

EPS = 1e-6
LANES = 128
MXU_COLS = 256
VMEM_LIMIT = 56 * 1024 * 1024
NEG_BIG = -1e30

D_MODEL = 1024
GLA_HEADS, GLA_DK, GLA_DV, GLA_RANK, GLA_TAU, GLA_CHUNK = 4, 64, 128, 16, 16.0, 64
DIFF_HEADS, DIFF_DH, DIFF_DV = 4, 64, 128
A_QK, A_V, B_QK, B_V = 256, 512, 512, 512
DIL_PATTERNS = ((128, 1), (512, 4), (2048, 16))
DIL_HEADS, DIL_DH = 4, 128
ODD_MIX = DIL_HEADS * DIL_DH
ODD_IN = 3 * 3 * ODD_MIX
D_FF = 2816

EV_AQ, EV_AK, EV_AV, EV_AG, EV_BQ, EV_BK, EV_BV, EV_AR = 0, 256, 512, 1024, 1536, 2048, 2560, 3072
EV_COLS = 3200

ROW_TILE = 512


def _const_spec(shape):
    nd = len(shape)
    return pl.BlockSpec(shape, lambda *_: (0,) * nd)


def _params(sem):
    return pltpu.CompilerParams(dimension_semantics=sem, vmem_limit_bytes=VMEM_LIMIT)


def _rms_rows(x, g):
    return x * lax.rsqrt(jnp.mean(x * x, axis=-1, keepdims=True) + EPS) * g


def _norm_proj_kernel(x_ref, g_ref, w_ref, hg_ref, o_ref, *, norm_group):
    hb = _rms_rows(x_ref[...], g_ref[...]).astype(BF16)
    n_cols = o_ref.shape[1]
    for c0 in range(0, n_cols, MXU_COLS):
        width = min(MXU_COLS, n_cols - c0)
        zw = jnp.dot(hb, w_ref[:, c0:c0 + width], preferred_element_type=F32)
        for sub in range(width // LANES):
            cs = slice(c0 + sub * LANES, c0 + (sub + 1) * LANES)
            z = zw[:, sub * LANES:(sub + 1) * LANES]
            grp = norm_group[cs.start // LANES]
            if grp == LANES:
                ss = jnp.sum(z * z, axis=-1, keepdims=True)
                z = z * lax.rsqrt(ss * (1.0 / LANES) + EPS) * hg_ref[:, cs]
            elif grp == LANES // 2:
                lo = lax.broadcasted_iota(jnp.int32, z.shape, 1) < grp
                sq = z * z
                s_lo = jnp.sum(jnp.where(lo, sq, 0.0), axis=-1, keepdims=True)
                s_hi = jnp.sum(jnp.where(lo, 0.0, sq), axis=-1, keepdims=True)
                ss = jnp.where(lo, s_lo, s_hi)
                z = z * lax.rsqrt(ss * (1.0 / grp) + EPS) * hg_ref[:, cs]
            o_ref[:, cs] = z.astype(o_ref.dtype)


def _norm_proj(x2, g, w, hg, norm_group):
    t, d = x2.shape
    n = w.shape[1]
    return pl.pallas_call(
        functools.partial(_norm_proj_kernel, norm_group=norm_group),
        grid=(t // ROW_TILE,),
        in_specs=[pl.BlockSpec((ROW_TILE, d), lambda i: (i, 0)),
                  _const_spec((1, d)), _const_spec((d, n)), _const_spec((1, n))],
        out_specs=pl.BlockSpec((ROW_TILE, n), lambda i: (i, 0)),
        out_shape=jax.ShapeDtypeStruct((t, n), BF16),
        compiler_params=_params(("parallel",)),
        name="norm_proj",
    )(x2, g, w, hg)


GLA_TILE = 512
GLA_BATCH = 2


def _split3(x):
    hi = x.astype(BF16)
    r1 = x - hi.astype(F32)
    mid = r1.astype(BF16)
    lo = (r1 - mid.astype(F32)).astype(BF16)
    return hi, mid, lo


def _gla_kernel(q_ref, k_ref, v_ref, gate_ref, ar_ref, wa2_ref, ba_ref, gain_ref, o_ref, state_ref):
    C = GLA_CHUNK
    tq = q_ref.shape[1]
    n_chunks = tq // C
    shift = C.bit_length() - 1

    @pl.when(pl.program_id(1) == 0)
    def _():
        state_ref[...] = jnp.zeros_like(state_ref)

    row = lax.broadcasted_iota(jnp.int32, (tq, tq), 0)
    col = lax.broadcasted_iota(jnp.int32, (tq, tq), 1)
    same = (row >> shift) == (col >> shift)
    incl = same & (col <= row)
    t_incl = jnp.where(incl, 1.0, 0.0).astype(BF16)

    pairs = [(bb, h) for bb in range(q_ref.shape[0]) for h in range(GLA_HEADS)]
    ks = [slice(h * GLA_DK, (h + 1) * GLA_DK) for h in range(GLA_HEADS)]
    vs = [slice(h * GLA_DV, (h + 1) * GLA_DV) for h in range(GLA_HEADS)]

    q_d, k_d, k_tt, b_last_t = {}, {}, {}, {}
    for bb in range(q_ref.shape[0]):
        logits = jnp.dot(ar_ref[bb], wa2_ref[...], preferred_element_type=F32) + ba_ref[...]
        log_sig = jnp.minimum(logits, 0.0) - jnp.log(1.0 + jnp.exp(-jnp.abs(logits)))
        la = log_sig * (1.0 / GLA_TAU)
        b = sum(jnp.dot(t_incl, p, preferred_element_type=F32) for p in _split3(la))
        b_last = jnp.concatenate([jnp.broadcast_to(b[(c + 1) * C - 1:(c + 1) * C, :], (C, b.shape[1]))
                                  for c in range(n_chunks)], axis=0)
        q_d[bb] = (q_ref[bb].astype(F32) * (GLA_DK ** -0.5) * jnp.exp(b)).astype(BF16)
        k_f = k_ref[bb].astype(F32)
        k_d[bb] = (k_f * jnp.exp(-b)).astype(BF16)
        k_tt[bb] = (k_f * jnp.exp(b_last - b)).T
        b_last_t[bb] = b_last.T

    att = {(bb, h): lax.dot_general(q_d[bb][:, ks[h]], k_d[bb][:, ks[h]], (((1,), (1,)), ((), ())),
                                    preferred_element_type=F32) for bb, h in pairs}
    kv = {(bb, h): jnp.dot(jnp.where(same, jnp.tile(k_tt[bb][ks[h], :], (n_chunks, 1)), 0.0).astype(BF16),
                           v_ref[bb, :, vs[h]], preferred_element_type=F32) for bb, h in pairs}
    o_intra = {(bb, h): jnp.dot(jnp.where(incl, att[bb, h], 0.0).astype(BF16), v_ref[bb, :, vs[h]],
                                preferred_element_type=F32) for bb, h in pairs}
    for bb, h in pairs:
        st = state_ref[bb, h]
        inter = []
        for c in range(n_chunks):
            rows = slice(c * C, (c + 1) * C)
            inter.append(jnp.dot(q_d[bb][rows, ks[h]], st.astype(BF16), preferred_element_type=F32))
            dec = jnp.exp(b_last_t[bb][ks[h], c * C:c * C + 1])
            st = st * dec + kv[bb, h][c * GLA_DK:(c + 1) * GLA_DK, :]
        state_ref[bb, h] = st
        o_h = o_intra[bb, h] + jnp.concatenate(inter, axis=0)
        g = gate_ref[bb, :, vs[h]].astype(F32)
        o_ref[bb, :, vs[h]] = (_rms_rows(o_h, gain_ref[...]) * (g / (1.0 + jnp.exp(-g)))).astype(o_ref.dtype)


def _gla(z3, wa2p, ba, gain):
    b, s, _ = z3.shape
    tq = min(GLA_TILE, s)
    nb = GLA_BATCH if b % GLA_BATCH == 0 else 1

    def col(width, start):
        return pl.BlockSpec((nb, tq, width), lambda bi, i: (bi, i, start // width))

    return pl.pallas_call(
        _gla_kernel,
        grid=(b // nb, s // tq),
        in_specs=[col(A_QK, EV_AQ), col(A_QK, EV_AK), col(A_V, EV_AV), col(A_V, EV_AG),
                  col(LANES, EV_AR), _const_spec((LANES, A_QK)), _const_spec((1, A_QK)),
                  _const_spec((1, GLA_DV))],
        out_specs=pl.BlockSpec((nb, tq, A_V), lambda bi, i: (bi, i, 0)),
        out_shape=jax.ShapeDtypeStruct((b, s, A_V), BF16),
        scratch_shapes=[pltpu.VMEM((nb, GLA_HEADS, GLA_DK, GLA_DV), F32)],
        compiler_params=_params(("parallel", "arbitrary")),
        name="gla",
    )(z3, z3, z3, z3, z3, wa2p, ba, gain)


DIFF_TILE_Q = 1024
DIFF_TILE_K = 512
DIFF_ROWS_FULL = 512
DIFF_ROWS_DIAG = 256
DIFF_AHEAD = 2
LOG2E = math.log2(math.e)


def _diff_kernel(q_ref, k_ref, v_ref, lq1_ref, lk1_ref, lq2_ref, lk2_ref, gain_ref,
                 o_ref, qv_ref, m_ref, acc_ref, *, lambda_init, tk):
    qi = pl.program_id(2)
    tq = q_ref.shape[1]
    span = tq // tk

    q = q_ref[0]
    first = lax.broadcasted_iota(jnp.int32, q.shape, 1) < DIFF_DH
    zero = jnp.zeros_like(q)
    qv_ref[0:tq, :] = jnp.where(first, q, zero)
    qv_ref[tq:2 * tq, :] = jnp.where(first, zero, q)

    def process(blocks, rc, fresh=False):
        tasks = []
        for bi, (block, diag) in enumerate(blocks):
            rows = pl.ds(pl.multiple_of(block * tk, tk), tk)
            k = k_ref[0, rows, :]
            v1 = jnp.concatenate([v_ref[0, rows, :], jnp.ones((tk, LANES), BF16)], axis=1)
            for c in range(2 * tq // rc):
                kw = tk if diag is None else max(min((c * rc) % tq + rc - diag * tk, tk), 0)
                if kw > 0:
                    tasks.append((k, v1, c, kw, diag, fresh and bi == 0))

        def scores(task):
            k, _, c, kw, _, _ = task
            return lax.dot_general(qv_ref[c * rc:(c + 1) * rc, :], k[0:kw],
                                   (((1,), (1,)), ((), ())), preferred_element_type=F32)

        ready = [scores(t) for t in tasks[:DIFF_AHEAD]]
        for idx, (_, v1, c, kw, diag, start) in enumerate(tasks):
            rs = slice(c * rc, (c + 1) * rc)
            s = ready.pop(0)
            if idx + DIFF_AHEAD < len(tasks):
                ready.append(scores(tasks[idx + DIFF_AHEAD]))
            if diag is not None:
                r = lax.broadcasted_iota(jnp.int32, (rc, kw), 0) + ((c * rc) % tq - diag * tk)
                col = lax.broadcasted_iota(jnp.int32, (rc, kw), 1)
                s = jnp.where(col <= r, s, NEG_BIG)
            m_cur = jnp.max(s, axis=-1, keepdims=True)
            m_new = jnp.broadcast_to(m_cur, (rc, LANES)) if start else jnp.maximum(m_ref[rs, :], m_cur)
            p = jnp.exp2(s - jnp.tile(m_new, (1, kw // LANES)))
            pv = jnp.dot(p.astype(BF16), v1[0:kw], preferred_element_type=F32)
            if start:
                acc_ref[rs, :] = pv
            else:
                alpha = jnp.exp2(m_ref[rs, :] - m_new)
                acc_ref[rs, :] = acc_ref[rs, :] * jnp.tile(alpha, (1, 2)) + pv
            m_ref[rs, :] = m_new

    def below_diagonal(j, carry):
        process([(j * span + d, None) for d in range(span)], DIFF_ROWS_FULL)
        return carry

    process([(qi * span + d, d) for d in range(span)], DIFF_ROWS_DIAG, fresh=True)
    lax.fori_loop(0, qi, below_diagonal, 0)

    lam = (jnp.exp(jnp.sum(lq1_ref[...] * lk1_ref[...], axis=-1, keepdims=True))
           - jnp.exp(jnp.sum(lq2_ref[...] * lk2_ref[...], axis=-1, keepdims=True))
           + lambda_init)
    o1 = acc_ref[0:tq, 0:DIFF_DV] / acc_ref[0:tq, DIFF_DV:]
    o2 = acc_ref[tq:2 * tq, 0:DIFF_DV] / acc_ref[tq:2 * tq, DIFF_DV:]
    o = _rms_rows(o1 - lam * o2, gain_ref[...]) * (1.0 - lambda_init)
    o_ref[0] = o.astype(o_ref.dtype)


def _diff_attention(z3, lq1, lk1, lq2, lk2, gain, lambda_init):
    b, s, _ = z3.shape
    tq = min(DIFF_TILE_Q, s)
    tk = min(DIFF_TILE_K, s)
    qc, kc, vc = EV_BQ // LANES, EV_BK // LANES, EV_BV // LANES
    return pl.pallas_call(
        functools.partial(_diff_kernel, lambda_init=lambda_init, tk=tk),
        grid=(b, DIFF_HEADS, s // tq),
        in_specs=[pl.BlockSpec((1, tq, LANES), lambda bi, h, qi: (bi, qi, qc + h)),
                  pl.BlockSpec((1, s, LANES), lambda bi, h, qi: (bi, 0, kc + h)),
                  pl.BlockSpec((1, s, LANES), lambda bi, h, qi: (bi, 0, vc + h)),
                  _const_spec((1, DIFF_DH)), _const_spec((1, DIFF_DH)),
                  _const_spec((1, DIFF_DH)), _const_spec((1, DIFF_DH)),
                  _const_spec((1, DIFF_DV))],
        out_specs=pl.BlockSpec((1, tq, LANES), lambda bi, h, qi: (bi, qi, h)),
        out_shape=jax.ShapeDtypeStruct((b, s, B_V), BF16),
        scratch_shapes=[pltpu.VMEM((2 * tq, LANES), BF16), pltpu.VMEM((2 * tq, LANES), F32),
                        pltpu.VMEM((2 * tq, 2 * DIFF_DV), F32)],
        compiler_params=_params(("parallel", "parallel", "arbitrary")),
        name="diff_attention",
    )(z3, z3, z3, lq1, lk1, lq2, lk2, gain)


def _proj2_res_kernel(a_ref, b_ref, w_ref, x_ref, o_ref):
    ka = a_ref.shape[1]
    o_ref[...] = (x_ref[...]
                  + jnp.dot(a_ref[...], w_ref[0:ka, :], preferred_element_type=F32)
                  + jnp.dot(b_ref[...], w_ref[ka:, :], preferred_element_type=F32))


def _proj2_res(a, b, w, x2):
    t, d = x2.shape
    row = lambda width: pl.BlockSpec((ROW_TILE, width), lambda i: (i, 0))
    return pl.pallas_call(
        _proj2_res_kernel,
        grid=(t // ROW_TILE,),
        in_specs=[row(a.shape[1]), row(b.shape[1]), _const_spec(w.shape), row(d)],
        out_specs=row(d),
        out_shape=jax.ShapeDtypeStruct((t, d), F32),
        compiler_params=_params(("parallel",)),
        name="proj2_res",
    )(a, b, w, x2)


DIL_WC = 128
DIL_STEP_ROWS = 512
LSE_LANES = LANES // DIL_HEADS
GROUP_COLS = 3 * ODD_MIX


def _odd_proj_kernel(x_ref, g_ref, w_ref, hg_ref, *refs):
    out_refs, h_ref = refs[:-1], refs[-1]
    tm = x_ref.shape[0]
    h = _rms_rows(x_ref[...], g_ref[...])
    n_lane_tiles = h_ref.shape[0]
    for c in range(n_lane_tiles):
        h_ref[c] = h[:, c * LANES:(c + 1) * LANES]
    for gi, o_ref in enumerate(out_refs):
        d = o_ref.shape[1]
        n = tm // d
        if d == 1:
            hb = h.astype(BF16)
        else:
            hb = jnp.concatenate(
                [jnp.concatenate([h_ref[c, pl.ds(r, n, stride=d), :] for r in range(d)], axis=0)
                 for c in range(n_lane_tiles)], axis=1).astype(BF16)
        for c0 in range(0, GROUP_COLS, MXU_COLS):
            zw = jnp.dot(hb, w_ref[:, gi * GROUP_COLS + c0:gi * GROUP_COLS + c0 + MXU_COLS],
                         preferred_element_type=F32)
            for sub in range(MXU_COLS // LANES):
                col = c0 + sub * LANES
                cs = slice(gi * GROUP_COLS + col, gi * GROUP_COLS + col + LANES)
                z = zw[:, sub * LANES:(sub + 1) * LANES]
                if col < 2 * ODD_MIX:
                    ss = jnp.sum(z * z, axis=-1, keepdims=True)
                    z = z * lax.rsqrt(ss * (1.0 / LANES) + EPS) * hg_ref[:, cs]
                z = z.astype(o_ref.dtype)
                for r in range(d):
                    o_ref[0, r, :, col:col + LANES] = z[r * n:(r + 1) * n]


def _odd_proj(x2, g, w, hg, batch, seq):
    t, dm = x2.shape
    tiles = seq // ROW_TILE
    out_specs, out_shapes = [], []
    for _, d in DIL_PATTERNS:
        out_specs.append(pl.BlockSpec((1, d, ROW_TILE // d, GROUP_COLS),
                                      lambda i: (i // tiles, 0, i % tiles, 0)))
        out_shapes.append(jax.ShapeDtypeStruct((batch, d, seq // d, GROUP_COLS), BF16))
    return pl.pallas_call(
        _odd_proj_kernel,
        grid=(t // ROW_TILE,),
        in_specs=[pl.BlockSpec((ROW_TILE, dm), lambda i: (i, 0)),
                  _const_spec((1, dm)), _const_spec(w.shape), _const_spec(hg.shape)],
        out_specs=out_specs,
        out_shape=out_shapes,
        scratch_shapes=[pltpu.VMEM((dm // LANES, ROW_TILE, LANES), F32)],
        compiler_params=_params(("parallel",)),
        name="odd_proj",
    )(x2, g, w, hg)


def _dil_kernel(q_ref, kp_ref, kc_ref, vp_ref, vc_ref, o_ref, lse_ref):
    n = pl.program_id(2)
    rows = q_ref.shape[2]
    wc = DIL_WC
    i = lax.broadcasted_iota(jnp.int32, (wc, 2 * wc), 0)
    j = lax.broadcasted_iota(jnp.int32, (wc, 2 * wc), 1)
    dist = i + wc - j
    band = (dist >= 0) & (dist <= wc)
    band_first = band & ((j >= wc) | (n > 0))
    ones = jnp.ones((2 * wc, DIL_DH), BF16)
    lane_head = lax.broadcasted_iota(jnp.int32, (wc, LANES), 1) // LSE_LANES
    tasks = [(ci, jb, h) for ci in range(q_ref.shape[1]) for jb in range(rows // wc) for h in range(DIL_HEADS)]

    def band_of(ref_prev, ref_cur, ci, jb, hs):
        if jb == 0:
            return jnp.concatenate([ref_prev[0, ci, :, hs], ref_cur[0, ci, 0:wc, hs]], axis=0)
        return ref_cur[0, ci, (jb - 1) * wc:(jb + 1) * wc, hs]

    def scores(task):
        ci, jb, h = task
        hs = slice(h * DIL_DH, (h + 1) * DIL_DH)
        return lax.dot_general(q_ref[0, ci, jb * wc:(jb + 1) * wc, hs], band_of(kp_ref, kc_ref, ci, jb, hs),
                               (((1,), (1,)), ((), ())), preferred_element_type=F32)

    s_next = scores(tasks[0])
    lse = None
    for idx, (ci, jb, h) in enumerate(tasks):
        rs = slice(jb * wc, (jb + 1) * wc)
        hs = slice(h * DIL_DH, (h + 1) * DIL_DH)
        s = s_next
        s_next = scores(tasks[idx + 1]) if idx + 1 < len(tasks) else None
        s = jnp.where(band_first if jb == 0 else band, s, NEG_BIG)
        m = jnp.max(s, axis=-1, keepdims=True)
        p = jnp.exp2(s - m)
        pv = jnp.dot(p.astype(BF16), jnp.concatenate([band_of(vp_ref, vc_ref, ci, jb, hs), ones], axis=1),
                     preferred_element_type=F32)
        l = pv[:, DIL_DH:]
        o_ref[0, ci, rs, hs] = (pv[:, 0:DIL_DH] / l).astype(o_ref.dtype)
        lse_h = m * (1.0 / LOG2E) + jnp.log(l)
        lse = lse_h if h == 0 else jnp.where(lane_head == h, lse_h, lse)
        if h == DIL_HEADS - 1:
            lse_ref[0, ci, rs, :] = lse


def _dilated_group(zg):
    b, d, length, _ = zg.shape
    rows = min(length, DIL_STEP_ROWS)
    per_row = rows // DIL_WC
    cps = max(1, min(d, DIL_STEP_ROWS // rows))

    def cur(which):
        return pl.BlockSpec((1, cps, rows, ODD_MIX), lambda bi, r, n: (bi, r, n, which))

    def prev(which):
        return pl.BlockSpec((1, cps, DIL_WC, ODD_MIX),
                            lambda bi, r, n: (bi, r, jnp.maximum(n * per_row - 1, 0), which))

    out_spec = pl.BlockSpec((1, cps, rows, ODD_MIX), lambda bi, r, n: (bi, r, n, 0))
    lse_spec = pl.BlockSpec((1, cps, rows, LANES), lambda bi, r, n: (bi, r, n, 0))
    return pl.pallas_call(
        _dil_kernel,
        grid=(b, d // cps, length // rows),
        in_specs=[cur(0), prev(1), cur(1), prev(2), cur(2)],
        out_specs=[out_spec, lse_spec],
        out_shape=[jax.ShapeDtypeStruct((b, d, length, ODD_MIX), BF16),
                   jax.ShapeDtypeStruct((b, d, length, LANES), F32)],
        compiler_params=_params(("parallel", "parallel", "arbitrary")),
        name=f"dilated_d{d}",
    )(zg, zg, zg, zg, zg)


def _combine_proj_res_kernel(*refs):
    n_groups = len(DIL_PATTERNS)
    o_refs, l_refs = refs[:n_groups], refs[n_groups:2 * n_groups]
    w_ref, x_ref, out_ref, stage_ref = refs[2 * n_groups:]
    tm = x_ref.shape[0]

    def natural(ref, slot):
        d = ref.shape[1]
        if d == 1:
            return ref[0, 0].astype(F32)
        n = tm // d
        n_lane_tiles = ref.shape[3] // LANES
        for r in range(d):
            blk = ref[0, r].astype(F32)
            for c in range(n_lane_tiles):
                stage_ref[slot, c, pl.ds(r, n, stride=d), :] = blk[:, c * LANES:(c + 1) * LANES]
        return jnp.concatenate([stage_ref[slot, c] for c in range(n_lane_tiles)], axis=1)

    outs = [natural(r, 2 * gi) for gi, r in enumerate(o_refs)]
    lses = [natural(r, 2 * gi + 1) for gi, r in enumerate(l_refs)]
    m = functools.reduce(jnp.maximum, lses)
    es = [jnp.exp(l - m) for l in lses]
    inv = 1.0 / sum(es)
    sel_row = lax.broadcasted_iota(jnp.int32, (LANES, ODD_MIX), 0)
    sel_col = lax.broadcasted_iota(jnp.int32, (LANES, ODD_MIX), 1)
    select = jnp.where(sel_row == (sel_col // DIL_DH) * LSE_LANES, 1.0, 0.0).astype(BF16)
    mix = sum(sum(jnp.dot(p, select, preferred_element_type=F32) for p in _split3(e * inv)) * o
              for e, o in zip(es, outs))
    out_ref[...] = x_ref[...] + jnp.dot(mix.astype(BF16), w_ref[...], preferred_element_type=F32)


def _combine_proj_res(os_, lses, w, x2, seq):
    t, dm = x2.shape
    tiles = seq // ROW_TILE
    row = pl.BlockSpec((ROW_TILE, dm), lambda i: (i, 0))
    cls = [pl.BlockSpec((1, a.shape[1], ROW_TILE // a.shape[1], a.shape[3]),
                        lambda i: (i // tiles, 0, i % tiles, 0)) for a in list(os_) + list(lses)]
    return pl.pallas_call(
        _combine_proj_res_kernel,
        grid=(t // ROW_TILE,),
        in_specs=cls + [_const_spec(w.shape), row],
        out_specs=row,
        out_shape=jax.ShapeDtypeStruct((t, dm), F32),
        scratch_shapes=[pltpu.VMEM((2 * len(os_), ODD_MIX // LANES, ROW_TILE, LANES), F32)],
        compiler_params=_params(("parallel",)),
        name="combine_proj_res",
    )(*os_, *lses, w, x2)


FFN_CHUNK = 256
FFN_TILE = 512
FFN_AHEAD = 1
FFN_DOWN_GROUP = 6
SUBLANES = 8
CARRY_ROWS = 2 * SUBLANES
STAGE_PAD_ROWS = 8


def _ffn_kernel(x_ref, g_ref, wup_ref, cw_ref, wdn_ref, o_ref, carry_ref, hs_ref, acc_ref, *, tiles_per_seq):
    tm, dm = x_ref.shape
    ng = tm // SUBLANES
    n_chunks = D_FF // FFN_CHUNK
    n_lane_tiles = dm // LANES

    h = _rms_rows(x_ref[...], g_ref[...])
    pitch = hs_ref.shape[1] // SUBLANES
    for c in range(n_lane_tiles):
        for s in range(SUBLANES):
            hs_ref[c, s * pitch:s * pitch + ng, :] = h[s * ng:(s + 1) * ng, c * LANES:(c + 1) * LANES]
    hb = jnp.concatenate(
        [jnp.concatenate([hs_ref[c, pl.ds(j, SUBLANES, stride=pitch), :] for j in range(ng)], axis=0)
         for c in range(n_lane_tiles)], axis=1).astype(BF16)

    @pl.when(pl.program_id(0) % tiles_per_seq == 0)
    def _():
        carry_ref[...] = jnp.zeros_like(carry_ref)

    last_sublane = lax.broadcasted_iota(jnp.int32, (SUBLANES, FFN_CHUNK), 0) == SUBLANES - 1

    def cols_of(c):
        return (slice(c * FFN_CHUNK, (c + 1) * FFN_CHUNK),
                slice(D_FF + c * FFN_CHUNK, D_FF + (c + 1) * FFN_CHUNK))

    def up(c):
        return [jnp.dot(hb, wup_ref[:, cols], preferred_element_type=F32) for cols in cols_of(c)]

    def wrap(cur, prev):
        return pltpu.roll(jnp.where(last_sublane, prev, cur), 1, 0)

    def conv(u, cols):
        prev = carry_ref[:, cols]
        w2 = wrap(u[tm - 2 * SUBLANES:tm - SUBLANES], prev[0:SUBLANES])
        w1 = wrap(u[tm - SUBLANES:], prev[SUBLANES:])
        u1 = jnp.concatenate([w1, u[:tm - SUBLANES]], axis=0)
        u2 = jnp.concatenate([w2, w1, u[:tm - 2 * SUBLANES]], axis=0)
        carry_ref[:, cols] = u[tm - CARRY_ROWS:]
        cw = cw_ref[:, cols]
        return u2 * cw[0:1] + u1 * cw[1:2] + u * cw[2:3] + cw[3:4]

    ready = [up(c) for c in range(FFN_AHEAD)]
    pending = []
    for c in range(n_chunks):
        u_cur = ready.pop(0)
        if c + FFN_AHEAD < n_chunks:
            ready.append(up(c + FFN_AHEAD))
        gate, val = [conv(u, cols) for u, cols in zip(u_cur, cols_of(c))]
        pending.append((gate / (1.0 + jnp.exp(-gate)) * val).astype(BF16))
        if len(pending) == FFN_DOWN_GROUP or c + 1 == n_chunks:
            first = c + 1 - len(pending)
            act = pending[0] if len(pending) == 1 else jnp.concatenate(pending, axis=1)
            d = jnp.dot(act, wdn_ref[first * FFN_CHUNK:(c + 1) * FFN_CHUNK, :], preferred_element_type=F32)
            for lt in range(n_lane_tiles):
                piece = d[:, lt * LANES:(lt + 1) * LANES]
                acc_ref[lt] = piece if first == 0 else acc_ref[lt] + piece
            pending = []

    for lt in range(n_lane_tiles):
        cs = slice(lt * LANES, (lt + 1) * LANES)
        for s in range(SUBLANES):
            rs = slice(s * ng, (s + 1) * ng)
            o_ref[rs, cs] = x_ref[rs, cs] + acc_ref[lt, pl.ds(s, ng, stride=SUBLANES), :]


def _ffn(x2, g, w_up, conv_tab, w_down, seq_len):
    t, d = x2.shape
    tile = min(FFN_TILE, seq_len)
    row = pl.BlockSpec((tile, d), lambda i: (i, 0))
    return pl.pallas_call(
        functools.partial(_ffn_kernel, tiles_per_seq=seq_len // tile),
        grid=(t // tile,),
        in_specs=[row, _const_spec((1, d)), _const_spec(w_up.shape), _const_spec(conv_tab.shape),
                  _const_spec(w_down.shape)],
        out_specs=row,
        out_shape=jax.ShapeDtypeStruct((t, d), F32),
        scratch_shapes=[pltpu.VMEM((CARRY_ROWS, 2 * D_FF), F32),
                        pltpu.VMEM((d // LANES, tile + STAGE_PAD_ROWS * SUBLANES, LANES), F32),
                        pltpu.VMEM((d // LANES, tile, LANES), F32)],
        compiler_params=_params(("arbitrary",)),
        name="conv_ffn",
    )(x2, g, w_up, conv_tab, w_down)


def _even_weights(w_in, w_a2, b_a, dq_gain, dk_gain):
    sizes = [A_QK, A_QK, A_V, A_V, GLA_RANK, B_QK, B_QK, B_V]
    offs = [0]
    for c in sizes:
        offs.append(offs[-1] + c)
    aq, ak, av, ag, ar, bq, bk, bv = [w_in[:, offs[i]:offs[i + 1]] for i in range(8)]
    pad = jnp.zeros((w_in.shape[0], LANES - GLA_RANK), w_in.dtype)
    w = jnp.concatenate([aq, ak, av, ag, bq, bk, bv, ar, pad], axis=1).astype(BF16)
    hg = jnp.ones((EV_COLS,), F32)
    hg = hg.at[EV_BQ:EV_BK].set(jnp.tile(dq_gain.astype(F32), B_QK // DIFF_DH) * (DIFF_DH ** -0.5 * LOG2E))
    hg = hg.at[EV_BK:EV_BV].set(jnp.tile(dk_gain.astype(F32), B_QK // DIFF_DH))
    norm_group = tuple(DIFF_DH if EV_BQ <= c * LANES < EV_BV else 0 for c in range(EV_COLS // LANES))
    wa2p = jnp.concatenate([w_a2, jnp.zeros((LANES - GLA_RANK, A_QK), w_a2.dtype)], axis=0).astype(BF16)
    return w, hg.reshape(1, EV_COLS), norm_group, wa2p, b_a.astype(F32).reshape(1, A_QK)


def _odd_gains(q_gain, k_gain):
    hg = jnp.ones((3, 3, ODD_MIX), F32)
    hg = hg.at[:, 0].set(jnp.tile(q_gain.astype(F32), DIL_HEADS) * (DIL_DH ** -0.5 * LOG2E))
    hg = hg.at[:, 1].set(jnp.tile(k_gain.astype(F32), DIL_HEADS))
    return hg.reshape(1, ODD_IN)


def kernel(x, norm_mix, norm_ffn, ev_w_in, ev_w_a2, ev_b_a, ev_gla_gain, ev_dq_gain, ev_dk_gain,
           ev_lq1, ev_lk1, ev_lq2, ev_lk2, ev_diff_gain, ev_w_out, od_w_in, od_q_gain, od_k_gain,
           od_w_out, ffn_w_up, ffn_conv_w, ffn_conv_b, ffn_w_down):
    b, s, d = x.shape
    depth = norm_mix.shape[0]
    x2 = x.reshape(b * s, d).astype(F32)
    row = lambda v: v.astype(F32).reshape(1, -1)
    for i in range(depth):
        g_mix = row(norm_mix[i])
        if i % 2 == 0:
            e = i // 2
            lambda_init = 0.8 - 0.6 * math.exp(-0.3 * i)
            w, hg, norm_group, wa2p, ba = _even_weights(ev_w_in[e], ev_w_a2[e], ev_b_a[e],
                                                        ev_dq_gain[e], ev_dk_gain[e])
            z3 = _norm_proj(x2, g_mix, w, hg, norm_group).reshape(b, s, EV_COLS)
            oa = _gla(z3, wa2p, ba, row(ev_gla_gain[e]))
            ob = _diff_attention(z3, row(ev_lq1[e]), row(ev_lk1[e]), row(ev_lq2[e]), row(ev_lk2[e]),
                                 row(ev_diff_gain[e]), lambda_init)
            x2 = _proj2_res(oa.reshape(b * s, A_V), ob.reshape(b * s, B_V),
                            ev_w_out[e].astype(BF16), x2)
        else:
            o = i // 2
            zs = _odd_proj(x2, g_mix, od_w_in[o].astype(BF16), _odd_gains(od_q_gain[o], od_k_gain[o]), b, s)
            outs = [_dilated_group(zg) for zg in zs]
            x2 = _combine_proj_res([o_ for o_, _ in outs], [l_ for _, l_ in outs],
                                   od_w_out[o].astype(BF16), x2, s)
        conv_tab = jnp.concatenate([ffn_conv_w[i].reshape(3, 2 * D_FF), ffn_conv_b[i].reshape(1, 2 * D_FF),
                                    jnp.zeros((4, 2 * D_FF), F32)], axis=0).astype(F32)
        x2 = _ffn(x2, row(norm_ffn[i]), ffn_w_up[i].astype(BF16), conv_tab,
                  ffn_w_down[i].astype(BF16), s)
    return x2.reshape(b, s, d).astype(x.dtype)
```

```python
import functools
import math

import jax
import jax.numpy as jnp
from jax import lax
from jax.experimental import pallas as pl
from jax.experimental.pallas import tpu as pltpu

F32 = jnp.float32
BF16 = jnp.bfloat16

EPS = 1e-6
LANES = 128
MXU_COLS = 256
VMEM_LIMIT = 56 * 1024 * 1024
NEG_BIG = -1e30

D_MODEL = 1024
GLA_HEADS, GLA_DK, GLA_DV, GLA_RANK, GLA_TAU, GLA_CHUNK = 4, 64, 128, 16, 16.0, 64
DIFF_HEADS, DIFF_DH, DIFF_DV = 4, 64, 128
A_QK, A_V, B_QK, B_V = 256, 512, 512, 512
DIL_PATTERNS = ((128, 1), (512, 4), (2048, 16))
DIL_HEADS, DIL_DH = 4, 128
ODD_MIX = DIL_HEADS * DIL_DH
ODD_IN = 3 * 3 * ODD_MIX
D_FF = 2816

EV_AQ, EV_AK, EV_AV, EV_AG, EV_BQ, EV_BK, EV_BV, EV_AR = 0, 256, 512, 1024, 1536, 2048, 2560, 3072
EV_COLS = 3200

ROW_TILE = 512


def _const_spec(shape):
    nd = len(shape)
    return pl.BlockSpec(shape, lambda *_: (0,) * nd)


def _params(sem):
    return pltpu.CompilerParams(dimension_semantics=sem, vmem_limit_bytes=VMEM_LIMIT)


def _rms_rows(x, g):
    return x * lax.rsqrt(jnp.mean(x * x, axis=-1, keepdims=True) + EPS) * g


def _norm_proj_kernel(x_ref, g_ref, w_ref, hg_ref, o_ref, *, norm_group):
    hb = _rms_rows(x_ref[...], g_ref[...]).astype(BF16)
    n_cols = o_ref.shape[1]
    for c0 in range(0, n_cols, MXU_COLS):
        width = min(MXU_COLS, n_cols - c0)
        zw = jnp.dot(hb, w_ref[:, c0:c0 + width], preferred_element_type=F32)
        for sub in range(width // LANES):
            cs = slice(c0 + sub * LANES, c0 + (sub + 1) * LANES)
            z = zw[:, sub * LANES:(sub + 1) * LANES]
            grp = norm_group[cs.start // LANES]
            if grp == LANES:
                ss = jnp.sum(z * z, axis=-1, keepdims=True)
                z = z * lax.rsqrt(ss * (1.0 / LANES) + EPS) * hg_ref[:, cs]
            elif grp == LANES // 2:
                lo = lax.broadcasted_iota(jnp.int32, z.shape, 1) < grp
                sq = z * z
                s_lo = jnp.sum(jnp.where(lo, sq, 0.0), axis=-1, keepdims=True)
                s_hi = jnp.sum(jnp.where(lo, 0.0, sq), axis=-1, keepdims=True)
                ss = jnp.where(lo, s_lo, s_hi)
                z = z * lax.rsqrt(ss * (1.0 / grp) + EPS) * hg_ref[:, cs]
            o_ref[:, cs] = z.astype(o_ref.dtype)


def _norm_proj(x2, g, w, hg, norm_group):
    t, d = x2.shape
    n = w.shape[1]
    return pl.pallas_call(
        functools.partial(_norm_proj_kernel, norm_group=norm_group),
        grid=(t // ROW_TILE,),
        in_specs=[pl.BlockSpec((ROW_TILE, d), lambda i: (i, 0)),
                  _const_spec((1, d)), _const_spec((d, n)), _const_spec((1, n))],
        out_specs=pl.BlockSpec((ROW_TILE, n), lambda i: (i, 0)),
        out_shape=jax.ShapeDtypeStruct((t, n), BF16),
        compiler_params=_params(("parallel",)),
        name="norm_proj",
    )(x2, g, w, hg)


GLA_TILE = 512
GLA_BATCH = 2


def _split3(x):
    hi = x.astype(BF16)
    r1 = x - hi.astype(F32)
    mid = r1.astype(BF16)
    lo = (r1 - mid.astype(F32)).astype(BF16)
    return hi, mid, lo


def _gla_kernel(q_ref, k_ref, v_ref, gate_ref, ar_ref, wa2_ref, ba_ref, gain_ref, o_ref, state_ref):
    C = GLA_CHUNK
    tq = q_ref.shape[1]
    n_chunks = tq // C
    shift = C.bit_length() - 1

    @pl.when(pl.program_id(1) == 0)
    def _():
        state_ref[...] = jnp.zeros_like(state_ref)

    row = lax.broadcasted_iota(jnp.int32, (tq, tq), 0)
    col = lax.broadcasted_iota(jnp.int32, (tq, tq), 1)
    same = (row >> shift) == (col >> shift)
    incl = same & (col <= row)
    t_incl = jnp.where(incl, 1.0, 0.0).astype(BF16)

    pairs = [(bb, h) for bb in range(q_ref.shape[0]) for h in range(GLA_HEADS)]
    ks = [slice(h * GLA_DK, (h + 1) * GLA_DK) for h in range(GLA_HEADS)]
    vs = [slice(h * GLA_DV, (h + 1) * GLA_DV) for h in range(GLA_HEADS)]

    q_d, k_d, k_tt, b_last_t = {}, {}, {}, {}
    for bb in range(q_ref.shape[0]):
        logits = jnp.dot(ar_ref[bb], wa2_ref[...], preferred_element_type=F32) + ba_ref[...]
        log_sig = jnp.minimum(logits, 0.0) - jnp.log(1.0 + jnp.exp(-jnp.abs(logits)))
        la = log_sig * (1.0 / GLA_TAU)
        b = sum(jnp.dot(t_incl, p, preferred_element_type=F32) for p in _split3(la))
        b_last = jnp.concatenate([jnp.broadcast_to(b[(c + 1) * C - 1:(c + 1) * C, :], (C, b.shape[1]))
                                  for c in range(n_chunks)], axis=0)
        q_d[bb] = (q_ref[bb].astype(F32) * (GLA_DK ** -0.5) * jnp.exp(b)).astype(BF16)
        k_f = k_ref[bb].astype(F32)
        k_d[bb] = (k_f * jnp.exp(-b)).astype(BF16)
        k_tt[bb] = (k_f * jnp.exp(b_last - b)).T
        b_last_t[bb] = b_last.T

    att = {(bb, h): lax.dot_general(q_d[bb][:, ks[h]], k_d[bb][:, ks[h]], (((1,), (1,)), ((), ())),
                                    preferred_element_type=F32) for bb, h in pairs}
    kv = {(bb, h): jnp.dot(jnp.where(same, jnp.tile(k_tt[bb][ks[h], :], (n_chunks, 1)), 0.0).astype(BF16),
                           v_ref[bb, :, vs[h]], preferred_element_type=F32) for bb, h in pairs}
    o_intra = {(bb, h): jnp.dot(jnp.where(incl, att[bb, h], 0.0).astype(BF16), v_ref[bb, :, vs[h]],
                                preferred_element_type=F32) for bb, h in pairs}
    for bb, h in pairs:
        st = state_ref[bb, h]
        inter = []
        for c in range(n_chunks):
            rows = slice(c * C, (c + 1) * C)
            inter.append(jnp.dot(q_d[bb][rows, ks[h]], st.astype(BF16), preferred_element_type=F32))
            dec = jnp.exp(b_last_t[bb][ks[h], c * C:c * C + 1])
            st = st * dec + kv[bb, h][c * GLA_DK:(c + 1) * GLA_DK, :]
        state_ref[bb, h] = st
        o_h = o_intra[bb, h] + jnp.concatenate(inter, axis=0)
        g = gate_ref[bb, :, vs[h]].astype(F32)
        o_ref[bb, :, vs[h]] = (_rms_rows(o_h, gain_ref[...]) * (g / (1.0 + jnp.exp(-g)))).astype(o_ref.dtype)


def _gla(z3, wa2p, ba, gain):
    b, s, _ = z3.shape
    tq = min(GLA_TILE, s)
    nb = GLA_BATCH if b % GLA_BATCH == 0 else 1

    def col(width, start):
        return pl.BlockSpec((nb, tq, width), lambda bi, i: (bi, i, start // width))

    return pl.pallas_call(
        _gla_kernel,
        grid=(b // nb, s // tq),
        in_specs=[col(A_QK, EV_AQ), col(A_QK, EV_AK), col(A_V, EV_AV), col(A_V, EV_AG),
                  col(LANES, EV_AR), _const_spec((LANES, A_QK)), _const_spec((1, A_QK)),
                  _const_spec((1, GLA_DV))],
        out_specs=pl.BlockSpec((nb, tq, A_V), lambda bi, i: (bi, i, 0)),
        out_shape=jax.ShapeDtypeStruct((b, s, A_V), BF16),
        scratch_shapes=[pltpu.VMEM((nb, GLA_HEADS, GLA_DK, GLA_DV), F32)],
        compiler_params=_params(("parallel", "arbitrary")),
        name="gla",
    )(z3, z3, z3, z3, z3, wa2p, ba, gain)


DIFF_TILE_Q = 1024
DIFF_TILE_K = 512
DIFF_ROWS_FULL = 512
DIFF_ROWS_DIAG = 256
DIFF_AHEAD = 2
LOG2E = math.log2(math.e)


def _diff_kernel(q_ref, k_ref, v_ref, lq1_ref, lk1_ref, lq2_ref, lk2_ref, gain_ref,
                 o_ref, qv_ref, m_ref, acc_ref, *, lambda_init, tk):
    qi = pl.program_id(2)
    tq = q_ref.shape[1]
    span = tq // tk

    q = q_ref[0]
    first = lax.broadcasted_iota(jnp.int32, q.shape, 1) < DIFF_DH
    zero = jnp.zeros_like(q)
    qv_ref[0:tq, :] = jnp.where(first, q, zero)
    qv_ref[tq:2 * tq, :] = jnp.where(first, zero, q)

    def process(blocks, rc, fresh=False):
        tasks = []
        for bi, (block, diag) in enumerate(blocks):
            rows = pl.ds(pl.multiple_of(block * tk, tk), tk)
            k = k_ref[0, rows, :]
            v1 = jnp.concatenate([v_ref[0, rows, :], jnp.ones((tk, LANES), BF16)], axis=1)
            for c in range(2 * tq // rc):
                kw = tk if diag is None else max(min((c * rc) % tq + rc - diag * tk, tk), 0)
                if kw > 0:
                    tasks.append((k, v1, c, kw, diag, fresh and bi == 0))

        def scores(task):
            k, _, c, kw, _, _ = task
            return lax.dot_general(qv_ref[c * rc:(c + 1) * rc, :], k[0:kw],
                                   (((1,), (1,)), ((), ())), preferred_element_type=F32)

        ready = [scores(t) for t in tasks[:DIFF_AHEAD]]
        for idx, (_, v1, c, kw, diag, start) in enumerate(tasks):
            rs = slice(c * rc, (c + 1) * rc)
            s = ready.pop(0)
            if idx + DIFF_AHEAD < len(tasks):
                ready.append(scores(tasks[idx + DIFF_AHEAD]))
            if diag is not None:
                r = lax.broadcasted_iota(jnp.int32, (rc, kw), 0) + ((c * rc) % tq - diag * tk)
                col = lax.broadcasted_iota(jnp.int32, (rc, kw), 1)
                s = jnp.where(col <= r, s, NEG_BIG)
            m_cur = jnp.max(s, axis=-1, keepdims=True)
            m_new = jnp.broadcast_to(m_cur, (rc, LANES)) if start else jnp.maximum(m_ref[rs, :], m_cur)
            p = jnp.exp2(s - jnp.tile(m_new, (1, kw // LANES)))
            pv = jnp.dot(p.astype(BF16), v1[0:kw], preferred_element_type=F32)
            if start:
                acc_ref[rs, :] = pv
            else:
                alpha = jnp.exp2(m_ref[rs, :] - m_new)
                acc_ref[rs, :] = acc_ref[rs, :] * jnp.tile(alpha, (1, 2)) + pv
            m_ref[rs, :] = m_new

    def below_diagonal(j, carry):
        process([(j * span + d, None) for d in range(span)], DIFF_ROWS_FULL)
        return carry

    process([(qi * span + d, d) for d in range(span)], DIFF_ROWS_DIAG, fresh=True)
    lax.fori_loop(0, qi, below_diagonal, 0)

    lam = (jnp.exp(jnp.sum(lq1_ref[...] * lk1_ref[...], axis=-1, keepdims=True))
           - jnp.exp(jnp.sum(lq2_ref[...] * lk2_ref[...], axis=-1, keepdims=True))
           + lambda_init)
    o1 = acc_ref[0:tq, 0:DIFF_DV] / acc_ref[0:tq, DIFF_DV:]
    o2 = acc_ref[tq:2 * tq, 0:DIFF_DV] / acc_ref[tq:2 * tq, DIFF_DV:]
    o = _rms_rows(o1 - lam * o2, gain_ref[...]) * (1.0 - lambda_init)
    o_ref[0] = o.astype(o_ref.dtype)


def _diff_attention(z3, lq1, lk1, lq2, lk2, gain, lambda_init):
    b, s, _ = z3.shape
    tq = min(DIFF_TILE_Q, s)
    tk = min(DIFF_TILE_K, s)
    qc, kc, vc = EV_BQ // LANES, EV_BK // LANES, EV_BV // LANES
    return pl.pallas_call(
        functools.partial(_diff_kernel, lambda_init=lambda_init, tk=tk),
        grid=(b, DIFF_HEADS, s // tq),
        in_specs=[pl.BlockSpec((1, tq, LANES), lambda bi, h, qi: (bi, qi, qc + h)),
                  pl.BlockSpec((1, s, LANES), lambda bi, h, qi: (bi, 0, kc + h)),
                  pl.BlockSpec((1, s, LANES), lambda bi, h, qi: (bi, 0, vc + h)),
                  _const_spec((1, DIFF_DH)), _const_spec((1, DIFF_DH)),
                  _const_spec((1, DIFF_DH)), _const_spec((1, DIFF_DH)),
                  _const_spec((1, DIFF_DV))],
        out_specs=pl.BlockSpec((1, tq, LANES), lambda bi, h, qi: (bi, qi, h)),
        out_shape=jax.ShapeDtypeStruct((b, s, B_V), BF16),
        scratch_shapes=[pltpu.VMEM((2 * tq, LANES), BF16), pltpu.VMEM((2 * tq, LANES), F32),
                        pltpu.VMEM((2 * tq, 2 * DIFF_DV), F32)],
        compiler_params=_params(("parallel", "parallel", "arbitrary")),
        name="diff_attention",
    )(z3, z3, z3, lq1, lk1, lq2, lk2, gain)


def _proj2_kernel(a_ref, b_ref, w_ref, o_ref):
    ka = a_ref.shape[1]
    o_ref[...] = (jnp.dot(a_ref[...], w_ref[0:ka, :], preferred_element_type=F32)
                  + jnp.dot(b_ref[...], w_ref[ka:, :], preferred_element_type=F32))


def _proj2(a, b, w):
    t, d = a.shape[0], w.shape[1]
    row = lambda width: pl.BlockSpec((ROW_TILE, width), lambda i: (i, 0))
    return pl.pallas_call(
        _proj2_kernel,
        grid=(t // ROW_TILE,),
        in_specs=[row(a.shape[1]), row(b.shape[1]), _const_spec(w.shape)],
        out_specs=row(d),
        out_shape=jax.ShapeDtypeStruct((t, d), F32),
        compiler_params=_params(("parallel",)),
        name="proj2",
    )(a, b, w)


DIL_WC = 128
DIL_STEP_ROWS = 512
LSE_LANES = LANES // DIL_HEADS
GROUP_COLS = 3 * ODD_MIX


def _odd_proj_kernel(x_ref, g_ref, w_ref, hg_ref, *refs):
    out_refs, h_ref = refs[:-1], refs[-1]
    tm = x_ref.shape[0]
    h = _rms_rows(x_ref[...], g_ref[...])
    n_lane_tiles = h_ref.shape[0]
    for c in range(n_lane_tiles):
        h_ref[c] = h[:, c * LANES:(c + 1) * LANES]
    for gi, o_ref in enumerate(out_refs):
        d = o_ref.shape[1]
        n = tm // d
        if d == 1:
            hb = h.astype(BF16)
        else:
            hb = jnp.concatenate(
                [jnp.concatenate([h_ref[c, pl.ds(r, n, stride=d), :] for r in range(d)], axis=0)
                 for c in range(n_lane_tiles)], axis=1).astype(BF16)
        for c0 in range(0, GROUP_COLS, MXU_COLS):
            zw = jnp.dot(hb, w_ref[:, gi * GROUP_COLS + c0:gi * GROUP_COLS + c0 + MXU_COLS],
                         preferred_element_type=F32)
            for sub in range(MXU_COLS // LANES):
                col = c0 + sub * LANES
                cs = slice(gi * GROUP_COLS + col, gi * GROUP_COLS + col + LANES)
                z = zw[:, sub * LANES:(sub + 1) * LANES]
                if col < 2 * ODD_MIX:
                    ss = jnp.sum(z * z, axis=-1, keepdims=True)
                    z = z * lax.rsqrt(ss * (1.0 / LANES) + EPS) * hg_ref[:, cs]
                z = z.astype(o_ref.dtype)
                for r in range(d):
                    o_ref[0, r, :, col:col + LANES] = z[r * n:(r + 1) * n]


def _odd_proj(x2, g, w, hg, batch, seq):
    t, dm = x2.shape
    tiles = seq // ROW_TILE
    out_specs, out_shapes = [], []
    for _, d in DIL_PATTERNS:
        out_specs.append(pl.BlockSpec((1, d, ROW_TILE // d, GROUP_COLS),
                                      lambda i: (i // tiles, 0, i % tiles, 0)))
        out_shapes.append(jax.ShapeDtypeStruct((batch, d, seq // d, GROUP_COLS), BF16))
    return pl.pallas_call(
        _odd_proj_kernel,
        grid=(t // ROW_TILE,),
        in_specs=[pl.BlockSpec((ROW_TILE, dm), lambda i: (i, 0)),
                  _const_spec((1, dm)), _const_spec(w.shape), _const_spec(hg.shape)],
        out_specs=out_specs,
        out_shape=out_shapes,
        scratch_shapes=[pltpu.VMEM((dm // LANES, ROW_TILE, LANES), F32)],
        compiler_params=_params(("parallel",)),
        name="odd_proj",
    )(x2, g, w, hg)


def _dil_kernel(q_ref, kp_ref, kc_ref, vp_ref, vc_ref, o_ref, lse_ref):
    n = pl.program_id(2)
    rows = q_ref.shape[2]
    wc = DIL_WC
    i = lax.broadcasted_iota(jnp.int32, (wc, 2 * wc), 0)
    j = lax.broadcasted_iota(jnp.int32, (wc, 2 * wc), 1)
    dist = i + wc - j
    band = (dist >= 0) & (dist <= wc)
    band_first = band & ((j >= wc) | (n > 0))
    ones = jnp.ones((2 * wc, DIL_DH), BF16)
    lane_head = lax.broadcasted_iota(jnp.int32, (wc, LANES), 1) // LSE_LANES
    tasks = [(ci, jb, h) for ci in range(q_ref.shape[1]) for jb in range(rows // wc) for h in range(DIL_HEADS)]

    def band_of(ref_prev, ref_cur, ci, jb, hs):
        if jb == 0:
            return jnp.concatenate([ref_prev[0, ci, :, hs], ref_cur[0, ci, 0:wc, hs]], axis=0)
        return ref_cur[0, ci, (jb - 1) * wc:(jb + 1) * wc, hs]

    def scores(task):
        ci, jb, h = task
        hs = slice(h * DIL_DH, (h + 1) * DIL_DH)
        return lax.dot_general(q_ref[0, ci, jb * wc:(jb + 1) * wc, hs], band_of(kp_ref, kc_ref, ci, jb, hs),
                               (((1,), (1,)), ((), ())), preferred_element_type=F32)

    s_next = scores(tasks[0])
    lse = None
    for idx, (ci, jb, h) in enumerate(tasks):
        rs = slice(jb * wc, (jb + 1) * wc)
        hs = slice(h * DIL_DH, (h + 1) * DIL_DH)
        s = s_next
        s_next = scores(tasks[idx + 1]) if idx + 1 < len(tasks) else None
        s = jnp.where(band_first if jb == 0 else band, s, NEG_BIG)
        m = jnp.max(s, axis=-1, keepdims=True)
        p = jnp.exp2(s - m)
        pv = jnp.dot(p.astype(BF16), jnp.concatenate([band_of(vp_ref, vc_ref, ci, jb, hs), ones], axis=1),
                     preferred_element_type=F32)
        l = pv[:, DIL_DH:]
        o_ref[0, ci, rs, hs] = (pv[:, 0:DIL_DH] / l).astype(o_ref.dtype)
        lse_h = m * (1.0 / LOG2E) + jnp.log(l)
        lse = lse_h if h == 0 else jnp.where(lane_head == h, lse_h, lse)
        if h == DIL_HEADS - 1:
            lse_ref[0, ci, rs, :] = lse


def _dilated_group(zg):
    b, d, length, _ = zg.shape
    rows = min(length, DIL_STEP_ROWS)
    per_row = rows // DIL_WC
    cps = max(1, min(d, DIL_STEP_ROWS // rows))

    def cur(which):
        return pl.BlockSpec((1, cps, rows, ODD_MIX), lambda bi, r, n: (bi, r, n, which))

    def prev(which):
        return pl.BlockSpec((1, cps, DIL_WC, ODD_MIX),
                            lambda bi, r, n: (bi, r, jnp.maximum(n * per_row - 1, 0), which))

    out_spec = pl.BlockSpec((1, cps, rows, ODD_MIX), lambda bi, r, n: (bi, r, n, 0))
    lse_spec = pl.BlockSpec((1, cps, rows, LANES), lambda bi, r, n: (bi, r, n, 0))
    return pl.pallas_call(
        _dil_kernel,
        grid=(b, d // cps, length // rows),
        in_specs=[cur(0), prev(1), cur(1), prev(2), cur(2)],
        out_specs=[out_spec, lse_spec],
        out_shape=[jax.ShapeDtypeStruct((b, d, length, ODD_MIX), BF16),
                   jax.ShapeDtypeStruct((b, d, length, LANES), F32)],
        compiler_params=_params(("parallel", "parallel", "arbitrary")),
        name=f"dilated_d{d}",
    )(zg, zg, zg, zg, zg)


def _combine_proj_kernel(*refs):
    n_groups = len(DIL_PATTERNS)
    o_refs, l_refs = refs[:n_groups], refs[n_groups:2 * n_groups]
    w_ref, out_ref, stage_ref = refs[2 * n_groups:]
    tm = out_ref.shape[0]

    def natural(ref, slot):
        d = ref.shape[1]
        if d == 1:
            return ref[0, 0].astype(F32)
        n = tm // d
        n_lane_tiles = ref.shape[3] // LANES
        for r in range(d):
            blk = ref[0, r].astype(F32)
            for c in range(n_lane_tiles):
                stage_ref[slot, c, pl.ds(r, n, stride=d), :] = blk[:, c * LANES:(c + 1) * LANES]
        return jnp.concatenate([stage_ref[slot, c] for c in range(n_lane_tiles)], axis=1)

    outs = [natural(r, 2 * gi) for gi, r in enumerate(o_refs)]
    lses = [natural(r, 2 * gi + 1) for gi, r in enumerate(l_refs)]
    m = functools.reduce(jnp.maximum, lses)
    es = [jnp.exp(l - m) for l in lses]
    inv = 1.0 / sum(es)
    sel_row = lax.broadcasted_iota(jnp.int32, (LANES, ODD_MIX), 0)
    sel_col = lax.broadcasted_iota(jnp.int32, (LANES, ODD_MIX), 1)
    select = jnp.where(sel_row == (sel_col // DIL_DH) * LSE_LANES, 1.0, 0.0).astype(BF16)
    select3 = jnp.concatenate([select] * 3, axis=0)

    def spread(w):
        return jnp.dot(jnp.concatenate(_split3(w), axis=1), select3, preferred_element_type=F32)

    mix = outs[-1] + sum(spread(e * inv) * (o - outs[-1]) for e, o in zip(es[:-1], outs[:-1]))
    out_ref[...] = jnp.dot(mix.astype(BF16), w_ref[...], preferred_element_type=F32)


def _combine_proj(os_, lses, w, seq):
    t, dm = os_[0].shape[0] * seq, w.shape[1]
    tiles = seq // ROW_TILE
    row = pl.BlockSpec((ROW_TILE, dm), lambda i: (i, 0))
    cls = [pl.BlockSpec((1, a.shape[1], ROW_TILE // a.shape[1], a.shape[3]),
                        lambda i: (i // tiles, 0, i % tiles, 0)) for a in list(os_) + list(lses)]
    return pl.pallas_call(
        _combine_proj_kernel,
        grid=(t // ROW_TILE,),
        in_specs=cls + [_const_spec(w.shape)],
        out_specs=row,
        out_shape=jax.ShapeDtypeStruct((t, dm), F32),
        scratch_shapes=[pltpu.VMEM((2 * len(os_), ODD_MIX // LANES, ROW_TILE, LANES), F32)],
        compiler_params=_params(("parallel",)),
        name="combine_proj",
    )(*os_, *lses, w)


FFN_CHUNK = 256
FFN_TILE = 512
FFN_AHEAD = 1
FFN_DOWN_GROUP = 6
SUBLANES = 8
CARRY_ROWS = 2 * SUBLANES
STAGE_PAD_ROWS = 8


def _ffn_kernel(x_ref, upd_ref, g_ref, wup_ref, cw_ref, wdn_ref, o_ref, carry_ref, hs_ref, acc_ref,
                *, tiles_per_seq):
    tm, dm = x_ref.shape
    ng = tm // SUBLANES
    n_chunks = D_FF // FFN_CHUNK
    n_lane_tiles = dm // LANES

    h = _rms_rows(x_ref[...] + upd_ref[...], g_ref[...])
    pitch = hs_ref.shape[1] // SUBLANES
    for c in range(n_lane_tiles):
        for s in range(SUBLANES):
            hs_ref[c, s * pitch:s * pitch + ng, :] = h[s * ng:(s + 1) * ng, c * LANES:(c + 1) * LANES]
    hb = jnp.concatenate(
        [jnp.concatenate([hs_ref[c, pl.ds(j, SUBLANES, stride=pitch), :] for j in range(ng)], axis=0)
         for c in range(n_lane_tiles)], axis=1).astype(BF16)

    @pl.when(pl.program_id(0) % tiles_per_seq == 0)
    def _():
        carry_ref[...] = jnp.zeros_like(carry_ref)

    last_sublane = lax.broadcasted_iota(jnp.int32, (SUBLANES, FFN_CHUNK), 0) == SUBLANES - 1

    def cols_of(c):
        return (slice(c * FFN_CHUNK, (c + 1) * FFN_CHUNK),
                slice(D_FF + c * FFN_CHUNK, D_FF + (c + 1) * FFN_CHUNK))

    def up(c):
        return [jnp.dot(hb, wup_ref[:, cols], preferred_element_type=F32) for cols in cols_of(c)]

    def wrap(cur, prev):
        return pltpu.roll(jnp.where(last_sublane, prev, cur), 1, 0)

    def conv(u, cols):
        prev = carry_ref[:, cols]
        w2 = wrap(u[tm - 2 * SUBLANES:tm - SUBLANES], prev[0:SUBLANES])
        w1 = wrap(u[tm - SUBLANES:], prev[SUBLANES:])
        u1 = jnp.concatenate([w1, u[:tm - SUBLANES]], axis=0)
        u2 = jnp.concatenate([w2, w1, u[:tm - 2 * SUBLANES]], axis=0)
        carry_ref[:, cols] = u[tm - CARRY_ROWS:]
        cw = cw_ref[:, cols]
        return u2 * cw[0:1] + u1 * cw[1:2] + u * cw[2:3] + cw[3:4]

    ready = [up(c) for c in range(FFN_AHEAD)]
    pending = []
    for c in range(n_chunks):
        u_cur = ready.pop(0)
        if c + FFN_AHEAD < n_chunks:
            ready.append(up(c + FFN_AHEAD))
        gate, val = [conv(u, cols) for u, cols in zip(u_cur, cols_of(c))]
        pending.append((gate / (1.0 + jnp.exp(-gate)) * val).astype(BF16))
        if len(pending) == FFN_DOWN_GROUP or c + 1 == n_chunks:
            first = c + 1 - len(pending)
            act = pending[0] if len(pending) == 1 else jnp.concatenate(pending, axis=1)
            d = jnp.dot(act, wdn_ref[first * FFN_CHUNK:(c + 1) * FFN_CHUNK, :], preferred_element_type=F32)
            for lt in range(n_lane_tiles):
                piece = d[:, lt * LANES:(lt + 1) * LANES]
                acc_ref[lt] = piece if first == 0 else acc_ref[lt] + piece
            pending = []

    for lt in range(n_lane_tiles):
        cs = slice(lt * LANES, (lt + 1) * LANES)
        for s in range(SUBLANES):
            rs = slice(s * ng, (s + 1) * ng)
            o_ref[rs, cs] = x_ref[rs, cs] + upd_ref[rs, cs] + acc_ref[lt, pl.ds(s, ng, stride=SUBLANES), :]


def _ffn(x2, update, g, w_up, conv_tab, w_down, seq_len):
    t, d = x2.shape
    tile = min(FFN_TILE, seq_len)
    row = pl.BlockSpec((tile, d), lambda i: (i, 0))
    return pl.pallas_call(
        functools.partial(_ffn_kernel, tiles_per_seq=seq_len // tile),
        grid=(t // tile,),
        in_specs=[row, row, _const_spec((1, d)), _const_spec(w_up.shape), _const_spec(conv_tab.shape),
                  _const_spec(w_down.shape)],
        out_specs=row,
        out_shape=jax.ShapeDtypeStruct((t, d), F32),
        scratch_shapes=[pltpu.VMEM((CARRY_ROWS, 2 * D_FF), F32),
                        pltpu.VMEM((d // LANES, tile + STAGE_PAD_ROWS * SUBLANES, LANES), F32),
                        pltpu.VMEM((d // LANES, tile, LANES), F32)],
        compiler_params=_params(("arbitrary",)),
        name="conv_ffn",
    )(x2, update, g, w_up, conv_tab, w_down)


def _even_weights(w_in, w_a2, b_a, dq_gain, dk_gain):
    sizes = [A_QK, A_QK, A_V, A_V, GLA_RANK, B_QK, B_QK, B_V]
    offs = [0]
    for c in sizes:
        offs.append(offs[-1] + c)
    aq, ak, av, ag, ar, bq, bk, bv = [w_in[:, offs[i]:offs[i + 1]] for i in range(8)]
    pad = jnp.zeros((w_in.shape[0], LANES - GLA_RANK), w_in.dtype)
    w = jnp.concatenate([aq, ak, av, ag, bq, bk, bv, ar, pad], axis=1).astype(BF16)
    hg = jnp.ones((EV_COLS,), F32)
    hg = hg.at[EV_BQ:EV_BK].set(jnp.tile(dq_gain.astype(F32), B_QK // DIFF_DH) * (DIFF_DH ** -0.5 * LOG2E))
    hg = hg.at[EV_BK:EV_BV].set(jnp.tile(dk_gain.astype(F32), B_QK // DIFF_DH))
    norm_group = tuple(DIFF_DH if EV_BQ <= c * LANES < EV_BV else 0 for c in range(EV_COLS // LANES))
    wa2p = jnp.concatenate([w_a2, jnp.zeros((LANES - GLA_RANK, A_QK), w_a2.dtype)], axis=0).astype(BF16)
    return w, hg.reshape(1, EV_COLS), norm_group, wa2p, b_a.astype(F32).reshape(1, A_QK)


def _odd_gains(q_gain, k_gain):
    hg = jnp.ones((3, 3, ODD_MIX), F32)
    hg = hg.at[:, 0].set(jnp.tile(q_gain.astype(F32), DIL_HEADS) * (DIL_DH ** -0.5 * LOG2E))
    hg = hg.at[:, 1].set(jnp.tile(k_gain.astype(F32), DIL_HEADS))
    return hg.reshape(1, ODD_IN)


def kernel(x, norm_mix, norm_ffn, ev_w_in, ev_w_a2, ev_b_a, ev_gla_gain, ev_dq_gain, ev_dk_gain,
           ev_lq1, ev_lk1, ev_lq2, ev_lk2, ev_diff_gain, ev_w_out, od_w_in, od_q_gain, od_k_gain,
           od_w_out, ffn_w_up, ffn_conv_w, ffn_conv_b, ffn_w_down):
    b, s, d = x.shape
    depth = norm_mix.shape[0]
    x2 = x.reshape(b * s, d).astype(F32)
    row = lambda v: v.astype(F32).reshape(1, -1)
    for i in range(depth):
        g_mix = row(norm_mix[i])
        if i % 2 == 0:
            e = i // 2
            lambda_init = 0.8 - 0.6 * math.exp(-0.3 * i)
            w, hg, norm_group, wa2p, ba = _even_weights(ev_w_in[e], ev_w_a2[e], ev_b_a[e],
                                                        ev_dq_gain[e], ev_dk_gain[e])
            z3 = _norm_proj(x2, g_mix, w, hg, norm_group).reshape(b, s, EV_COLS)
            oa = _gla(z3, wa2p, ba, row(ev_gla_gain[e]))
            ob = _diff_attention(z3, row(ev_lq1[e]), row(ev_lk1[e]), row(ev_lq2[e]), row(ev_lk2[e]),
                                 row(ev_diff_gain[e]), lambda_init)
            update = _proj2(oa.reshape(b * s, A_V), ob.reshape(b * s, B_V), ev_w_out[e].astype(BF16))
        else:
            o = i // 2
            zs = _odd_proj(x2, g_mix, od_w_in[o].astype(BF16), _odd_gains(od_q_gain[o], od_k_gain[o]), b, s)
            outs = [_dilated_group(zg) for zg in zs]
            update = _combine_proj([o_ for o_, _ in outs], [l_ for _, l_ in outs], od_w_out[o].astype(BF16), s)
        conv_tab = jnp.concatenate([ffn_conv_w[i].reshape(3, 2 * D_FF), ffn_conv_b[i].reshape(1, 2 * D_FF),
                                    jnp.zeros((4, 2 * D_FF), F32)], axis=0).astype(F32)
        x2 = _ffn(x2, update, row(norm_ffn[i]), ffn_w_up[i].astype(BF16), conv_tab,
                  ffn_w_down[i].astype(BF16), s)
    return x2.reshape(b, s, d).astype(x.dtype)
```

```python
import functools
import math

import jax
import jax.numpy as jnp
from jax import lax
from jax.experimental import pallas as pl
from jax.experimental.pallas import tpu as pltpu

F32 = jnp.float32
BF16 = jnp.bfloat16

EPS = 1e-6
LANES = 128
MXU_COLS = 256
VMEM_LIMIT = 56 * 1024 * 1024
NEG_BIG = -1e30

D_MODEL = 1024
GLA_HEADS, GLA_DK, GLA_DV, GLA_RANK, GLA_TAU, GLA_CHUNK = 4, 64, 128, 16, 16.0, 64
DIFF_HEADS, DIFF_DH, DIFF_DV = 4, 64, 128
A_QK, A_V, B_QK, B_V = 256, 512, 512, 512
DIL_PATTERNS = ((128, 1), (512, 4), (2048, 16))
DIL_HEADS, DIL_DH = 4, 128
ODD_MIX = DIL_HEADS * DIL_DH
ODD_IN = 3 * 3 * ODD_MIX
D_FF = 2816

EV_AQ, EV_AK, EV_AV, EV_AG, EV_BQ, EV_BK, EV_BV, EV_AR = 0, 256, 512, 1024, 1536, 2048, 2560, 3072
EV_COLS = 3200

ROW_TILE = 512


def _const_spec(shape):
    nd = len(shape)
    return pl.BlockSpec(shape, lambda *_: (0,) * nd)


def _params(sem):
    return pltpu.CompilerParams(dimension_semantics=sem, vmem_limit_bytes=VMEM_LIMIT)


def _rms_rows(x, g):
    return x * lax.rsqrt(jnp.mean(x * x, axis=-1, keepdims=True) + EPS) * g


def _norm_proj_kernel(x_ref, g_ref, w_ref, hg_ref, o_ref, *, norm_group):
    hb = _rms_rows(x_ref[...], g_ref[...]).astype(BF16)
    n_cols = o_ref.shape[1]
    for c0 in range(0, n_cols, MXU_COLS):
        width = min(MXU_COLS, n_cols - c0)
        zw = jnp.dot(hb, w_ref[:, c0:c0 + width], preferred_element_type=F32)
        for sub in range(width // LANES):
            cs = slice(c0 + sub * LANES, c0 + (sub + 1) * LANES)
            z = zw[:, sub * LANES:(sub + 1) * LANES]
            grp = norm_group[cs.start // LANES]
            if grp == LANES:
                ss = jnp.sum(z * z, axis=-1, keepdims=True)
                z = z * lax.rsqrt(ss * (1.0 / LANES) + EPS) * hg_ref[:, cs]
            elif grp == LANES // 2:
                lo = lax.broadcasted_iota(jnp.int32, z.shape, 1) < grp
                sq = z * z
                s_lo = jnp.sum(jnp.where(lo, sq, 0.0), axis=-1, keepdims=True)
                s_hi = jnp.sum(jnp.where(lo, 0.0, sq), axis=-1, keepdims=True)
                ss = jnp.where(lo, s_lo, s_hi)
                z = z * lax.rsqrt(ss * (1.0 / grp) + EPS) * hg_ref[:, cs]
            o_ref[:, cs] = z.astype(o_ref.dtype)


def _norm_proj(x2, g, w, hg, norm_group):
    t, d = x2.shape
    n = w.shape[1]
    return pl.pallas_call(
        functools.partial(_norm_proj_kernel, norm_group=norm_group),
        grid=(t // ROW_TILE,),
        in_specs=[pl.BlockSpec((ROW_TILE, d), lambda i: (i, 0)),
                  _const_spec((1, d)), _const_spec((d, n)), _const_spec((1, n))],
        out_specs=pl.BlockSpec((ROW_TILE, n), lambda i: (i, 0)),
        out_shape=jax.ShapeDtypeStruct((t, n), BF16),
        compiler_params=_params(("parallel",)),
        name="norm_proj",
    )(x2, g, w, hg)


GLA_TILE = 512
GLA_BATCH = 2


def _split3(x):
    hi = x.astype(BF16)
    r1 = x - hi.astype(F32)
    mid = r1.astype(BF16)
    lo = (r1 - mid.astype(F32)).astype(BF16)
    return hi, mid, lo


def _gla_kernel(q_ref, k_ref, v_ref, gate_ref, ar_ref, wa2_ref, ba_ref, gain_ref, o_ref, state_ref):
    C = GLA_CHUNK
    tq = q_ref.shape[1]
    n_chunks = tq // C
    shift = C.bit_length() - 1

    @pl.when(pl.program_id(1) == 0)
    def _():
        state_ref[...] = jnp.zeros_like(state_ref)

    row = lax.broadcasted_iota(jnp.int32, (tq, tq), 0)
    col = lax.broadcasted_iota(jnp.int32, (tq, tq), 1)
    same = (row >> shift) == (col >> shift)
    incl = same & (col <= row)
    t_incl = jnp.where(incl, 1.0, 0.0).astype(BF16)

    pairs = [(bb, h) for bb in range(q_ref.shape[0]) for h in range(GLA_HEADS)]
    ks = [slice(h * GLA_DK, (h + 1) * GLA_DK) for h in range(GLA_HEADS)]
    vs = [slice(h * GLA_DV, (h + 1) * GLA_DV) for h in range(GLA_HEADS)]

    q_d, k_d, k_tt, b_last_t = {}, {}, {}, {}
    for bb in range(q_ref.shape[0]):
        logits = jnp.dot(ar_ref[bb], wa2_ref[...], preferred_element_type=F32) + ba_ref[...]
        log_sig = jnp.minimum(logits, 0.0) - jnp.log(1.0 + jnp.exp(-jnp.abs(logits)))
        la = log_sig * (1.0 / GLA_TAU)
        b = sum(jnp.dot(t_incl, p, preferred_element_type=F32) for p in _split3(la))
        b_last = jnp.concatenate([jnp.broadcast_to(b[(c + 1) * C - 1:(c + 1) * C, :], (C, b.shape[1]))
                                  for c in range(n_chunks)], axis=0)
        q_d[bb] = (q_ref[bb].astype(F32) * (GLA_DK ** -0.5) * jnp.exp(b)).astype(BF16)
        k_f = k_ref[bb].astype(F32)
        k_d[bb] = (k_f * jnp.exp(-b)).astype(BF16)
        k_tt[bb] = (k_f * jnp.exp(b_last - b)).T
        b_last_t[bb] = b_last.T

    att = {(bb, h): lax.dot_general(q_d[bb][:, ks[h]], k_d[bb][:, ks[h]], (((1,), (1,)), ((), ())),
                                    preferred_element_type=F32) for bb, h in pairs}
    kv = {(bb, h): jnp.dot(jnp.where(same, jnp.tile(k_tt[bb][ks[h], :], (n_chunks, 1)), 0.0).astype(BF16),
                           v_ref[bb, :, vs[h]], preferred_element_type=F32) for bb, h in pairs}
    o_intra = {(bb, h): jnp.dot(jnp.where(incl, att[bb, h], 0.0).astype(BF16), v_ref[bb, :, vs[h]],
                                preferred_element_type=F32) for bb, h in pairs}
    for bb, h in pairs:
        st = state_ref[bb, h]
        inter = []
        for c in range(n_chunks):
            rows = slice(c * C, (c + 1) * C)
            inter.append(jnp.dot(q_d[bb][rows, ks[h]], st.astype(BF16), preferred_element_type=F32))
            dec = jnp.exp(b_last_t[bb][ks[h], c * C:c * C + 1])
            st = st * dec + kv[bb, h][c * GLA_DK:(c + 1) * GLA_DK, :]
        state_ref[bb, h] = st
        o_h = o_intra[bb, h] + jnp.concatenate(inter, axis=0)
        g = gate_ref[bb, :, vs[h]].astype(F32)
        o_ref[bb, :, vs[h]] = (_rms_rows(o_h, gain_ref[...]) * (g / (1.0 + jnp.exp(-g)))).astype(o_ref.dtype)


def _gla(z3, wa2p, ba, gain):
    b, s, _ = z3.shape
    tq = min(GLA_TILE, s)
    nb = GLA_BATCH if b % GLA_BATCH == 0 else 1

    def col(width, start):
        return pl.BlockSpec((nb, tq, width), lambda bi, i: (bi, i, start // width))

    return pl.pallas_call(
        _gla_kernel,
        grid=(b // nb, s // tq),
        in_specs=[col(A_QK, EV_AQ), col(A_QK, EV_AK), col(A_V, EV_AV), col(A_V, EV_AG),
                  col(LANES, EV_AR), _const_spec((LANES, A_QK)), _const_spec((1, A_QK)),
                  _const_spec((1, GLA_DV))],
        out_specs=pl.BlockSpec((nb, tq, A_V), lambda bi, i: (bi, i, 0)),
        out_shape=jax.ShapeDtypeStruct((b, s, A_V), BF16),
        scratch_shapes=[pltpu.VMEM((nb, GLA_HEADS, GLA_DK, GLA_DV), F32)],
        compiler_params=_params(("parallel", "arbitrary")),
        name="gla",
    )(z3, z3, z3, z3, z3, wa2p, ba, gain)


DIFF_TILE_Q = 2048
DIFF_TILE_K = 512
DIFF_ROWS_FULL = 512
DIFF_ROWS_DIAG = 256
DIFF_AHEAD = 2
LOG2E = math.log2(math.e)


def _diff_kernel(q_ref, k_ref, v_ref, lq1_ref, lk1_ref, lq2_ref, lk2_ref, gain_ref,
                 o_ref, qv_ref, m_ref, acc_ref, *, lambda_init, tk):
    qi = pl.program_id(2)
    tq = q_ref.shape[1]
    span = tq // tk

    q = q_ref[0]
    first = lax.broadcasted_iota(jnp.int32, q.shape, 1) < DIFF_DH
    zero = jnp.zeros_like(q)
    qv_ref[0:tq, :] = jnp.where(first, q, zero)
    qv_ref[tq:2 * tq, :] = jnp.where(first, zero, q)

    def process(blocks, rc, fresh=False):
        tasks = []
        for bi, (block, diag) in enumerate(blocks):
            rows = pl.ds(pl.multiple_of(block * tk, tk), tk)
            k = k_ref[0, rows, :]
            v1 = jnp.concatenate([v_ref[0, rows, :], jnp.ones((tk, LANES), BF16)], axis=1)
            for c in range(2 * tq // rc):
                kw = tk if diag is None else max(min((c * rc) % tq + rc - diag * tk, tk), 0)
                if kw > 0:
                    tasks.append((k, v1, c, kw, diag, fresh and bi == 0))

        def scores(task):
            k, _, c, kw, _, _ = task
            return lax.dot_general(qv_ref[c * rc:(c + 1) * rc, :], k[0:kw],
                                   (((1,), (1,)), ((), ())), preferred_element_type=F32)

        ready = [scores(t) for t in tasks[:DIFF_AHEAD]]
        for idx, (_, v1, c, kw, diag, start) in enumerate(tasks):
            rs = slice(c * rc, (c + 1) * rc)
            s = ready.pop(0)
            if idx + DIFF_AHEAD < len(tasks):
                ready.append(scores(tasks[idx + DIFF_AHEAD]))
            if diag is not None:
                r = lax.broadcasted_iota(jnp.int32, (rc, kw), 0) + ((c * rc) % tq - diag * tk)
                col = lax.broadcasted_iota(jnp.int32, (rc, kw), 1)
                s = jnp.where(col <= r, s, NEG_BIG)
            m_cur = jnp.max(s, axis=-1, keepdims=True)
            m_new = jnp.broadcast_to(m_cur, (rc, LANES)) if start else jnp.maximum(m_ref[rs, :], m_cur)
            p = jnp.exp2(s - jnp.tile(m_new, (1, kw // LANES)))
            pv = jnp.dot(p.astype(BF16), v1[0:kw], preferred_element_type=F32)
            if start:
                acc_ref[rs, :] = pv
            else:
                alpha = jnp.exp2(m_ref[rs, :] - m_new)
                acc_ref[rs, :] = acc_ref[rs, :] * jnp.tile(alpha, (1, 2)) + pv
            m_ref[rs, :] = m_new

    def below_diagonal(j, carry):
        process([(j * span + d, None) for d in range(span)], DIFF_ROWS_FULL)
        return carry

    process([(qi * span + d, d) for d in range(span)], DIFF_ROWS_DIAG, fresh=True)
    lax.fori_loop(0, qi, below_diagonal, 0)

    lam = (jnp.exp(jnp.sum(lq1_ref[...] * lk1_ref[...], axis=-1, keepdims=True))
           - jnp.exp(jnp.sum(lq2_ref[...] * lk2_ref[...], axis=-1, keepdims=True))
           + lambda_init)
    o1 = acc_ref[0:tq, 0:DIFF_DV] / acc_ref[0:tq, DIFF_DV:]
    o2 = acc_ref[tq:2 * tq, 0:DIFF_DV] / acc_ref[tq:2 * tq, DIFF_DV:]
    o = _rms_rows(o1 - lam * o2, gain_ref[...]) * (1.0 - lambda_init)
    o_ref[0] = o.astype(o_ref.dtype)


def _diff_attention(z3, lq1, lk1, lq2, lk2, gain, lambda_init):
    b, s, _ = z3.shape
    tq = min(DIFF_TILE_Q, s)
    tk = min(DIFF_TILE_K, s)
    qc, kc, vc = EV_BQ // LANES, EV_BK // LANES, EV_BV // LANES
    return pl.pallas_call(
        functools.partial(_diff_kernel, lambda_init=lambda_init, tk=tk),
        grid=(b, DIFF_HEADS, s // tq),
        in_specs=[pl.BlockSpec((1, tq, LANES), lambda bi, h, qi: (bi, qi, qc + h)),
                  pl.BlockSpec((1, s, LANES), lambda bi, h, qi: (bi, 0, kc + h)),
                  pl.BlockSpec((1, s, LANES), lambda bi, h, qi: (bi, 0, vc + h)),
                  _const_spec((1, DIFF_DH)), _const_spec((1, DIFF_DH)),
                  _const_spec((1, DIFF_DH)), _const_spec((1, DIFF_DH)),
                  _const_spec((1, DIFF_DV))],
        out_specs=pl.BlockSpec((1, tq, LANES), lambda bi, h, qi: (bi, qi, h)),
        out_shape=jax.ShapeDtypeStruct((b, s, B_V), BF16),
        scratch_shapes=[pltpu.VMEM((2 * tq, LANES), BF16), pltpu.VMEM((2 * tq, LANES), F32),
                        pltpu.VMEM((2 * tq, 2 * DIFF_DV), F32)],
        compiler_params=_params(("parallel", "parallel", "arbitrary")),
        name="diff_attention",
    )(z3, z3, z3, lq1, lk1, lq2, lk2, gain)


def _proj2_kernel(a_ref, b_ref, w_ref, o_ref):
    ka = a_ref.shape[1]
    o_ref[...] = (jnp.dot(a_ref[...], w_ref[0:ka, :], preferred_element_type=F32)
                  + jnp.dot(b_ref[...], w_ref[ka:, :], preferred_element_type=F32))


def _proj2(a, b, w):
    t, d = a.shape[0], w.shape[1]
    row = lambda width: pl.BlockSpec((ROW_TILE, width), lambda i: (i, 0))
    return pl.pallas_call(
        _proj2_kernel,
        grid=(t // ROW_TILE,),
        in_specs=[row(a.shape[1]), row(b.shape[1]), _const_spec(w.shape)],
        out_specs=row(d),
        out_shape=jax.ShapeDtypeStruct((t, d), F32),
        compiler_params=_params(("parallel",)),
        name="proj2",
    )(a, b, w)


DIL_WC = 128
DIL_STEP_ROWS = 512
LSE_LANES = LANES // DIL_HEADS
GROUP_COLS = 3 * ODD_MIX


def _odd_proj_kernel(x_ref, g_ref, w_ref, hg_ref, *refs):
    out_refs, h_ref = refs[:-1], refs[-1]
    tm = x_ref.shape[0]
    h = _rms_rows(x_ref[...], g_ref[...])
    n_lane_tiles = h_ref.shape[0]
    for c in range(n_lane_tiles):
        h_ref[c] = h[:, c * LANES:(c + 1) * LANES]
    for gi, o_ref in enumerate(out_refs):
        d = o_ref.shape[1]
        n = tm // d
        if d == 1:
            hb = h.astype(BF16)
        else:
            hb = jnp.concatenate(
                [jnp.concatenate([h_ref[c, pl.ds(r, n, stride=d), :] for r in range(d)], axis=0)
                 for c in range(n_lane_tiles)], axis=1).astype(BF16)
        for c0 in range(0, GROUP_COLS, MXU_COLS):
            zw = jnp.dot(hb, w_ref[:, gi * GROUP_COLS + c0:gi * GROUP_COLS + c0 + MXU_COLS],
                         preferred_element_type=F32)
            for sub in range(MXU_COLS // LANES):
                col = c0 + sub * LANES
                cs = slice(gi * GROUP_COLS + col, gi * GROUP_COLS + col + LANES)
                z = zw[:, sub * LANES:(sub + 1) * LANES]
                if col < 2 * ODD_MIX:
                    ss = jnp.sum(z * z, axis=-1, keepdims=True)
                    z = z * lax.rsqrt(ss * (1.0 / LANES) + EPS) * hg_ref[:, cs]
                z = z.astype(o_ref.dtype)
                for r in range(d):
                    o_ref[0, r, :, col:col + LANES] = z[r * n:(r + 1) * n]


def _odd_proj(x2, g, w, hg, batch, seq):
    t, dm = x2.shape
    tiles = seq // ROW_TILE
    out_specs, out_shapes = [], []
    for _, d in DIL_PATTERNS:
        out_specs.append(pl.BlockSpec((1, d, ROW_TILE // d, GROUP_COLS),
                                      lambda i: (i // tiles, 0, i % tiles, 0)))
        out_shapes.append(jax.ShapeDtypeStruct((batch, d, seq // d, GROUP_COLS), BF16))
    return pl.pallas_call(
        _odd_proj_kernel,
        grid=(t // ROW_TILE,),
        in_specs=[pl.BlockSpec((ROW_TILE, dm), lambda i: (i, 0)),
                  _const_spec((1, dm)), _const_spec(w.shape), _const_spec(hg.shape)],
        out_specs=out_specs,
        out_shape=out_shapes,
        scratch_shapes=[pltpu.VMEM((dm // LANES, ROW_TILE, LANES), F32)],
        compiler_params=_params(("parallel",)),
        name="odd_proj",
    )(x2, g, w, hg)


def _dil_kernel(q_ref, kp_ref, kc_ref, vp_ref, vc_ref, o_ref, lse_ref):
    n = pl.program_id(2)
    rows = q_ref.shape[2]
    wc = DIL_WC
    i = lax.broadcasted_iota(jnp.int32, (wc, 2 * wc), 0)
    j = lax.broadcasted_iota(jnp.int32, (wc, 2 * wc), 1)
    dist = i + wc - j
    band = (dist >= 0) & (dist <= wc)
    band_first = band & ((j >= wc) | (n > 0))
    ones = jnp.ones((2 * wc, DIL_DH), BF16)
    lane_head = lax.broadcasted_iota(jnp.int32, (wc, LANES), 1) // LSE_LANES
    tasks = [(ci, jb, h) for ci in range(q_ref.shape[1]) for jb in range(rows // wc) for h in range(DIL_HEADS)]

    def band_of(ref_prev, ref_cur, ci, jb, hs):
        if jb == 0:
            return jnp.concatenate([ref_prev[0, ci, :, hs], ref_cur[0, ci, 0:wc, hs]], axis=0)
        return ref_cur[0, ci, (jb - 1) * wc:(jb + 1) * wc, hs]

    def scores(task):
        ci, jb, h = task
        hs = slice(h * DIL_DH, (h + 1) * DIL_DH)
        return lax.dot_general(q_ref[0, ci, jb * wc:(jb + 1) * wc, hs], band_of(kp_ref, kc_ref, ci, jb, hs),
                               (((1,), (1,)), ((), ())), preferred_element_type=F32)

    s_next = scores(tasks[0])
    lse = None
    for idx, (ci, jb, h) in enumerate(tasks):
        rs = slice(jb * wc, (jb + 1) * wc)
        hs = slice(h * DIL_DH, (h + 1) * DIL_DH)
        s = s_next
        s_next = scores(tasks[idx + 1]) if idx + 1 < len(tasks) else None
        s = jnp.where(band_first if jb == 0 else band, s, NEG_BIG)
        m = jnp.max(s, axis=-1, keepdims=True)
        p = jnp.exp2(s - m)
        pv = jnp.dot(p.astype(BF16), jnp.concatenate([band_of(vp_ref, vc_ref, ci, jb, hs), ones], axis=1),
                     preferred_element_type=F32)
        l = pv[:, DIL_DH:]
        o_ref[0, ci, rs, hs] = (pv[:, 0:DIL_DH] / l).astype(o_ref.dtype)
        lse_h = m * (1.0 / LOG2E) + jnp.log(l)
        lse = lse_h if h == 0 else jnp.where(lane_head == h, lse_h, lse)
        if h == DIL_HEADS - 1:
            lse_ref[0, ci, rs, :] = lse


def _dilated_group(zg):
    b, d, length, _ = zg.shape
    rows = min(length, DIL_STEP_ROWS)
    per_row = rows // DIL_WC
    cps = max(1, min(d, DIL_STEP_ROWS // rows))

    def cur(which):
        return pl.BlockSpec((1, cps, rows, ODD_MIX), lambda bi, r, n: (bi, r, n, which))

    def prev(which):
        return pl.BlockSpec((1, cps, DIL_WC, ODD_MIX),
                            lambda bi, r, n: (bi, r, jnp.maximum(n * per_row - 1, 0), which))

    out_spec = pl.BlockSpec((1, cps, rows, ODD_MIX), lambda bi, r, n: (bi, r, n, 0))
    lse_spec = pl.BlockSpec((1, cps, rows, LANES), lambda bi, r, n: (bi, r, n, 0))
    return pl.pallas_call(
        _dil_kernel,
        grid=(b, d // cps, length // rows),
        in_specs=[cur(0), prev(1), cur(1), prev(2), cur(2)],
        out_specs=[out_spec, lse_spec],
        out_shape=[jax.ShapeDtypeStruct((b, d, length, ODD_MIX), BF16),
                   jax.ShapeDtypeStruct((b, d, length, LANES), F32)],
        compiler_params=_params(("parallel", "parallel", "arbitrary")),
        name=f"dilated_d{d}",
    )(zg, zg, zg, zg, zg)


def _combine_proj_kernel(*refs):
    n_groups = len(DIL_PATTERNS)
    o_refs, l_refs = refs[:n_groups], refs[n_groups:2 * n_groups]
    w_ref, out_ref, stage_ref = refs[2 * n_groups:]
    tm = out_ref.shape[0]

    def natural(ref, slot):
        d = ref.shape[1]
        if d == 1:
            return ref[0, 0].astype(F32)
        n = tm // d
        n_lane_tiles = ref.shape[3] // LANES
        for r in range(d):
            blk = ref[0, r].astype(F32)
            for c in range(n_lane_tiles):
                stage_ref[slot, c, pl.ds(r, n, stride=d), :] = blk[:, c * LANES:(c + 1) * LANES]
        return jnp.concatenate([stage_ref[slot, c] for c in range(n_lane_tiles)], axis=1)

    outs = [natural(r, 2 * gi) for gi, r in enumerate(o_refs)]
    lses = [natural(r, 2 * gi + 1) for gi, r in enumerate(l_refs)]
    m = functools.reduce(jnp.maximum, lses)
    es = [jnp.exp(l - m) for l in lses]
    inv = 1.0 / sum(es)
    sel_row = lax.broadcasted_iota(jnp.int32, (LANES, ODD_MIX), 0)
    sel_col = lax.broadcasted_iota(jnp.int32, (LANES, ODD_MIX), 1)
    select = jnp.where(sel_row == (sel_col // DIL_DH) * LSE_LANES, 1.0, 0.0).astype(BF16)
    select3 = jnp.concatenate([select] * 3, axis=0)

    def spread(w):
        return jnp.dot(jnp.concatenate(_split3(w), axis=1), select3, preferred_element_type=F32)

    mix = outs[-1] + sum(spread(e * inv) * (o - outs[-1]) for e, o in zip(es[:-1], outs[:-1]))
    out_ref[...] = jnp.dot(mix.astype(BF16), w_ref[...], preferred_element_type=F32)


def _combine_proj(os_, lses, w, seq):
    t, dm = os_[0].shape[0] * seq, w.shape[1]
    tiles = seq // ROW_TILE
    row = pl.BlockSpec((ROW_TILE, dm), lambda i: (i, 0))
    cls = [pl.BlockSpec((1, a.shape[1], ROW_TILE // a.shape[1], a.shape[3]),
                        lambda i: (i // tiles, 0, i % tiles, 0)) for a in list(os_) + list(lses)]
    return pl.pallas_call(
        _combine_proj_kernel,
        grid=(t // ROW_TILE,),
        in_specs=cls + [_const_spec(w.shape)],
        out_specs=row,
        out_shape=jax.ShapeDtypeStruct((t, dm), F32),
        scratch_shapes=[pltpu.VMEM((2 * len(os_), ODD_MIX // LANES, ROW_TILE, LANES), F32)],
        compiler_params=_params(("parallel",)),
        name="combine_proj",
    )(*os_, *lses, w)


FFN_CHUNK = 256
FFN_TILE = 512
FFN_AHEAD = 1
FFN_DOWN_GROUP = 6
SUBLANES = 8
CARRY_ROWS = 2 * SUBLANES
STAGE_PAD_ROWS = 8


def _ffn_kernel(x_ref, upd_ref, g_ref, wup_ref, cw_ref, wdn_ref, o_ref, carry_ref, hs_ref, acc_ref,
                *, tiles_per_seq):
    tm, dm = x_ref.shape
    ng = tm // SUBLANES
    n_chunks = D_FF // FFN_CHUNK
    n_lane_tiles = dm // LANES

    h = _rms_rows(x_ref[...] + upd_ref[...], g_ref[...])
    pitch = hs_ref.shape[1] // SUBLANES
    for c in range(n_lane_tiles):
        for s in range(SUBLANES):
            hs_ref[c, s * pitch:s * pitch + ng, :] = h[s * ng:(s + 1) * ng, c * LANES:(c + 1) * LANES]
    hb = jnp.concatenate(
        [jnp.concatenate([hs_ref[c, pl.ds(j, SUBLANES, stride=pitch), :] for j in range(ng)], axis=0)
         for c in range(n_lane_tiles)], axis=1).astype(BF16)

    @pl.when(pl.program_id(0) % tiles_per_seq == 0)
    def _():
        carry_ref[...] = jnp.zeros_like(carry_ref)

    last_sublane = lax.broadcasted_iota(jnp.int32, (SUBLANES, FFN_CHUNK), 0) == SUBLANES - 1

    def cols_of(c):
        return (slice(c * FFN_CHUNK, (c + 1) * FFN_CHUNK),
                slice(D_FF + c * FFN_CHUNK, D_FF + (c + 1) * FFN_CHUNK))

    def up(c):
        return [jnp.dot(hb, wup_ref[:, cols], preferred_element_type=F32) for cols in cols_of(c)]

    def wrap(cur, prev):
        return pltpu.roll(jnp.where(last_sublane, prev, cur), 1, 0)

    def conv(u, cols):
        prev = carry_ref[:, cols]
        w2 = wrap(u[tm - 2 * SUBLANES:tm - SUBLANES], prev[0:SUBLANES])
        w1 = wrap(u[tm - SUBLANES:], prev[SUBLANES:])
        u1 = jnp.concatenate([w1, u[:tm - SUBLANES]], axis=0)
        u2 = jnp.concatenate([w2, w1, u[:tm - 2 * SUBLANES]], axis=0)
        carry_ref[:, cols] = u[tm - CARRY_ROWS:]
        cw = cw_ref[:, cols]
        return u2 * cw[0:1] + u1 * cw[1:2] + u * cw[2:3] + cw[3:4]

    ready = [up(c) for c in range(FFN_AHEAD)]
    pending = []
    for c in range(n_chunks):
        u_cur = ready.pop(0)
        if c + FFN_AHEAD < n_chunks:
            ready.append(up(c + FFN_AHEAD))
        gate, val = [conv(u, cols) for u, cols in zip(u_cur, cols_of(c))]
        pending.append((gate / (1.0 + jnp.exp(-gate)) * val).astype(BF16))
        if len(pending) == FFN_DOWN_GROUP or c + 1 == n_chunks:
            first = c + 1 - len(pending)
            act = pending[0] if len(pending) == 1 else jnp.concatenate(pending, axis=1)
            d = jnp.dot(act, wdn_ref[first * FFN_CHUNK:(c + 1) * FFN_CHUNK, :], preferred_element_type=F32)
            for lt in range(n_lane_tiles):
                piece = d[:, lt * LANES:(lt + 1) * LANES]
                acc_ref[lt] = piece if first == 0 else acc_ref[lt] + piece
            pending = []

    for lt in range(n_lane_tiles):
        cs = slice(lt * LANES, (lt + 1) * LANES)
        for s in range(SUBLANES):
            rs = slice(s * ng, (s + 1) * ng)
            o_ref[rs, cs] = x_ref[rs, cs] + upd_ref[rs, cs] + acc_ref[lt, pl.ds(s, ng, stride=SUBLANES), :]


def _ffn(x2, update, g, w_up, conv_tab, w_down, seq_len):
    t, d = x2.shape
    tile = min(FFN_TILE, seq_len)
    row = pl.BlockSpec((tile, d), lambda i: (i, 0))
    return pl.pallas_call(
        functools.partial(_ffn_kernel, tiles_per_seq=seq_len // tile),
        grid=(t // tile,),
        in_specs=[row, row, _const_spec((1, d)), _const_spec(w_up.shape), _const_spec(conv_tab.shape),
                  _const_spec(w_down.shape)],
        out_specs=row,
        out_shape=jax.ShapeDtypeStruct((t, d), F32),
        scratch_shapes=[pltpu.VMEM((CARRY_ROWS, 2 * D_FF), F32),
                        pltpu.VMEM((d // LANES, tile + STAGE_PAD_ROWS * SUBLANES, LANES), F32),
                        pltpu.VMEM((d // LANES, tile, LANES), F32)],
        compiler_params=_params(("arbitrary",)),
        name="conv_ffn",
    )(x2, update, g, w_up, conv_tab, w_down)


def _even_weights(w_in, w_a2, b_a, dq_gain, dk_gain):
    sizes = [A_QK, A_QK, A_V, A_V, GLA_RANK, B_QK, B_QK, B_V]
    offs = [0]
    for c in sizes:
        offs.append(offs[-1] + c)
    aq, ak, av, ag, ar, bq, bk, bv = [w_in[:, offs[i]:offs[i + 1]] for i in range(8)]
    pad = jnp.zeros((w_in.shape[0], LANES - GLA_RANK), w_in.dtype)
    w = jnp.concatenate([aq, ak, av, ag, bq, bk, bv, ar, pad], axis=1).astype(BF16)
    hg = jnp.ones((EV_COLS,), F32)
    hg = hg.at[EV_BQ:EV_BK].set(jnp.tile(dq_gain.astype(F32), B_QK // DIFF_DH) * (DIFF_DH ** -0.5 * LOG2E))
    hg = hg.at[EV_BK:EV_BV].set(jnp.tile(dk_gain.astype(F32), B_QK // DIFF_DH))
    norm_group = tuple(DIFF_DH if EV_BQ <= c * LANES < EV_BV else 0 for c in range(EV_COLS // LANES))
    wa2p = jnp.concatenate([w_a2, jnp.zeros((LANES - GLA_RANK, A_QK), w_a2.dtype)], axis=0).astype(BF16)
    return w, hg.reshape(1, EV_COLS), norm_group, wa2p, b_a.astype(F32).reshape(1, A_QK)


def _odd_gains(q_gain, k_gain):
    hg = jnp.ones((3, 3, ODD_MIX), F32)
    hg = hg.at[:, 0].set(jnp.tile(q_gain.astype(F32), DIL_HEADS) * (DIL_DH ** -0.5 * LOG2E))
    hg = hg.at[:, 1].set(jnp.tile(k_gain.astype(F32), DIL_HEADS))
    return hg.reshape(1, ODD_IN)


def kernel(x, norm_mix, norm_ffn, ev_w_in, ev_w_a2, ev_b_a, ev_gla_gain, ev_dq_gain, ev_dk_gain,
           ev_lq1, ev_lk1, ev_lq2, ev_lk2, ev_diff_gain, ev_w_out, od_w_in, od_q_gain, od_k_gain,
           od_w_out, ffn_w_up, ffn_conv_w, ffn_conv_b, ffn_w_down):
    b, s, d = x.shape
    depth = norm_mix.shape[0]
    x2 = x.reshape(b * s, d).astype(F32)
    row = lambda v: v.astype(F32).reshape(1, -1)
    for i in range(depth):
        g_mix = row(norm_mix[i])
        if i % 2 == 0:
            e = i // 2
            lambda_init = 0.8 - 0.6 * math.exp(-0.3 * i)
            w, hg, norm_group, wa2p, ba = _even_weights(ev_w_in[e], ev_w_a2[e], ev_b_a[e],
                                                        ev_dq_gain[e], ev_dk_gain[e])
            z3 = _norm_proj(x2, g_mix, w, hg, norm_group).reshape(b, s, EV_COLS)
            oa = _gla(z3, wa2p, ba, row(ev_gla_gain[e]))
            ob = _diff_attention(z3, row(ev_lq1[e]), row(ev_lk1[e]), row(ev_lq2[e]), row(ev_lk2[e]),
                                 row(ev_diff_gain[e]), lambda_init)
            update = _proj2(oa.reshape(b * s, A_V), ob.reshape(b * s, B_V), ev_w_out[e].astype(BF16))
        else:
            o = i // 2
            zs = _odd_proj(x2, g_mix, od_w_in[o].astype(BF16), _odd_gains(od_q_gain[o], od_k_gain[o]), b, s)
            outs = [_dilated_group(zg) for zg in zs]
            update = _combine_proj([o_ for o_, _ in outs], [l_ for _, l_ in outs], od_w_out[o].astype(BF16), s)
        conv_tab = jnp.concatenate([ffn_conv_w[i].reshape(3, 2 * D_FF), ffn_conv_b[i].reshape(1, 2 * D_FF),
                                    jnp.zeros((4, 2 * D_FF), F32)], axis=0).astype(F32)
        x2 = _ffn(x2, update, row(norm_ffn[i]), ffn_w_up[i].astype(BF16), conv_tab,
                  ffn_w_down[i].astype(BF16), s)
    return x2.reshape(b, s, d).astype(x.dtype)
```

```python
import functools
import math

import jax
import jax.numpy as jnp
from jax import lax
from jax.experimental import pallas as pl
from jax.experimental.pallas import tpu as pltpu

F32 = jnp.float32
BF16 = jnp.bfloat16

EPS = 1e-6
LANES = 128
MXU_COLS = 256
VMEM_LIMIT = 56 * 1024 * 1024
NEG_BIG = -1e30

D_MODEL = 1024
GLA_HEADS, GLA_DK, GLA_DV, GLA_RANK, GLA_TAU, GLA_CHUNK = 4, 64, 128, 16, 16.0, 64
DIFF_HEADS, DIFF_DH, DIFF_DV = 4, 64, 128
A_QK, A_V, B_QK, B_V = 256, 512, 512, 512
DIL_PATTERNS = ((128, 1), (512, 4), (2048, 16))
DIL_HEADS, DIL_DH = 4, 128
ODD_MIX = DIL_HEADS * DIL_DH
ODD_IN = 3 * 3 * ODD_MIX
D_FF = 2816

EV_AQ, EV_AK, EV_AV, EV_AG, EV_BQ, EV_BK, EV_BV, EV_AR = 0, 256, 512, 1024, 1536, 2048, 2560, 3072
EV_COLS = 3200

ROW_TILE = 1024


def _const_spec(shape):
    nd = len(shape)
    return pl.BlockSpec(shape, lambda *_: (0,) * nd)


def _params(sem):
    return pltpu.CompilerParams(dimension_semantics=sem, vmem_limit_bytes=VMEM_LIMIT)


def _rms_rows(x, g):
    return x * lax.rsqrt(jnp.mean(x * x, axis=-1, keepdims=True) + EPS) * g


def _norm_proj_kernel(x_ref, g_ref, w_ref, hg_ref, o_ref, *, norm_group):
    hb = _rms_rows(x_ref[...], g_ref[...]).astype(BF16)
    n_cols = o_ref.shape[1]
    for c0 in range(0, n_cols, MXU_COLS):
        width = min(MXU_COLS, n_cols - c0)
        zw = jnp.dot(hb, w_ref[:, c0:c0 + width], preferred_element_type=F32)
        for sub in range(width // LANES):
            cs = slice(c0 + sub * LANES, c0 + (sub + 1) * LANES)
            z = zw[:, sub * LANES:(sub + 1) * LANES]
            grp = norm_group[cs.start // LANES]
            if grp == LANES:
                ss = jnp.sum(z * z, axis=-1, keepdims=True)
                z = z * lax.rsqrt(ss * (1.0 / LANES) + EPS) * hg_ref[:, cs]
            elif grp == LANES // 2:
                lo = lax.broadcasted_iota(jnp.int32, z.shape, 1) < grp
                sq = z * z
                s_lo = jnp.sum(jnp.where(lo, sq, 0.0), axis=-1, keepdims=True)
                s_hi = jnp.sum(jnp.where(lo, 0.0, sq), axis=-1, keepdims=True)
                ss = jnp.where(lo, s_lo, s_hi)
                z = z * lax.rsqrt(ss * (1.0 / grp) + EPS) * hg_ref[:, cs]
            o_ref[:, cs] = z.astype(o_ref.dtype)


def _norm_proj(x2, g, w, hg, norm_group):
    t, d = x2.shape
    n = w.shape[1]
    return pl.pallas_call(
        functools.partial(_norm_proj_kernel, norm_group=norm_group),
        grid=(t // ROW_TILE,),
        in_specs=[pl.BlockSpec((ROW_TILE, d), lambda i: (i, 0)),
                  _const_spec((1, d)), _const_spec((d, n)), _const_spec((1, n))],
        out_specs=pl.BlockSpec((ROW_TILE, n), lambda i: (i, 0)),
        out_shape=jax.ShapeDtypeStruct((t, n), BF16),
        compiler_params=_params(("parallel",)),
        name="norm_proj",
    )(x2, g, w, hg)


GLA_TILE = 512
GLA_BATCH = 2


def _split3(x):
    hi = x.astype(BF16)
    r1 = x - hi.astype(F32)
    mid = r1.astype(BF16)
    lo = (r1 - mid.astype(F32)).astype(BF16)
    return hi, mid, lo


def _gla_kernel(q_ref, k_ref, v_ref, gate_ref, ar_ref, wa2_ref, ba_ref, gain_ref, o_ref, state_ref):
    C = GLA_CHUNK
    tq = q_ref.shape[1]
    n_chunks = tq // C
    shift = C.bit_length() - 1

    @pl.when(pl.program_id(1) == 0)
    def _():
        state_ref[...] = jnp.zeros_like(state_ref)

    row = lax.broadcasted_iota(jnp.int32, (tq, tq), 0)
    col = lax.broadcasted_iota(jnp.int32, (tq, tq), 1)
    same = (row >> shift) == (col >> shift)
    incl = same & (col <= row)
    t_incl = jnp.where(incl, 1.0, 0.0).astype(BF16)

    pairs = [(bb, h) for bb in range(q_ref.shape[0]) for h in range(GLA_HEADS)]
    ks = [slice(h * GLA_DK, (h + 1) * GLA_DK) for h in range(GLA_HEADS)]
    vs = [slice(h * GLA_DV, (h + 1) * GLA_DV) for h in range(GLA_HEADS)]

    q_d, k_d, k_tt, b_last_t = {}, {}, {}, {}
    for bb in range(q_ref.shape[0]):
        logits = jnp.dot(ar_ref[bb], wa2_ref[...], preferred_element_type=F32) + ba_ref[...]
        log_sig = jnp.minimum(logits, 0.0) - jnp.log(1.0 + jnp.exp(-jnp.abs(logits)))
        la = log_sig * (1.0 / GLA_TAU)
        b = sum(jnp.dot(t_incl, p, preferred_element_type=F32) for p in _split3(la))
        b_last = jnp.concatenate([jnp.broadcast_to(b[(c + 1) * C - 1:(c + 1) * C, :], (C, b.shape[1]))
                                  for c in range(n_chunks)], axis=0)
        q_d[bb] = (q_ref[bb].astype(F32) * (GLA_DK ** -0.5) * jnp.exp(b)).astype(BF16)
        k_f = k_ref[bb].astype(F32)
        k_d[bb] = (k_f * jnp.exp(-b)).astype(BF16)
        k_tt[bb] = (k_f * jnp.exp(b_last - b)).T
        b_last_t[bb] = b_last.T

    att = {(bb, h): lax.dot_general(q_d[bb][:, ks[h]], k_d[bb][:, ks[h]], (((1,), (1,)), ((), ())),
                                    preferred_element_type=F32) for bb, h in pairs}
    kv = {(bb, h): jnp.dot(jnp.where(same, jnp.tile(k_tt[bb][ks[h], :], (n_chunks, 1)), 0.0).astype(BF16),
                           v_ref[bb, :, vs[h]], preferred_element_type=F32) for bb, h in pairs}
    o_intra = {(bb, h): jnp.dot(jnp.where(incl, att[bb, h], 0.0).astype(BF16), v_ref[bb, :, vs[h]],
                                preferred_element_type=F32) for bb, h in pairs}
    for bb, h in pairs:
        st = state_ref[bb, h]
        inter = []
        for c in range(n_chunks):
            rows = slice(c * C, (c + 1) * C)
            inter.append(jnp.dot(q_d[bb][rows, ks[h]], st.astype(BF16), preferred_element_type=F32))
            dec = jnp.exp(b_last_t[bb][ks[h], c * C:c * C + 1])
            st = st * dec + kv[bb, h][c * GLA_DK:(c + 1) * GLA_DK, :]
        state_ref[bb, h] = st
        o_h = o_intra[bb, h] + jnp.concatenate(inter, axis=0)
        g = gate_ref[bb, :, vs[h]].astype(F32)
        o_ref[bb, :, vs[h]] = (_rms_rows(o_h, gain_ref[...]) * (g / (1.0 + jnp.exp(-g)))).astype(o_ref.dtype)


def _gla(z3, wa2p, ba, gain):
    b, s, _ = z3.shape
    tq = min(GLA_TILE, s)
    nb = GLA_BATCH if b % GLA_BATCH == 0 else 1

    def col(width, start):
        return pl.BlockSpec((nb, tq, width), lambda bi, i: (bi, i, start // width))

    return pl.pallas_call(
        _gla_kernel,
        grid=(b // nb, s // tq),
        in_specs=[col(A_QK, EV_AQ), col(A_QK, EV_AK), col(A_V, EV_AV), col(A_V, EV_AG),
                  col(LANES, EV_AR), _const_spec((LANES, A_QK)), _const_spec((1, A_QK)),
                  _const_spec((1, GLA_DV))],
        out_specs=pl.BlockSpec((nb, tq, A_V), lambda bi, i: (bi, i, 0)),
        out_shape=jax.ShapeDtypeStruct((b, s, A_V), BF16),
        scratch_shapes=[pltpu.VMEM((nb, GLA_HEADS, GLA_DK, GLA_DV), F32)],
        compiler_params=_params(("parallel", "arbitrary")),
        name="gla",
    )(z3, z3, z3, z3, z3, wa2p, ba, gain)


DIFF_TILE_Q = 2048
DIFF_TILE_K = 512
DIFF_ROWS_FULL = 512
DIFF_ROWS_DIAG = 256
DIFF_AHEAD = 2
LOG2E = math.log2(math.e)


def _diff_kernel(q_ref, k_ref, v_ref, lq1_ref, lk1_ref, lq2_ref, lk2_ref, gain_ref,
                 o_ref, qv_ref, m_ref, acc_ref, *, lambda_init, tk):
    qi = pl.program_id(2)
    tq = q_ref.shape[1]
    span = tq // tk

    q = q_ref[0]
    first = lax.broadcasted_iota(jnp.int32, q.shape, 1) < DIFF_DH
    zero = jnp.zeros_like(q)
    qv_ref[0:tq, :] = jnp.where(first, q, zero)
    qv_ref[tq:2 * tq, :] = jnp.where(first, zero, q)

    def process(blocks, rc, fresh=False):
        tasks = []
        for bi, (block, diag) in enumerate(blocks):
            rows = pl.ds(pl.multiple_of(block * tk, tk), tk)
            k = k_ref[0, rows, :]
            v1 = jnp.concatenate([v_ref[0, rows, :], jnp.ones((tk, LANES), BF16)], axis=1)
            for c in range(2 * tq // rc):
                kw = tk if diag is None else max(min((c * rc) % tq + rc - diag * tk, tk), 0)
                if kw > 0:
                    tasks.append((k, v1, c, kw, diag, fresh and bi == 0))

        def scores(task):
            k, _, c, kw, _, _ = task
            return lax.dot_general(qv_ref[c * rc:(c + 1) * rc, :], k[0:kw],
                                   (((1,), (1,)), ((), ())), preferred_element_type=F32)

        ready = [scores(t) for t in tasks[:DIFF_AHEAD]]
        for idx, (_, v1, c, kw, diag, start) in enumerate(tasks):
            rs = slice(c * rc, (c + 1) * rc)
            s = ready.pop(0)
            if idx + DIFF_AHEAD < len(tasks):
                ready.append(scores(tasks[idx + DIFF_AHEAD]))
            if diag is not None:
                r = lax.broadcasted_iota(jnp.int32, (rc, kw), 0) + ((c * rc) % tq - diag * tk)
                col = lax.broadcasted_iota(jnp.int32, (rc, kw), 1)
                s = jnp.where(col <= r, s, NEG_BIG)
            m_cur = jnp.max(s, axis=-1, keepdims=True)
            m_new = jnp.broadcast_to(m_cur, (rc, LANES)) if start else jnp.maximum(m_ref[rs, :], m_cur)
            p = jnp.exp2(s - jnp.tile(m_new, (1, kw // LANES)))
            pv = jnp.dot(p.astype(BF16), v1[0:kw], preferred_element_type=F32)
            if start:
                acc_ref[rs, :] = pv
            else:
                alpha = jnp.exp2(m_ref[rs, :] - m_new)
                acc_ref[rs, :] = acc_ref[rs, :] * jnp.tile(alpha, (1, 2)) + pv
            m_ref[rs, :] = m_new

    def below_diagonal(j, carry):
        process([(j * span + d, None) for d in range(span)], DIFF_ROWS_FULL)
        return carry

    process([(qi * span + d, d) for d in range(span)], DIFF_ROWS_DIAG, fresh=True)
    lax.fori_loop(0, qi, below_diagonal, 0)

    lam = (jnp.exp(jnp.sum(lq1_ref[...] * lk1_ref[...], axis=-1, keepdims=True))
           - jnp.exp(jnp.sum(lq2_ref[...] * lk2_ref[...], axis=-1, keepdims=True))
           + lambda_init)
    o1 = acc_ref[0:tq, 0:DIFF_DV] / acc_ref[0:tq, DIFF_DV:]
    o2 = acc_ref[tq:2 * tq, 0:DIFF_DV] / acc_ref[tq:2 * tq, DIFF_DV:]
    o = _rms_rows(o1 - lam * o2, gain_ref[...]) * (1.0 - lambda_init)
    o_ref[0] = o.astype(o_ref.dtype)


def _diff_attention(z3, lq1, lk1, lq2, lk2, gain, lambda_init):
    b, s, _ = z3.shape
    tq = min(DIFF_TILE_Q, s)
    tk = min(DIFF_TILE_K, s)
    qc, kc, vc = EV_BQ // LANES, EV_BK // LANES, EV_BV // LANES
    return pl.pallas_call(
        functools.partial(_diff_kernel, lambda_init=lambda_init, tk=tk),
        grid=(b, DIFF_HEADS, s // tq),
        in_specs=[pl.BlockSpec((1, tq, LANES), lambda bi, h, qi: (bi, qi, qc + h)),
                  pl.BlockSpec((1, s, LANES), lambda bi, h, qi: (bi, 0, kc + h)),
                  pl.BlockSpec((1, s, LANES), lambda bi, h, qi: (bi, 0, vc + h)),
                  _const_spec((1, DIFF_DH)), _const_spec((1, DIFF_DH)),
                  _const_spec((1, DIFF_DH)), _const_spec((1, DIFF_DH)),
                  _const_spec((1, DIFF_DV))],
        out_specs=pl.BlockSpec((1, tq, LANES), lambda bi, h, qi: (bi, qi, h)),
        out_shape=jax.ShapeDtypeStruct((b, s, B_V), BF16),
        scratch_shapes=[pltpu.VMEM((2 * tq, LANES), BF16), pltpu.VMEM((2 * tq, LANES), F32),
                        pltpu.VMEM((2 * tq, 2 * DIFF_DV), F32)],
        compiler_params=_params(("parallel", "parallel", "arbitrary")),
        name="diff_attention",
    )(z3, z3, z3, lq1, lk1, lq2, lk2, gain)


def _proj2_kernel(a_ref, b_ref, w_ref, o_ref):
    ka = a_ref.shape[1]
    o_ref[...] = (jnp.dot(a_ref[...], w_ref[0:ka, :], preferred_element_type=F32)
                  + jnp.dot(b_ref[...], w_ref[ka:, :], preferred_element_type=F32))


def _proj2(a, b, w):
    t, d = a.shape[0], w.shape[1]
    row = lambda width: pl.BlockSpec((ROW_TILE, width), lambda i: (i, 0))
    return pl.pallas_call(
        _proj2_kernel,
        grid=(t // ROW_TILE,),
        in_specs=[row(a.shape[1]), row(b.shape[1]), _const_spec(w.shape)],
        out_specs=row(d),
        out_shape=jax.ShapeDtypeStruct((t, d), F32),
        compiler_params=_params(("parallel",)),
        name="proj2",
    )(a, b, w)


DIL_WC = 128
DIL_STEP_ROWS = 2048
LSE_LANES = LANES // DIL_HEADS
GROUP_COLS = 3 * ODD_MIX


def _odd_proj_kernel(x_ref, g_ref, w_ref, hg_ref, *refs):
    out_refs, h_ref = refs[:-1], refs[-1]
    tm = x_ref.shape[0]
    h = _rms_rows(x_ref[...], g_ref[...])
    n_lane_tiles = h_ref.shape[0]
    for c in range(n_lane_tiles):
        h_ref[c] = h[:, c * LANES:(c + 1) * LANES]
    for gi, o_ref in enumerate(out_refs):
        d = o_ref.shape[1]
        n = tm // d
        if d == 1:
            hb = h.astype(BF16)
        else:
            hb = jnp.concatenate(
                [jnp.concatenate([h_ref[c, pl.ds(r, n, stride=d), :] for r in range(d)], axis=0)
                 for c in range(n_lane_tiles)], axis=1).astype(BF16)
        for c0 in range(0, GROUP_COLS, MXU_COLS):
            zw = jnp.dot(hb, w_ref[:, gi * GROUP_COLS + c0:gi * GROUP_COLS + c0 + MXU_COLS],
                         preferred_element_type=F32)
            for sub in range(MXU_COLS // LANES):
                col = c0 + sub * LANES
                cs = slice(gi * GROUP_COLS + col, gi * GROUP_COLS + col + LANES)
                z = zw[:, sub * LANES:(sub + 1) * LANES]
                if col < 2 * ODD_MIX:
                    ss = jnp.sum(z * z, axis=-1, keepdims=True)
                    z = z * lax.rsqrt(ss * (1.0 / LANES) + EPS) * hg_ref[:, cs]
                z = z.astype(o_ref.dtype)
                for r in range(d):
                    o_ref[0, r, :, col:col + LANES] = z[r * n:(r + 1) * n]


def _odd_proj(x2, g, w, hg, batch, seq):
    t, dm = x2.shape
    tiles = seq // ROW_TILE
    out_specs, out_shapes = [], []
    for _, d in DIL_PATTERNS:
        out_specs.append(pl.BlockSpec((1, d, ROW_TILE // d, GROUP_COLS),
                                      lambda i: (i // tiles, 0, i % tiles, 0)))
        out_shapes.append(jax.ShapeDtypeStruct((batch, d, seq // d, GROUP_COLS), BF16))
    return pl.pallas_call(
        _odd_proj_kernel,
        grid=(t // ROW_TILE,),
        in_specs=[pl.BlockSpec((ROW_TILE, dm), lambda i: (i, 0)),
                  _const_spec((1, dm)), _const_spec(w.shape), _const_spec(hg.shape)],
        out_specs=out_specs,
        out_shape=out_shapes,
        scratch_shapes=[pltpu.VMEM((dm // LANES, ROW_TILE, LANES), F32)],
        compiler_params=_params(("parallel",)),
        name="odd_proj",
    )(x2, g, w, hg)


def _dil_kernel(q_ref, kp_ref, kc_ref, vp_ref, vc_ref, o_ref, lse_ref):
    n = pl.program_id(2)
    rows = q_ref.shape[2]
    wc = DIL_WC
    i = lax.broadcasted_iota(jnp.int32, (wc, 2 * wc), 0)
    j = lax.broadcasted_iota(jnp.int32, (wc, 2 * wc), 1)
    dist = i + wc - j
    band = (dist >= 0) & (dist <= wc)
    band_first = band & ((j >= wc) | (n > 0))
    ones = jnp.ones((2 * wc, DIL_DH), BF16)
    lane_head = lax.broadcasted_iota(jnp.int32, (wc, LANES), 1) // LSE_LANES
    tasks = [(ci, jb, h) for ci in range(q_ref.shape[1]) for jb in range(rows // wc) for h in range(DIL_HEADS)]

    def band_of(ref_prev, ref_cur, ci, jb, hs):
        if jb == 0:
            return jnp.concatenate([ref_prev[0, ci, :, hs], ref_cur[0, ci, 0:wc, hs]], axis=0)
        return ref_cur[0, ci, (jb - 1) * wc:(jb + 1) * wc, hs]

    def scores(task):
        ci, jb, h = task
        hs = slice(h * DIL_DH, (h + 1) * DIL_DH)
        return lax.dot_general(q_ref[0, ci, jb * wc:(jb + 1) * wc, hs], band_of(kp_ref, kc_ref, ci, jb, hs),
                               (((1,), (1,)), ((), ())), preferred_element_type=F32)

    s_next = scores(tasks[0])
    lse = None
    for idx, (ci, jb, h) in enumerate(tasks):
        rs = slice(jb * wc, (jb + 1) * wc)
        hs = slice(h * DIL_DH, (h + 1) * DIL_DH)
        s = s_next
        s_next = scores(tasks[idx + 1]) if idx + 1 < len(tasks) else None
        s = jnp.where(band_first if jb == 0 else band, s, NEG_BIG)
        m = jnp.max(s, axis=-1, keepdims=True)
        p = jnp.exp2(s - m)
        pv = jnp.dot(p.astype(BF16), jnp.concatenate([band_of(vp_ref, vc_ref, ci, jb, hs), ones], axis=1),
                     preferred_element_type=F32)
        l = pv[:, DIL_DH:]
        o_ref[0, ci, rs, hs] = (pv[:, 0:DIL_DH] / l).astype(o_ref.dtype)
        lse_h = m * (1.0 / LOG2E) + jnp.log(l)
        lse = lse_h if h == 0 else jnp.where(lane_head == h, lse_h, lse)
        if h == DIL_HEADS - 1:
            lse_ref[0, ci, rs, :] = lse


def _dilated_group(zg):
    b, d, length, _ = zg.shape
    rows = min(length, DIL_STEP_ROWS)
    per_row = rows // DIL_WC
    cps = max(1, min(d, DIL_STEP_ROWS // rows))

    def cur(which):
        return pl.BlockSpec((1, cps, rows, ODD_MIX), lambda bi, r, n: (bi, r, n, which))

    def prev(which):
        return pl.BlockSpec((1, cps, DIL_WC, ODD_MIX),
                            lambda bi, r, n: (bi, r, jnp.maximum(n * per_row - 1, 0), which))

    out_spec = pl.BlockSpec((1, cps, rows, ODD_MIX), lambda bi, r, n: (bi, r, n, 0))
    lse_spec = pl.BlockSpec((1, cps, rows, LANES), lambda bi, r, n: (bi, r, n, 0))
    return pl.pallas_call(
        _dil_kernel,
        grid=(b, d // cps, length // rows),
        in_specs=[cur(0), prev(1), cur(1), prev(2), cur(2)],
        out_specs=[out_spec, lse_spec],
        out_shape=[jax.ShapeDtypeStruct((b, d, length, ODD_MIX), BF16),
                   jax.ShapeDtypeStruct((b, d, length, LANES), F32)],
        compiler_params=_params(("parallel", "parallel", "arbitrary")),
        name=f"dilated_d{d}",
    )(zg, zg, zg, zg, zg)


def _combine_proj_kernel(*refs):
    n_groups = len(DIL_PATTERNS)
    o_refs, l_refs = refs[:n_groups], refs[n_groups:2 * n_groups]
    w_ref, out_ref, stage_ref = refs[2 * n_groups:]
    tm = out_ref.shape[0]

    def natural(ref, slot):
        d = ref.shape[1]
        if d == 1:
            return ref[0, 0].astype(F32)
        n = tm // d
        n_lane_tiles = ref.shape[3] // LANES
        for r in range(d):
            blk = ref[0, r].astype(F32)
            for c in range(n_lane_tiles):
                stage_ref[slot, c, pl.ds(r, n, stride=d), :] = blk[:, c * LANES:(c + 1) * LANES]
        return jnp.concatenate([stage_ref[slot, c] for c in range(n_lane_tiles)], axis=1)

    outs = [natural(r, 2 * gi) for gi, r in enumerate(o_refs)]
    lses = [natural(r, 2 * gi + 1) for gi, r in enumerate(l_refs)]
    m = functools.reduce(jnp.maximum, lses)
    es = [jnp.exp(l - m) for l in lses]
    inv = 1.0 / sum(es)
    sel_row = lax.broadcasted_iota(jnp.int32, (LANES, ODD_MIX), 0)
    sel_col = lax.broadcasted_iota(jnp.int32, (LANES, ODD_MIX), 1)
    select = jnp.where(sel_row == (sel_col // DIL_DH) * LSE_LANES, 1.0, 0.0).astype(BF16)
    select3 = jnp.concatenate([select] * 3, axis=0)

    def spread(w):
        return jnp.dot(jnp.concatenate(_split3(w), axis=1), select3, preferred_element_type=F32)

    mix = outs[-1] + sum(spread(e * inv) * (o - outs[-1]) for e, o in zip(es[:-1], outs[:-1]))
    out_ref[...] = jnp.dot(mix.astype(BF16), w_ref[...], preferred_element_type=F32)


def _combine_proj(os_, lses, w, seq):
    t, dm = os_[0].shape[0] * seq, w.shape[1]
    tiles = seq // ROW_TILE
    row = pl.BlockSpec((ROW_TILE, dm), lambda i: (i, 0))
    cls = [pl.BlockSpec((1, a.shape[1], ROW_TILE // a.shape[1], a.shape[3]),
                        lambda i: (i // tiles, 0, i % tiles, 0)) for a in list(os_) + list(lses)]
    return pl.pallas_call(
        _combine_proj_kernel,
        grid=(t // ROW_TILE,),
        in_specs=cls + [_const_spec(w.shape)],
        out_specs=row,
        out_shape=jax.ShapeDtypeStruct((t, dm), F32),
        scratch_shapes=[pltpu.VMEM((2 * len(os_), ODD_MIX // LANES, ROW_TILE, LANES), F32)],
        compiler_params=_params(("parallel",)),
        name="combine_proj",
    )(*os_, *lses, w)


FFN_CHUNK = 256
FFN_TILE = 512
FFN_AHEAD = 1
FFN_DOWN_GROUP = 6
SUBLANES = 8
CARRY_ROWS = 2 * SUBLANES
STAGE_PAD_ROWS = 8


def _ffn_kernel(x_ref, upd_ref, g_ref, wup_ref, cw_ref, wdn_ref, o_ref, carry_ref, hs_ref, acc_ref,
                *, tiles_per_seq):
    tm, dm = x_ref.shape
    ng = tm // SUBLANES
    n_chunks = D_FF // FFN_CHUNK
    n_lane_tiles = dm // LANES

    h = _rms_rows(x_ref[...] + upd_ref[...], g_ref[...])
    pitch = hs_ref.shape[1] // SUBLANES
    for c in range(n_lane_tiles):
        for s in range(SUBLANES):
            hs_ref[c, s * pitch:s * pitch + ng, :] = h[s * ng:(s + 1) * ng, c * LANES:(c + 1) * LANES]
    hb = jnp.concatenate(
        [jnp.concatenate([hs_ref[c, pl.ds(j, SUBLANES, stride=pitch), :] for j in range(ng)], axis=0)
         for c in range(n_lane_tiles)], axis=1).astype(BF16)

    @pl.when(pl.program_id(0) % tiles_per_seq == 0)
    def _():
        carry_ref[...] = jnp.zeros_like(carry_ref)

    last_sublane = lax.broadcasted_iota(jnp.int32, (SUBLANES, FFN_CHUNK), 0) == SUBLANES - 1

    def cols_of(c):
        return (slice(c * FFN_CHUNK, (c + 1) * FFN_CHUNK),
                slice(D_FF + c * FFN_CHUNK, D_FF + (c + 1) * FFN_CHUNK))

    def up(c):
        return [jnp.dot(hb, wup_ref[:, cols], preferred_element_type=F32) for cols in cols_of(c)]

    def wrap(cur, prev):
        return pltpu.roll(jnp.where(last_sublane, prev, cur), 1, 0)

    def conv(u, cols):
        prev = carry_ref[:, cols]
        w2 = wrap(u[tm - 2 * SUBLANES:tm - SUBLANES], prev[0:SUBLANES])
        w1 = wrap(u[tm - SUBLANES:], prev[SUBLANES:])
        u1 = jnp.concatenate([w1, u[:tm - SUBLANES]], axis=0)
        u2 = jnp.concatenate([w2, w1, u[:tm - 2 * SUBLANES]], axis=0)
        carry_ref[:, cols] = u[tm - CARRY_ROWS:]
        cw = cw_ref[:, cols]
        return u2 * cw[0:1] + u1 * cw[1:2] + u * cw[2:3] + cw[3:4]

    ready = [up(c) for c in range(FFN_AHEAD)]
    pending = []
    for c in range(n_chunks):
        u_cur = ready.pop(0)
        if c + FFN_AHEAD < n_chunks:
            ready.append(up(c + FFN_AHEAD))
        gate, val = [conv(u, cols) for u, cols in zip(u_cur, cols_of(c))]
        pending.append((gate / (1.0 + jnp.exp(-gate)) * val).astype(BF16))
        if len(pending) == FFN_DOWN_GROUP or c + 1 == n_chunks:
            first = c + 1 - len(pending)
            act = pending[0] if len(pending) == 1 else jnp.concatenate(pending, axis=1)
            d = jnp.dot(act, wdn_ref[first * FFN_CHUNK:(c + 1) * FFN_CHUNK, :], preferred_element_type=F32)
            for lt in range(n_lane_tiles):
                piece = d[:, lt * LANES:(lt + 1) * LANES]
                acc_ref[lt] = piece if first == 0 else acc_ref[lt] + piece
            pending = []

    for lt in range(n_lane_tiles):
        cs = slice(lt * LANES, (lt + 1) * LANES)
        for s in range(SUBLANES):
            rs = slice(s * ng, (s + 1) * ng)
            o_ref[rs, cs] = x_ref[rs, cs] + upd_ref[rs, cs] + acc_ref[lt, pl.ds(s, ng, stride=SUBLANES), :]


def _ffn(x2, update, g, w_up, conv_tab, w_down, seq_len):
    t, d = x2.shape
    tile = min(FFN_TILE, seq_len)
    row = pl.BlockSpec((tile, d), lambda i: (i, 0))
    return pl.pallas_call(
        functools.partial(_ffn_kernel, tiles_per_seq=seq_len // tile),
        grid=(t // tile,),
        in_specs=[row, row, _const_spec((1, d)), _const_spec(w_up.shape), _const_spec(conv_tab.shape),
                  _const_spec(w_down.shape)],
        out_specs=row,
        out_shape=jax.ShapeDtypeStruct((t, d), F32),
        scratch_shapes=[pltpu.VMEM((CARRY_ROWS, 2 * D_FF), F32),
                        pltpu.VMEM((d // LANES, tile + STAGE_PAD_ROWS * SUBLANES, LANES), F32),
                        pltpu.VMEM((d // LANES, tile, LANES), F32)],
        compiler_params=_params(("arbitrary",)),
        name="conv_ffn",
    )(x2, update, g, w_up, conv_tab, w_down)


def _even_weights(w_in, w_a2, b_a, dq_gain, dk_gain):
    sizes = [A_QK, A_QK, A_V, A_V, GLA_RANK, B_QK, B_QK, B_V]
    offs = [0]
    for c in sizes:
        offs.append(offs[-1] + c)
    aq, ak, av, ag, ar, bq, bk, bv = [w_in[:, offs[i]:offs[i + 1]] for i in range(8)]
    pad = jnp.zeros((w_in.shape[0], LANES - GLA_RANK), w_in.dtype)
    w = jnp.concatenate([aq, ak, av, ag, bq, bk, bv, ar, pad], axis=1).astype(BF16)
    hg = jnp.ones((EV_COLS,), F32)
    hg = hg.at[EV_BQ:EV_BK].set(jnp.tile(dq_gain.astype(F32), B_QK // DIFF_DH) * (DIFF_DH ** -0.5 * LOG2E))
    hg = hg.at[EV_BK:EV_BV].set(jnp.tile(dk_gain.astype(F32), B_QK // DIFF_DH))
    norm_group = tuple(DIFF_DH if EV_BQ <= c * LANES < EV_BV else 0 for c in range(EV_COLS // LANES))
    wa2p = jnp.concatenate([w_a2, jnp.zeros((LANES - GLA_RANK, A_QK), w_a2.dtype)], axis=0).astype(BF16)
    return w, hg.reshape(1, EV_COLS), norm_group, wa2p, b_a.astype(F32).reshape(1, A_QK)


def _odd_gains(q_gain, k_gain):
    hg = jnp.ones((3, 3, ODD_MIX), F32)
    hg = hg.at[:, 0].set(jnp.tile(q_gain.astype(F32), DIL_HEADS) * (DIL_DH ** -0.5 * LOG2E))
    hg = hg.at[:, 1].set(jnp.tile(k_gain.astype(F32), DIL_HEADS))
    return hg.reshape(1, ODD_IN)


def kernel(x, norm_mix, norm_ffn, ev_w_in, ev_w_a2, ev_b_a, ev_gla_gain, ev_dq_gain, ev_dk_gain,
           ev_lq1, ev_lk1, ev_lq2, ev_lk2, ev_diff_gain, ev_w_out, od_w_in, od_q_gain, od_k_gain,
           od_w_out, ffn_w_up, ffn_conv_w, ffn_conv_b, ffn_w_down):
    b, s, d = x.shape
    depth = norm_mix.shape[0]
    x2 = x.reshape(b * s, d).astype(F32)
    row = lambda v: v.astype(F32).reshape(1, -1)
    for i in range(depth):
        g_mix = row(norm_mix[i])
        if i % 2 == 0:
            e = i // 2
            lambda_init = 0.8 - 0.6 * math.exp(-0.3 * i)
            w, hg, norm_group, wa2p, ba = _even_weights(ev_w_in[e], ev_w_a2[e], ev_b_a[e],
                                                        ev_dq_gain[e], ev_dk_gain[e])
            z3 = _norm_proj(x2, g_mix, w, hg, norm_group).reshape(b, s, EV_COLS)
            oa = _gla(z3, wa2p, ba, row(ev_gla_gain[e]))
            ob = _diff_attention(z3, row(ev_lq1[e]), row(ev_lk1[e]), row(ev_lq2[e]), row(ev_lk2[e]),
                                 row(ev_diff_gain[e]), lambda_init)
            update = _proj2(oa.reshape(b * s, A_V), ob.reshape(b * s, B_V), ev_w_out[e].astype(BF16))
        else:
            o = i // 2
            zs = _odd_proj(x2, g_mix, od_w_in[o].astype(BF16), _odd_gains(od_q_gain[o], od_k_gain[o]), b, s)
            outs = [_dilated_group(zg) for zg in zs]
            update = _combine_proj([o_ for o_, _ in outs], [l_ for _, l_ in outs], od_w_out[o].astype(BF16), s)
        conv_tab = jnp.concatenate([ffn_conv_w[i].reshape(3, 2 * D_FF), ffn_conv_b[i].reshape(1, 2 * D_FF),
                                    jnp.zeros((4, 2 * D_FF), F32)], axis=0).astype(F32)
        x2 = _ffn(x2, update, row(norm_ffn[i]), ffn_w_up[i].astype(BF16), conv_tab,
                  ffn_w_down[i].astype(BF16), s)
    return x2.reshape(b, s, d).astype(x.dtype)
```

```python
import functools
import math

import jax
import jax.numpy as jnp
from jax import lax
from jax.experimental import pallas as pl
from jax.experimental.pallas import tpu as pltpu

F32 = jnp.float32
BF16 = jnp.bfloat16

EPS = 1e-6
LANES = 128
MXU_COLS = 256
VMEM_LIMIT = 56 * 1024 * 1024
NEG_BIG = -1e30

D_MODEL = 1024
GLA_HEADS, GLA_DK, GLA_DV, GLA_RANK, GLA_TAU, GLA_CHUNK = 4, 64, 128, 16, 16.0, 64
DIFF_HEADS, DIFF_DH, DIFF_DV = 4, 64, 128
A_QK, A_V, B_QK, B_V = 256, 512, 512, 512
DIL_PATTERNS = ((128, 1), (512, 4), (2048, 16))
DIL_HEADS, DIL_DH = 4, 128
ODD_MIX = DIL_HEADS * DIL_DH
ODD_IN = 3 * 3 * ODD_MIX
D_FF = 2816

EV_AQ, EV_AK, EV_AV, EV_AG, EV_BQ, EV_BK, EV_BV, EV_AR = 0, 256, 512, 1024, 1536, 2048, 2560, 3072
EV_COLS = 3200

ROW_TILE = 1024


def _const_spec(shape):
    nd = len(shape)
    return pl.BlockSpec(shape, lambda *_: (0,) * nd)


def _params(sem):
    return pltpu.CompilerParams(dimension_semantics=sem, vmem_limit_bytes=VMEM_LIMIT)


def _rms_rows(x, g):
    return x * lax.rsqrt(jnp.mean(x * x, axis=-1, keepdims=True) + EPS) * g


def _norm_proj_kernel(x_ref, g_ref, w_ref, hg_ref, o_ref, *, norm_group):
    hb = _rms_rows(x_ref[...], g_ref[...]).astype(BF16)
    n_cols = o_ref.shape[1]
    for c0 in range(0, n_cols, MXU_COLS):
        width = min(MXU_COLS, n_cols - c0)
        zw = jnp.dot(hb, w_ref[:, c0:c0 + width], preferred_element_type=F32)
        for sub in range(width // LANES):
            cs = slice(c0 + sub * LANES, c0 + (sub + 1) * LANES)
            z = zw[:, sub * LANES:(sub + 1) * LANES]
            grp = norm_group[cs.start // LANES]
            if grp == LANES:
                ss = jnp.sum(z * z, axis=-1, keepdims=True)
                z = z * lax.rsqrt(ss * (1.0 / LANES) + EPS) * hg_ref[:, cs]
            elif grp == LANES // 2:
                lo = lax.broadcasted_iota(jnp.int32, z.shape, 1) < grp
                sq = z * z
                s_lo = jnp.sum(jnp.where(lo, sq, 0.0), axis=-1, keepdims=True)
                s_hi = jnp.sum(jnp.where(lo, 0.0, sq), axis=-1, keepdims=True)
                ss = jnp.where(lo, s_lo, s_hi)
                z = z * lax.rsqrt(ss * (1.0 / grp) + EPS) * hg_ref[:, cs]
            o_ref[:, cs] = z.astype(o_ref.dtype)


def _norm_proj(x2, g, w, hg, norm_group):
    t, d = x2.shape
    n = w.shape[1]
    return pl.pallas_call(
        functools.partial(_norm_proj_kernel, norm_group=norm_group),
        grid=(t // ROW_TILE,),
        in_specs=[pl.BlockSpec((ROW_TILE, d), lambda i: (i, 0)),
                  _const_spec((1, d)), _const_spec((d, n)), _const_spec((1, n))],
        out_specs=pl.BlockSpec((ROW_TILE, n), lambda i: (i, 0)),
        out_shape=jax.ShapeDtypeStruct((t, n), BF16),
        compiler_params=_params(("parallel",)),
        name="norm_proj",
    )(x2, g, w, hg)


GLA_TILE = 512
GLA_SUB = 256
GLA_BATCH = 2


def _split3(x):
    hi = x.astype(BF16)
    r1 = x - hi.astype(F32)
    mid = r1.astype(BF16)
    lo = (r1 - mid.astype(F32)).astype(BF16)
    return hi, mid, lo


def _gla_kernel(q_ref, k_ref, v_ref, gate_ref, ar_ref, wa2_ref, ba_ref, gain_ref, o_ref, state_ref):
    C = GLA_CHUNK
    tq = q_ref.shape[1]
    sub = min(GLA_SUB, tq)
    n_chunks, n_subs, per_sub = tq // C, tq // sub, sub // C
    shift = C.bit_length() - 1

    @pl.when(pl.program_id(1) == 0)
    def _():
        state_ref[...] = jnp.zeros_like(state_ref)

    row = lax.broadcasted_iota(jnp.int32, (sub, sub), 0)
    col = lax.broadcasted_iota(jnp.int32, (sub, sub), 1)
    same = (row >> shift) == (col >> shift)
    incl = same & (col <= row)
    t_incl = jnp.where(incl, 1.0, 0.0).astype(BF16)
    subs = [slice(sb * sub, (sb + 1) * sub) for sb in range(n_subs)]

    pairs = [(bb, h) for bb in range(q_ref.shape[0]) for h in range(GLA_HEADS)]
    triples = [(bb, h, sb) for bb, h in pairs for sb in range(n_subs)]
    ks = [slice(h * GLA_DK, (h + 1) * GLA_DK) for h in range(GLA_HEADS)]
    vs = [slice(h * GLA_DV, (h + 1) * GLA_DV) for h in range(GLA_HEADS)]

    q_d, k_d, k_tt, b_last_t = {}, {}, {}, {}
    for bb in range(q_ref.shape[0]):
        logits = jnp.dot(ar_ref[bb], wa2_ref[...], preferred_element_type=F32) + ba_ref[...]
        log_sig = jnp.minimum(logits, 0.0) - jnp.log(1.0 + jnp.exp(-jnp.abs(logits)))
        la = log_sig * (1.0 / GLA_TAU)
        parts = _split3(la)
        b = jnp.concatenate([sum(jnp.dot(t_incl, p[rs], preferred_element_type=F32) for p in parts)
                             for rs in subs], axis=0)
        b_last = jnp.concatenate([jnp.broadcast_to(b[(c + 1) * C - 1:(c + 1) * C, :], (C, b.shape[1]))
                                  for c in range(n_chunks)], axis=0)
        q_d[bb] = (q_ref[bb].astype(F32) * (GLA_DK ** -0.5) * jnp.exp(b)).astype(BF16)
        k_f = k_ref[bb].astype(F32)
        k_d[bb] = (k_f * jnp.exp(-b)).astype(BF16)
        k_tt[bb] = (k_f * jnp.exp(b_last - b)).T
        b_last_t[bb] = b_last.T

    att = {(bb, h, sb): lax.dot_general(q_d[bb][subs[sb], ks[h]], k_d[bb][subs[sb], ks[h]],
                                        (((1,), (1,)), ((), ())), preferred_element_type=F32)
           for bb, h, sb in triples}
    kv = {(bb, h, sb): jnp.dot(jnp.where(same, jnp.tile(k_tt[bb][ks[h], subs[sb]], (per_sub, 1)), 0.0).astype(BF16),
                               v_ref[bb, subs[sb], vs[h]], preferred_element_type=F32) for bb, h, sb in triples}
    o_intra = {(bb, h, sb): jnp.dot(jnp.where(incl, att[bb, h, sb], 0.0).astype(BF16), v_ref[bb, subs[sb], vs[h]],
                                    preferred_element_type=F32) for bb, h, sb in triples}
    for bb, h in pairs:
        st = state_ref[bb, h]
        inter = []
        for c in range(n_chunks):
            rows = slice(c * C, (c + 1) * C)
            inter.append(jnp.dot(q_d[bb][rows, ks[h]], st.astype(BF16), preferred_element_type=F32))
            dec = jnp.exp(b_last_t[bb][ks[h], c * C:c * C + 1])
            first = (c % per_sub) * GLA_DK
            st = st * dec + kv[bb, h, c // per_sub][first:first + GLA_DK, :]
        state_ref[bb, h] = st
        o_h = (jnp.concatenate([o_intra[bb, h, sb] for sb in range(n_subs)], axis=0)
               + jnp.concatenate(inter, axis=0))
        g = gate_ref[bb, :, vs[h]].astype(F32)
        o_ref[bb, :, vs[h]] = (_rms_rows(o_h, gain_ref[...]) * (g / (1.0 + jnp.exp(-g)))).astype(o_ref.dtype)


def _gla(z3, wa2p, ba, gain):
    b, s, _ = z3.shape
    tq = min(GLA_TILE, s)
    nb = GLA_BATCH if b % GLA_BATCH == 0 else 1

    def col(width, start):
        return pl.BlockSpec((nb, tq, width), lambda bi, i: (bi, i, start // width))

    return pl.pallas_call(
        _gla_kernel,
        grid=(b // nb, s // tq),
        in_specs=[col(A_QK, EV_AQ), col(A_QK, EV_AK), col(A_V, EV_AV), col(A_V, EV_AG),
                  col(LANES, EV_AR), _const_spec((LANES, A_QK)), _const_spec((1, A_QK)),
                  _const_spec((1, GLA_DV))],
        out_specs=pl.BlockSpec((nb, tq, A_V), lambda bi, i: (bi, i, 0)),
        out_shape=jax.ShapeDtypeStruct((b, s, A_V), BF16),
        scratch_shapes=[pltpu.VMEM((nb, GLA_HEADS, GLA_DK, GLA_DV), F32)],
        compiler_params=_params(("parallel", "arbitrary")),
        name="gla",
    )(z3, z3, z3, z3, z3, wa2p, ba, gain)


DIFF_TILE_Q = 2048
DIFF_TILE_K = 512
DIFF_ROWS_FULL = 512
DIFF_ROWS_DIAG = 256
DIFF_AHEAD = 2
LOG2E = math.log2(math.e)


def _diff_kernel(q_ref, k_ref, v_ref, lq1_ref, lk1_ref, lq2_ref, lk2_ref, gain_ref,
                 o_ref, qv_ref, m_ref, acc_ref, *, lambda_init, tk):
    qi = pl.program_id(2)
    tq = q_ref.shape[1]
    span = tq // tk

    q = q_ref[0]
    first = lax.broadcasted_iota(jnp.int32, q.shape, 1) < DIFF_DH
    zero = jnp.zeros_like(q)
    qv_ref[0:tq, :] = jnp.where(first, q, zero)
    qv_ref[tq:2 * tq, :] = jnp.where(first, zero, q)

    def process(blocks, rc, fresh=False):
        tasks = []
        for bi, (block, diag) in enumerate(blocks):
            rows = pl.ds(pl.multiple_of(block * tk, tk), tk)
            k = k_ref[0, rows, :]
            v1 = jnp.concatenate([v_ref[0, rows, :], jnp.ones((tk, LANES), BF16)], axis=1)
            for c in range(2 * tq // rc):
                kw = tk if diag is None else max(min((c * rc) % tq + rc - diag * tk, tk), 0)
                if kw > 0:
                    tasks.append((k, v1, c, kw, diag, fresh and bi == 0))

        def scores(task):
            k, _, c, kw, _, _ = task
            return lax.dot_general(qv_ref[c * rc:(c + 1) * rc, :], k[0:kw],
                                   (((1,), (1,)), ((), ())), preferred_element_type=F32)

        ready = [scores(t) for t in tasks[:DIFF_AHEAD]]
        for idx, (_, v1, c, kw, diag, start) in enumerate(tasks):
            rs = slice(c * rc, (c + 1) * rc)
            s = ready.pop(0)
            if idx + DIFF_AHEAD < len(tasks):
                ready.append(scores(tasks[idx + DIFF_AHEAD]))
            if diag is not None:
                r = lax.broadcasted_iota(jnp.int32, (rc, kw), 0) + ((c * rc) % tq - diag * tk)
                col = lax.broadcasted_iota(jnp.int32, (rc, kw), 1)
                s = jnp.where(col <= r, s, NEG_BIG)
            m_cur = jnp.max(s, axis=-1, keepdims=True)
            m_new = jnp.broadcast_to(m_cur, (rc, LANES)) if start else jnp.maximum(m_ref[rs, :], m_cur)
            p = jnp.exp2(s - jnp.tile(m_new, (1, kw // LANES)))
            pv = jnp.dot(p.astype(BF16), v1[0:kw], preferred_element_type=F32)
            if start:
                acc_ref[rs, :] = pv
            else:
                alpha = jnp.exp2(m_ref[rs, :] - m_new)
                acc_ref[rs, :] = acc_ref[rs, :] * jnp.tile(alpha, (1, 2)) + pv
            m_ref[rs, :] = m_new

    def below_diagonal(j, carry):
        process([(j * span + d, None) for d in range(span)], DIFF_ROWS_FULL)
        return carry

    process([(qi * span + d, d) for d in range(span)], DIFF_ROWS_DIAG, fresh=True)
    lax.fori_loop(0, qi, below_diagonal, 0)

    lam = (jnp.exp(jnp.sum(lq1_ref[...] * lk1_ref[...], axis=-1, keepdims=True))
           - jnp.exp(jnp.sum(lq2_ref[...] * lk2_ref[...], axis=-1, keepdims=True))
           + lambda_init)
    o1 = acc_ref[0:tq, 0:DIFF_DV] / acc_ref[0:tq, DIFF_DV:]
    o2 = acc_ref[tq:2 * tq, 0:DIFF_DV] / acc_ref[tq:2 * tq, DIFF_DV:]
    o = _rms_rows(o1 - lam * o2, gain_ref[...]) * (1.0 - lambda_init)
    o_ref[0] = o.astype(o_ref.dtype)


def _diff_attention(z3, lq1, lk1, lq2, lk2, gain, lambda_init):
    b, s, _ = z3.shape
    tq = min(DIFF_TILE_Q, s)
    tk = min(DIFF_TILE_K, s)
    qc, kc, vc = EV_BQ // LANES, EV_BK // LANES, EV_BV // LANES
    return pl.pallas_call(
        functools.partial(_diff_kernel, lambda_init=lambda_init, tk=tk),
        grid=(b, DIFF_HEADS, s // tq),
        in_specs=[pl.BlockSpec((1, tq, LANES), lambda bi, h, qi: (bi, qi, qc + h)),
                  pl.BlockSpec((1, s, LANES), lambda bi, h, qi: (bi, 0, kc + h)),
                  pl.BlockSpec((1, s, LANES), lambda bi, h, qi: (bi, 0, vc + h)),
                  _const_spec((1, DIFF_DH)), _const_spec((1, DIFF_DH)),
                  _const_spec((1, DIFF_DH)), _const_spec((1, DIFF_DH)),
                  _const_spec((1, DIFF_DV))],
        out_specs=pl.BlockSpec((1, tq, LANES), lambda bi, h, qi: (bi, qi, h)),
        out_shape=jax.ShapeDtypeStruct((b, s, B_V), BF16),
        scratch_shapes=[pltpu.VMEM((2 * tq, LANES), BF16), pltpu.VMEM((2 * tq, LANES), F32),
                        pltpu.VMEM((2 * tq, 2 * DIFF_DV), F32)],
        compiler_params=_params(("parallel", "parallel", "arbitrary")),
        name="diff_attention",
    )(z3, z3, z3, lq1, lk1, lq2, lk2, gain)


def _proj2_kernel(a_ref, b_ref, w_ref, o_ref):
    ka = a_ref.shape[1]
    o_ref[...] = (jnp.dot(a_ref[...], w_ref[0:ka, :], preferred_element_type=F32)
                  + jnp.dot(b_ref[...], w_ref[ka:, :], preferred_element_type=F32))


def _proj2(a, b, w):
    t, d = a.shape[0], w.shape[1]
    row = lambda width: pl.BlockSpec((ROW_TILE, width), lambda i: (i, 0))
    return pl.pallas_call(
        _proj2_kernel,
        grid=(t // ROW_TILE,),
        in_specs=[row(a.shape[1]), row(b.shape[1]), _const_spec(w.shape)],
        out_specs=row(d),
        out_shape=jax.ShapeDtypeStruct((t, d), F32),
        compiler_params=_params(("parallel",)),
        name="proj2",
    )(a, b, w)


DIL_WC = 128
DIL_STEP_ROWS = 2048
LSE_LANES = LANES // DIL_HEADS
GROUP_COLS = 3 * ODD_MIX


def _odd_proj_kernel(x_ref, g_ref, w_ref, hg_ref, *refs):
    out_refs, h_ref = refs[:-1], refs[-1]
    tm = x_ref.shape[0]
    h = _rms_rows(x_ref[...], g_ref[...])
    n_lane_tiles = h_ref.shape[0]
    for c in range(n_lane_tiles):
        h_ref[c] = h[:, c * LANES:(c + 1) * LANES]
    for gi, o_ref in enumerate(out_refs):
        d = o_ref.shape[1]
        n = tm // d
        if d == 1:
            hb = h.astype(BF16)
        else:
            hb = jnp.concatenate(
                [jnp.concatenate([h_ref[c, pl.ds(r, n, stride=d), :] for r in range(d)], axis=0)
                 for c in range(n_lane_tiles)], axis=1).astype(BF16)
        for c0 in range(0, GROUP_COLS, MXU_COLS):
            zw = jnp.dot(hb, w_ref[:, gi * GROUP_COLS + c0:gi * GROUP_COLS + c0 + MXU_COLS],
                         preferred_element_type=F32)
            for sub in range(MXU_COLS // LANES):
                col = c0 + sub * LANES
                cs = slice(gi * GROUP_COLS + col, gi * GROUP_COLS + col + LANES)
                z = zw[:, sub * LANES:(sub + 1) * LANES]
                if col < 2 * ODD_MIX:
                    ss = jnp.sum(z * z, axis=-1, keepdims=True)
                    z = z * lax.rsqrt(ss * (1.0 / LANES) + EPS) * hg_ref[:, cs]
                z = z.astype(o_ref.dtype)
                for r in range(d):
                    o_ref[0, r, :, col:col + LANES] = z[r * n:(r + 1) * n]


def _odd_proj(x2, g, w, hg, batch, seq):
    t, dm = x2.shape
    tiles = seq // ROW_TILE
    out_specs, out_shapes = [], []
    for _, d in DIL_PATTERNS:
        out_specs.append(pl.BlockSpec((1, d, ROW_TILE // d, GROUP_COLS),
                                      lambda i: (i // tiles, 0, i % tiles, 0)))
        out_shapes.append(jax.ShapeDtypeStruct((batch, d, seq // d, GROUP_COLS), BF16))
    return pl.pallas_call(
        _odd_proj_kernel,
        grid=(t // ROW_TILE,),
        in_specs=[pl.BlockSpec((ROW_TILE, dm), lambda i: (i, 0)),
                  _const_spec((1, dm)), _const_spec(w.shape), _const_spec(hg.shape)],
        out_specs=out_specs,
        out_shape=out_shapes,
        scratch_shapes=[pltpu.VMEM((dm // LANES, ROW_TILE, LANES), F32)],
        compiler_params=_params(("parallel",)),
        name="odd_proj",
    )(x2, g, w, hg)


def _dil_kernel(q_ref, kp_ref, kc_ref, vp_ref, vc_ref, o_ref, lse_ref):
    n = pl.program_id(2)
    rows = q_ref.shape[2]
    wc = DIL_WC
    i = lax.broadcasted_iota(jnp.int32, (wc, 2 * wc), 0)
    j = lax.broadcasted_iota(jnp.int32, (wc, 2 * wc), 1)
    dist = i + wc - j
    band = (dist >= 0) & (dist <= wc)
    band_first = band & ((j >= wc) | (n > 0))
    ones = jnp.ones((2 * wc, DIL_DH), BF16)
    lane_head = lax.broadcasted_iota(jnp.int32, (wc, LANES), 1) // LSE_LANES
    tasks = [(ci, jb, h) for ci in range(q_ref.shape[1]) for jb in range(rows // wc) for h in range(DIL_HEADS)]

    def band_of(ref_prev, ref_cur, ci, jb, hs):
        if jb == 0:
            return jnp.concatenate([ref_prev[0, ci, :, hs], ref_cur[0, ci, 0:wc, hs]], axis=0)
        return ref_cur[0, ci, (jb - 1) * wc:(jb + 1) * wc, hs]

    def scores(task):
        ci, jb, h = task
        hs = slice(h * DIL_DH, (h + 1) * DIL_DH)
        return lax.dot_general(q_ref[0, ci, jb * wc:(jb + 1) * wc, hs], band_of(kp_ref, kc_ref, ci, jb, hs),
                               (((1,), (1,)), ((), ())), preferred_element_type=F32)

    s_next = scores(tasks[0])
    lse = None
    for idx, (ci, jb, h) in enumerate(tasks):
        rs = slice(jb * wc, (jb + 1) * wc)
        hs = slice(h * DIL_DH, (h + 1) * DIL_DH)
        s = s_next
        s_next = scores(tasks[idx + 1]) if idx + 1 < len(tasks) else None
        s = jnp.where(band_first if jb == 0 else band, s, NEG_BIG)
        m = jnp.max(s, axis=-1, keepdims=True)
        p = jnp.exp2(s - m)
        pv = jnp.dot(p.astype(BF16), jnp.concatenate([band_of(vp_ref, vc_ref, ci, jb, hs), ones], axis=1),
                     preferred_element_type=F32)
        l = pv[:, DIL_DH:]
        o_ref[0, ci, rs, hs] = (pv[:, 0:DIL_DH] / l).astype(o_ref.dtype)
        lse_h = m * (1.0 / LOG2E) + jnp.log(l)
        lse = lse_h if h == 0 else jnp.where(lane_head == h, lse_h, lse)
        if h == DIL_HEADS - 1:
            lse_ref[0, ci, rs, :] = lse


def _dilated_group(zg):
    b, d, length, _ = zg.shape
    rows = min(length, DIL_STEP_ROWS)
    per_row = rows // DIL_WC
    cps = max(1, min(d, DIL_STEP_ROWS // rows))

    def cur(which):
        return pl.BlockSpec((1, cps, rows, ODD_MIX), lambda bi, r, n: (bi, r, n, which))

    def prev(which):
        return pl.BlockSpec((1, cps, DIL_WC, ODD_MIX),
                            lambda bi, r, n: (bi, r, jnp.maximum(n * per_row - 1, 0), which))

    out_spec = pl.BlockSpec((1, cps, rows, ODD_MIX), lambda bi, r, n: (bi, r, n, 0))
    lse_spec = pl.BlockSpec((1, cps, rows, LANES), lambda bi, r, n: (bi, r, n, 0))
    return pl.pallas_call(
        _dil_kernel,
        grid=(b, d // cps, length // rows),
        in_specs=[cur(0), prev(1), cur(1), prev(2), cur(2)],
        out_specs=[out_spec, lse_spec],
        out_shape=[jax.ShapeDtypeStruct((b, d, length, ODD_MIX), BF16),
                   jax.ShapeDtypeStruct((b, d, length, LANES), F32)],
        compiler_params=_params(("parallel", "parallel", "arbitrary")),
        name=f"dilated_d{d}",
    )(zg, zg, zg, zg, zg)


def _combine_proj_kernel(*refs):
    n_groups = len(DIL_PATTERNS)
    o_refs, l_refs = refs[:n_groups], refs[n_groups:2 * n_groups]
    w_ref, out_ref, stage_ref = refs[2 * n_groups:]
    tm = out_ref.shape[0]

    def natural(ref, slot):
        d = ref.shape[1]
        if d == 1:
            return ref[0, 0].astype(F32)
        n = tm // d
        n_lane_tiles = ref.shape[3] // LANES
        for r in range(d):
            blk = ref[0, r].astype(F32)
            for c in range(n_lane_tiles):
                stage_ref[slot, c, pl.ds(r, n, stride=d), :] = blk[:, c * LANES:(c + 1) * LANES]
        return jnp.concatenate([stage_ref[slot, c] for c in range(n_lane_tiles)], axis=1)

    outs = [natural(r, 2 * gi) for gi, r in enumerate(o_refs)]
    lses = [natural(r, 2 * gi + 1) for gi, r in enumerate(l_refs)]
    m = functools.reduce(jnp.maximum, lses)
    es = [jnp.exp(l - m) for l in lses]
    inv = 1.0 / sum(es)
    sel_row = lax.broadcasted_iota(jnp.int32, (LANES, ODD_MIX), 0)
    sel_col = lax.broadcasted_iota(jnp.int32, (LANES, ODD_MIX), 1)
    select = jnp.where(sel_row == (sel_col // DIL_DH) * LSE_LANES, 1.0, 0.0).astype(BF16)
    select3 = jnp.concatenate([select] * 3, axis=0)

    def spread(w):
        return jnp.dot(jnp.concatenate(_split3(w), axis=1), select3, preferred_element_type=F32)

    mix = outs[-1] + sum(spread(e * inv) * (o - outs[-1]) for e, o in zip(es[:-1], outs[:-1]))
    out_ref[...] = jnp.dot(mix.astype(BF16), w_ref[...], preferred_element_type=F32)


def _combine_proj(os_, lses, w, seq):
    t, dm = os_[0].shape[0] * seq, w.shape[1]
    tiles = seq // ROW_TILE
    row = pl.BlockSpec((ROW_TILE, dm), lambda i: (i, 0))
    cls = [pl.BlockSpec((1, a.shape[1], ROW_TILE // a.shape[1], a.shape[3]),
                        lambda i: (i // tiles, 0, i % tiles, 0)) for a in list(os_) + list(lses)]
    return pl.pallas_call(
        _combine_proj_kernel,
        grid=(t // ROW_TILE,),
        in_specs=cls + [_const_spec(w.shape)],
        out_specs=row,
        out_shape=jax.ShapeDtypeStruct((t, dm), F32),
        scratch_shapes=[pltpu.VMEM((2 * len(os_), ODD_MIX // LANES, ROW_TILE, LANES), F32)],
        compiler_params=_params(("parallel",)),
        name="combine_proj",
    )(*os_, *lses, w)


FFN_CHUNK = 256
FFN_TILE = 512
FFN_AHEAD = 1
FFN_DOWN_GROUP = 6
SUBLANES = 8
CARRY_ROWS = 2 * SUBLANES
STAGE_PAD_ROWS = 8


def _ffn_kernel(x_ref, upd_ref, g_ref, wup_ref, cw_ref, wdn_ref, o_ref, carry_ref, hs_ref, acc_ref,
                *, tiles_per_seq):
    tm, dm = x_ref.shape
    ng = tm // SUBLANES
    n_chunks = D_FF // FFN_CHUNK
    n_lane_tiles = dm // LANES

    h = _rms_rows(x_ref[...] + upd_ref[...], g_ref[...])
    pitch = hs_ref.shape[1] // SUBLANES
    for c in range(n_lane_tiles):
        for s in range(SUBLANES):
            hs_ref[c, s * pitch:s * pitch + ng, :] = h[s * ng:(s + 1) * ng, c * LANES:(c + 1) * LANES]
    hb = jnp.concatenate(
        [jnp.concatenate([hs_ref[c, pl.ds(j, SUBLANES, stride=pitch), :] for j in range(ng)], axis=0)
         for c in range(n_lane_tiles)], axis=1).astype(BF16)

    @pl.when(pl.program_id(0) % tiles_per_seq == 0)
    def _():
        carry_ref[...] = jnp.zeros_like(carry_ref)

    last_sublane = lax.broadcasted_iota(jnp.int32, (SUBLANES, FFN_CHUNK), 0) == SUBLANES - 1

    def cols_of(c):
        return (slice(c * FFN_CHUNK, (c + 1) * FFN_CHUNK),
                slice(D_FF + c * FFN_CHUNK, D_FF + (c + 1) * FFN_CHUNK))

    def up(c):
        return [jnp.dot(hb, wup_ref[:, cols], preferred_element_type=F32) for cols in cols_of(c)]

    def wrap(cur, prev):
        return pltpu.roll(jnp.where(last_sublane, prev, cur), 1, 0)

    def conv(u, cols):
        prev = carry_ref[:, cols]
        w2 = wrap(u[tm - 2 * SUBLANES:tm - SUBLANES], prev[0:SUBLANES])
        w1 = wrap(u[tm - SUBLANES:], prev[SUBLANES:])
        u1 = jnp.concatenate([w1, u[:tm - SUBLANES]], axis=0)
        u2 = jnp.concatenate([w2, w1, u[:tm - 2 * SUBLANES]], axis=0)
        carry_ref[:, cols] = u[tm - CARRY_ROWS:]
        cw = cw_ref[:, cols]
        return u2 * cw[0:1] + u1 * cw[1:2] + u * cw[2:3] + cw[3:4]

    ready = [up(c) for c in range(FFN_AHEAD)]
    pending = []
    for c in range(n_chunks):
        u_cur = ready.pop(0)
        if c + FFN_AHEAD < n_chunks:
            ready.append(up(c + FFN_AHEAD))
        gate, val = [conv(u, cols) for u, cols in zip(u_cur, cols_of(c))]
        pending.append((gate / (1.0 + jnp.exp(-gate)) * val).astype(BF16))
        if len(pending) == FFN_DOWN_GROUP or c + 1 == n_chunks:
            first = c + 1 - len(pending)
            act = pending[0] if len(pending) == 1 else jnp.concatenate(pending, axis=1)
            d = jnp.dot(act, wdn_ref[first * FFN_CHUNK:(c + 1) * FFN_CHUNK, :], preferred_element_type=F32)
            for lt in range(n_lane_tiles):
                piece = d[:, lt * LANES:(lt + 1) * LANES]
                acc_ref[lt] = piece if first == 0 else acc_ref[lt] + piece
            pending = []

    for lt in range(n_lane_tiles):
        cs = slice(lt * LANES, (lt + 1) * LANES)
        for s in range(SUBLANES):
            rs = slice(s * ng, (s + 1) * ng)
            o_ref[rs, cs] = x_ref[rs, cs] + upd_ref[rs, cs] + acc_ref[lt, pl.ds(s, ng, stride=SUBLANES), :]


def _ffn(x2, update, g, w_up, conv_tab, w_down, seq_len):
    t, d = x2.shape
    tile = min(FFN_TILE, seq_len)
    row = pl.BlockSpec((tile, d), lambda i: (i, 0))
    return pl.pallas_call(
        functools.partial(_ffn_kernel, tiles_per_seq=seq_len // tile),
        grid=(t // tile,),
        in_specs=[row, row, _const_spec((1, d)), _const_spec(w_up.shape), _const_spec(conv_tab.shape),
                  _const_spec(w_down.shape)],
        out_specs=row,
        out_shape=jax.ShapeDtypeStruct((t, d), F32),
        scratch_shapes=[pltpu.VMEM((CARRY_ROWS, 2 * D_FF), F32),
                        pltpu.VMEM((d // LANES, tile + STAGE_PAD_ROWS * SUBLANES, LANES), F32),
                        pltpu.VMEM((d // LANES, tile, LANES), F32)],
        compiler_params=_params(("arbitrary",)),
        name="conv_ffn",
    )(x2, update, g, w_up, conv_tab, w_down)


def _even_weights(w_in, w_a2, b_a, dq_gain, dk_gain):
    sizes = [A_QK, A_QK, A_V, A_V, GLA_RANK, B_QK, B_QK, B_V]
    offs = [0]
    for c in sizes:
        offs.append(offs[-1] + c)
    aq, ak, av, ag, ar, bq, bk, bv = [w_in[:, offs[i]:offs[i + 1]] for i in range(8)]
    pad = jnp.zeros((w_in.shape[0], LANES - GLA_RANK), w_in.dtype)
    w = jnp.concatenate([aq, ak, av, ag, bq, bk, bv, ar, pad], axis=1).astype(BF16)
    hg = jnp.ones((EV_COLS,), F32)
    hg = hg.at[EV_BQ:EV_BK].set(jnp.tile(dq_gain.astype(F32), B_QK // DIFF_DH) * (DIFF_DH ** -0.5 * LOG2E))
    hg = hg.at[EV_BK:EV_BV].set(jnp.tile(dk_gain.astype(F32), B_QK // DIFF_DH))
    norm_group = tuple(DIFF_DH if EV_BQ <= c * LANES < EV_BV else 0 for c in range(EV_COLS // LANES))
    wa2p = jnp.concatenate([w_a2, jnp.zeros((LANES - GLA_RANK, A_QK), w_a2.dtype)], axis=0).astype(BF16)
    return w, hg.reshape(1, EV_COLS), norm_group, wa2p, b_a.astype(F32).reshape(1, A_QK)


def _odd_gains(q_gain, k_gain):
    hg = jnp.ones((3, 3, ODD_MIX), F32)
    hg = hg.at[:, 0].set(jnp.tile(q_gain.astype(F32), DIL_HEADS) * (DIL_DH ** -0.5 * LOG2E))
    hg = hg.at[:, 1].set(jnp.tile(k_gain.astype(F32), DIL_HEADS))
    return hg.reshape(1, ODD_IN)


def kernel(x, norm_mix, norm_ffn, ev_w_in, ev_w_a2, ev_b_a, ev_gla_gain, ev_dq_gain, ev_dk_gain,
           ev_lq1, ev_lk1, ev_lq2, ev_lk2, ev_diff_gain, ev_w_out, od_w_in, od_q_gain, od_k_gain,
           od_w_out, ffn_w_up, ffn_conv_w, ffn_conv_b, ffn_w_down):
    b, s, d = x.shape
    depth = norm_mix.shape[0]
    x2 = x.reshape(b * s, d).astype(F32)
    row = lambda v: v.astype(F32).reshape(1, -1)
    for i in range(depth):
        g_mix = row(norm_mix[i])
        if i % 2 == 0:
            e = i // 2
            lambda_init = 0.8 - 0.6 * math.exp(-0.3 * i)
            w, hg, norm_group, wa2p, ba = _even_weights(ev_w_in[e], ev_w_a2[e], ev_b_a[e],
                                                        ev_dq_gain[e], ev_dk_gain[e])
            z3 = _norm_proj(x2, g_mix, w, hg, norm_group).reshape(b, s, EV_COLS)
            oa = _gla(z3, wa2p, ba, row(ev_gla_gain[e]))
            ob = _diff_attention(z3, row(ev_lq1[e]), row(ev_lk1[e]), row(ev_lq2[e]), row(ev_lk2[e]),
                                 row(ev_diff_gain[e]), lambda_init)
            update = _proj2(oa.reshape(b * s, A_V), ob.reshape(b * s, B_V), ev_w_out[e].astype(BF16))
        else:
            o = i // 2
            zs = _odd_proj(x2, g_mix, od_w_in[o].astype(BF16), _odd_gains(od_q_gain[o], od_k_gain[o]), b, s)
            outs = [_dilated_group(zg) for zg in zs]
            update = _combine_proj([o_ for o_, _ in outs], [l_ for _, l_ in outs], od_w_out[o].astype(BF16), s)
        conv_tab = jnp.concatenate([ffn_conv_w[i].reshape(3, 2 * D_FF), ffn_conv_b[i].reshape(1, 2 * D_FF),
                                    jnp.zeros((4, 2 * D_FF), F32)], axis=0).astype(F32)
        x2 = _ffn(x2, update, row(norm_ffn[i]), ffn_w_up[i].astype(BF16), conv_tab,
                  ffn_w_down[i].astype(BF16), s)
    return x2.reshape(b, s, d).astype(x.dtype)
```

```python
import functools
import math

import jax
import jax.numpy as jnp
from jax import lax
from jax.experimental import pallas as pl
from jax.experimental.pallas import tpu as pltpu

F32 = jnp.float32
BF16 = jnp.bfloat16

EPS = 1e-6
LANES = 128
MXU_COLS = 256
VMEM_LIMIT = 56 * 1024 * 1024
UPDATE_DTYPE = BF16
NEG_BIG = -1e30

D_MODEL = 1024
GLA_HEADS, GLA_DK, GLA_DV, GLA_RANK, GLA_TAU, GLA_CHUNK = 4, 64, 128, 16, 16.0, 64
DIFF_HEADS, DIFF_DH, DIFF_DV = 4, 64, 128
A_QK, A_V, B_QK, B_V = 256, 512, 512, 512
DIL_PATTERNS = ((128, 1), (512, 4), (2048, 16))
DIL_HEADS, DIL_DH = 4, 128
ODD_MIX = DIL_HEADS * DIL_DH
ODD_IN = 3 * 3 * ODD_MIX
D_FF = 2816

EV_AQ, EV_AK, EV_AV, EV_AG, EV_BQ, EV_BK, EV_BV, EV_AR = 0, 256, 512, 1024, 1536, 2048, 2560, 3072
EV_COLS = 3200

ROW_TILE = 1024


def _const_spec(shape):
    nd = len(shape)
    return pl.BlockSpec(shape, lambda *_: (0,) * nd)


def _params(sem):
    return pltpu.CompilerParams(dimension_semantics=sem, vmem_limit_bytes=VMEM_LIMIT)


def _rms_rows(x, g):
    return x * lax.rsqrt(jnp.mean(x * x, axis=-1, keepdims=True) + EPS) * g


def _norm_proj_kernel(x_ref, g_ref, w_ref, hg_ref, o_ref, *, norm_group):
    hb = _rms_rows(x_ref[...], g_ref[...]).astype(BF16)
    n_cols = o_ref.shape[1]
    for c0 in range(0, n_cols, MXU_COLS):
        width = min(MXU_COLS, n_cols - c0)
        zw = jnp.dot(hb, w_ref[:, c0:c0 + width], preferred_element_type=F32)
        for sub in range(width // LANES):
            cs = slice(c0 + sub * LANES, c0 + (sub + 1) * LANES)
            z = zw[:, sub * LANES:(sub + 1) * LANES]
            grp = norm_group[cs.start // LANES]
            if grp == LANES:
                ss = jnp.sum(z * z, axis=-1, keepdims=True)
                z = z * lax.rsqrt(ss * (1.0 / LANES) + EPS) * hg_ref[:, cs]
            elif grp == LANES // 2:
                lo = lax.broadcasted_iota(jnp.int32, z.shape, 1) < grp
                sq = z * z
                s_lo = jnp.sum(jnp.where(lo, sq, 0.0), axis=-1, keepdims=True)
                s_hi = jnp.sum(jnp.where(lo, 0.0, sq), axis=-1, keepdims=True)
                ss = jnp.where(lo, s_lo, s_hi)
                z = z * lax.rsqrt(ss * (1.0 / grp) + EPS) * hg_ref[:, cs]
            o_ref[:, cs] = z.astype(o_ref.dtype)


def _norm_proj(x2, g, w, hg, norm_group):
    t, d = x2.shape
    n = w.shape[1]
    return pl.pallas_call(
        functools.partial(_norm_proj_kernel, norm_group=norm_group),
        grid=(t // ROW_TILE,),
        in_specs=[pl.BlockSpec((ROW_TILE, d), lambda i: (i, 0)),
                  _const_spec((1, d)), _const_spec((d, n)), _const_spec((1, n))],
        out_specs=pl.BlockSpec((ROW_TILE, n), lambda i: (i, 0)),
        out_shape=jax.ShapeDtypeStruct((t, n), BF16),
        compiler_params=_params(("parallel",)),
        name="norm_proj",
    )(x2, g, w, hg)


GLA_TILE = 512
GLA_SUB = 256
GLA_BATCH = 2


def _split3(x):
    hi = x.astype(BF16)
    r1 = x - hi.astype(F32)
    mid = r1.astype(BF16)
    lo = (r1 - mid.astype(F32)).astype(BF16)
    return hi, mid, lo


def _gla_kernel(q_ref, k_ref, v_ref, gate_ref, ar_ref, wa2_ref, ba_ref, gain_ref, o_ref, state_ref):
    C = GLA_CHUNK
    tq = q_ref.shape[1]
    sub = min(GLA_SUB, tq)
    n_chunks, n_subs, per_sub = tq // C, tq // sub, sub // C
    shift = C.bit_length() - 1

    @pl.when(pl.program_id(1) == 0)
    def _():
        state_ref[...] = jnp.zeros_like(state_ref)

    row = lax.broadcasted_iota(jnp.int32, (sub, sub), 0)
    col = lax.broadcasted_iota(jnp.int32, (sub, sub), 1)
    same = (row >> shift) == (col >> shift)
    incl = same & (col <= row)
    t_incl = jnp.where(incl, 1.0, 0.0).astype(BF16)
    subs = [slice(sb * sub, (sb + 1) * sub) for sb in range(n_subs)]

    pairs = [(bb, h) for bb in range(q_ref.shape[0]) for h in range(GLA_HEADS)]
    triples = [(bb, h, sb) for bb, h in pairs for sb in range(n_subs)]
    ks = [slice(h * GLA_DK, (h + 1) * GLA_DK) for h in range(GLA_HEADS)]
    vs = [slice(h * GLA_DV, (h + 1) * GLA_DV) for h in range(GLA_HEADS)]

    q_d, k_d, k_tt, b_last_t = {}, {}, {}, {}
    for bb in range(q_ref.shape[0]):
        logits = jnp.dot(ar_ref[bb], wa2_ref[...], preferred_element_type=F32) + ba_ref[...]
        log_sig = jnp.minimum(logits, 0.0) - jnp.log(1.0 + jnp.exp(-jnp.abs(logits)))
        la = log_sig * (1.0 / GLA_TAU)
        parts = _split3(la)
        b = jnp.concatenate([sum(jnp.dot(t_incl, p[rs], preferred_element_type=F32) for p in parts)
                             for rs in subs], axis=0)
        b_last = jnp.concatenate([jnp.broadcast_to(b[(c + 1) * C - 1:(c + 1) * C, :], (C, b.shape[1]))
                                  for c in range(n_chunks)], axis=0)
        q_d[bb] = (q_ref[bb].astype(F32) * (GLA_DK ** -0.5) * jnp.exp(b)).astype(BF16)
        k_f = k_ref[bb].astype(F32)
        k_d[bb] = (k_f * jnp.exp(-b)).astype(BF16)
        k_tt[bb] = (k_f * jnp.exp(b_last - b)).T
        b_last_t[bb] = b_last.T

    att = {(bb, h, sb): lax.dot_general(q_d[bb][subs[sb], ks[h]], k_d[bb][subs[sb], ks[h]],
                                        (((1,), (1,)), ((), ())), preferred_element_type=F32)
           for bb, h, sb in triples}
    kv = {(bb, h, sb): jnp.dot(jnp.where(same, jnp.tile(k_tt[bb][ks[h], subs[sb]], (per_sub, 1)), 0.0).astype(BF16),
                               v_ref[bb, subs[sb], vs[h]], preferred_element_type=F32) for bb, h, sb in triples}
    o_intra = {(bb, h, sb): jnp.dot(jnp.where(incl, att[bb, h, sb], 0.0).astype(BF16), v_ref[bb, subs[sb], vs[h]],
                                    preferred_element_type=F32) for bb, h, sb in triples}
    for bb, h in pairs:
        st = state_ref[bb, h]
        inter = []
        for c in range(n_chunks):
            rows = slice(c * C, (c + 1) * C)
            inter.append(jnp.dot(q_d[bb][rows, ks[h]], st.astype(BF16), preferred_element_type=F32))
            dec = jnp.exp(b_last_t[bb][ks[h], c * C:c * C + 1])
            first = (c % per_sub) * GLA_DK
            st = st * dec + kv[bb, h, c // per_sub][first:first + GLA_DK, :]
        state_ref[bb, h] = st
        o_h = (jnp.concatenate([o_intra[bb, h, sb] for sb in range(n_subs)], axis=0)
               + jnp.concatenate(inter, axis=0))
        g = gate_ref[bb, :, vs[h]].astype(F32)
        o_ref[bb, :, vs[h]] = (_rms_rows(o_h, gain_ref[...]) * (g / (1.0 + jnp.exp(-g)))).astype(o_ref.dtype)


def _gla(z3, wa2p, ba, gain):
    b, s, _ = z3.shape
    tq = min(GLA_TILE, s)
    nb = GLA_BATCH if b % GLA_BATCH == 0 else 1

    def col(width, start):
        return pl.BlockSpec((nb, tq, width), lambda bi, i: (bi, i, start // width))

    return pl.pallas_call(
        _gla_kernel,
        grid=(b // nb, s // tq),
        in_specs=[col(A_QK, EV_AQ), col(A_QK, EV_AK), col(A_V, EV_AV), col(A_V, EV_AG),
                  col(LANES, EV_AR), _const_spec((LANES, A_QK)), _const_spec((1, A_QK)),
                  _const_spec((1, GLA_DV))],
        out_specs=pl.BlockSpec((nb, tq, A_V), lambda bi, i: (bi, i, 0)),
        out_shape=jax.ShapeDtypeStruct((b, s, A_V), BF16),
        scratch_shapes=[pltpu.VMEM((nb, GLA_HEADS, GLA_DK, GLA_DV), F32)],
        compiler_params=_params(("parallel", "arbitrary")),
        name="gla",
    )(z3, z3, z3, z3, z3, wa2p, ba, gain)


DIFF_TILE_Q = 2048
DIFF_TILE_K = 512
DIFF_ROWS_FULL = 512
DIFF_ROWS_DIAG = 256
DIFF_AHEAD = 2
LOG2E = math.log2(math.e)


def _diff_kernel(q_ref, k_ref, v_ref, lq1_ref, lk1_ref, lq2_ref, lk2_ref, gain_ref,
                 o_ref, qv_ref, m_ref, acc_ref, *, lambda_init, tk):
    qi = pl.program_id(2)
    tq = q_ref.shape[1]
    span = tq // tk

    q = q_ref[0]
    first = lax.broadcasted_iota(jnp.int32, q.shape, 1) < DIFF_DH
    zero = jnp.zeros_like(q)
    qv_ref[0:tq, :] = jnp.where(first, q, zero)
    qv_ref[tq:2 * tq, :] = jnp.where(first, zero, q)

    def process(blocks, rc, fresh=False):
        tasks = []
        for bi, (block, diag) in enumerate(blocks):
            rows = pl.ds(pl.multiple_of(block * tk, tk), tk)
            k = k_ref[0, rows, :]
            v1 = jnp.concatenate([v_ref[0, rows, :], jnp.ones((tk, LANES), BF16)], axis=1)
            for c in range(2 * tq // rc):
                kw = tk if diag is None else max(min((c * rc) % tq + rc - diag * tk, tk), 0)
                if kw > 0:
                    tasks.append((k, v1, c, kw, diag, fresh and bi == 0))

        def scores(task):
            k, _, c, kw, _, _ = task
            return lax.dot_general(qv_ref[c * rc:(c + 1) * rc, :], k[0:kw],
                                   (((1,), (1,)), ((), ())), preferred_element_type=F32)

        ready = [scores(t) for t in tasks[:DIFF_AHEAD]]
        for idx, (_, v1, c, kw, diag, start) in enumerate(tasks):
            rs = slice(c * rc, (c + 1) * rc)
            s = ready.pop(0)
            if idx + DIFF_AHEAD < len(tasks):
                ready.append(scores(tasks[idx + DIFF_AHEAD]))
            if diag is not None:
                r = lax.broadcasted_iota(jnp.int32, (rc, kw), 0) + ((c * rc) % tq - diag * tk)
                col = lax.broadcasted_iota(jnp.int32, (rc, kw), 1)
                s = jnp.where(col <= r, s, NEG_BIG)
            m_cur = jnp.max(s, axis=-1, keepdims=True)
            m_new = jnp.broadcast_to(m_cur, (rc, LANES)) if start else jnp.maximum(m_ref[rs, :], m_cur)
            p = jnp.exp2(s - jnp.tile(m_new, (1, kw // LANES)))
            pv = jnp.dot(p.astype(BF16), v1[0:kw], preferred_element_type=F32)
            if start:
                acc_ref[rs, :] = pv
            else:
                alpha = jnp.exp2(m_ref[rs, :] - m_new)
                acc_ref[rs, :] = acc_ref[rs, :] * jnp.tile(alpha, (1, 2)) + pv
            m_ref[rs, :] = m_new

    def below_diagonal(j, carry):
        process([(j * span + d, None) for d in range(span)], DIFF_ROWS_FULL)
        return carry

    process([(qi * span + d, d) for d in range(span)], DIFF_ROWS_DIAG, fresh=True)
    lax.fori_loop(0, qi, below_diagonal, 0)

    lam = (jnp.exp(jnp.sum(lq1_ref[...] * lk1_ref[...], axis=-1, keepdims=True))
           - jnp.exp(jnp.sum(lq2_ref[...] * lk2_ref[...], axis=-1, keepdims=True))
           + lambda_init)
    o1 = acc_ref[0:tq, 0:DIFF_DV] / acc_ref[0:tq, DIFF_DV:]
    o2 = acc_ref[tq:2 * tq, 0:DIFF_DV] / acc_ref[tq:2 * tq, DIFF_DV:]
    o = _rms_rows(o1 - lam * o2, gain_ref[...]) * (1.0 - lambda_init)
    o_ref[0] = o.astype(o_ref.dtype)


def _diff_attention(z3, lq1, lk1, lq2, lk2, gain, lambda_init):
    b, s, _ = z3.shape
    tq = min(DIFF_TILE_Q, s)
    tk = min(DIFF_TILE_K, s)
    qc, kc, vc = EV_BQ // LANES, EV_BK // LANES, EV_BV // LANES
    return pl.pallas_call(
        functools.partial(_diff_kernel, lambda_init=lambda_init, tk=tk),
        grid=(b, DIFF_HEADS, s // tq),
        in_specs=[pl.BlockSpec((1, tq, LANES), lambda bi, h, qi: (bi, qi, qc + h)),
                  pl.BlockSpec((1, s, LANES), lambda bi, h, qi: (bi, 0, kc + h)),
                  pl.BlockSpec((1, s, LANES), lambda bi, h, qi: (bi, 0, vc + h)),
                  _const_spec((1, DIFF_DH)), _const_spec((1, DIFF_DH)),
                  _const_spec((1, DIFF_DH)), _const_spec((1, DIFF_DH)),
                  _const_spec((1, DIFF_DV))],
        out_specs=pl.BlockSpec((1, tq, LANES), lambda bi, h, qi: (bi, qi, h)),
        out_shape=jax.ShapeDtypeStruct((b, s, B_V), BF16),
        scratch_shapes=[pltpu.VMEM((2 * tq, LANES), BF16), pltpu.VMEM((2 * tq, LANES), F32),
                        pltpu.VMEM((2 * tq, 2 * DIFF_DV), F32)],
        compiler_params=_params(("parallel", "parallel", "arbitrary")),
        name="diff_attention",
    )(z3, z3, z3, lq1, lk1, lq2, lk2, gain)


def _proj2_kernel(a_ref, b_ref, w_ref, o_ref):
    ka = a_ref.shape[1]
    o_ref[...] = (jnp.dot(a_ref[...], w_ref[0:ka, :], preferred_element_type=F32)
                  + jnp.dot(b_ref[...], w_ref[ka:, :], preferred_element_type=F32)).astype(o_ref.dtype)


def _proj2(a, b, w):
    t, d = a.shape[0], w.shape[1]
    row = lambda width: pl.BlockSpec((ROW_TILE, width), lambda i: (i, 0))
    return pl.pallas_call(
        _proj2_kernel,
        grid=(t // ROW_TILE,),
        in_specs=[row(a.shape[1]), row(b.shape[1]), _const_spec(w.shape)],
        out_specs=row(d),
        out_shape=jax.ShapeDtypeStruct((t, d), UPDATE_DTYPE),
        compiler_params=_params(("parallel",)),
        name="proj2",
    )(a, b, w)


DIL_WC = 128
DIL_STEP_ROWS = 2048
LSE_LANES = LANES // DIL_HEADS
GROUP_COLS = 3 * ODD_MIX


def _odd_proj_kernel(x_ref, g_ref, w_ref, hg_ref, *refs):
    out_refs, h_ref = refs[:-1], refs[-1]
    tm = x_ref.shape[0]
    h = _rms_rows(x_ref[...], g_ref[...])
    n_lane_tiles = h_ref.shape[0]
    for c in range(n_lane_tiles):
        h_ref[c] = h[:, c * LANES:(c + 1) * LANES]
    for gi, o_ref in enumerate(out_refs):
        d = o_ref.shape[1]
        n = tm // d
        if d == 1:
            hb = h.astype(BF16)
        else:
            hb = jnp.concatenate(
                [jnp.concatenate([h_ref[c, pl.ds(r, n, stride=d), :] for r in range(d)], axis=0)
                 for c in range(n_lane_tiles)], axis=1).astype(BF16)
        for c0 in range(0, GROUP_COLS, MXU_COLS):
            zw = jnp.dot(hb, w_ref[:, gi * GROUP_COLS + c0:gi * GROUP_COLS + c0 + MXU_COLS],
                         preferred_element_type=F32)
            for sub in range(MXU_COLS // LANES):
                col = c0 + sub * LANES
                cs = slice(gi * GROUP_COLS + col, gi * GROUP_COLS + col + LANES)
                z = zw[:, sub * LANES:(sub + 1) * LANES]
                if col < 2 * ODD_MIX:
                    ss = jnp.sum(z * z, axis=-1, keepdims=True)
                    z = z * lax.rsqrt(ss * (1.0 / LANES) + EPS) * hg_ref[:, cs]
                z = z.astype(o_ref.dtype)
                for r in range(d):
                    o_ref[0, r, :, col:col + LANES] = z[r * n:(r + 1) * n]


def _odd_proj(x2, g, w, hg, batch, seq):
    t, dm = x2.shape
    tiles = seq // ROW_TILE
    out_specs, out_shapes = [], []
    for _, d in DIL_PATTERNS:
        out_specs.append(pl.BlockSpec((1, d, ROW_TILE // d, GROUP_COLS),
                                      lambda i: (i // tiles, 0, i % tiles, 0)))
        out_shapes.append(jax.ShapeDtypeStruct((batch, d, seq // d, GROUP_COLS), BF16))
    return pl.pallas_call(
        _odd_proj_kernel,
        grid=(t // ROW_TILE,),
        in_specs=[pl.BlockSpec((ROW_TILE, dm), lambda i: (i, 0)),
                  _const_spec((1, dm)), _const_spec(w.shape), _const_spec(hg.shape)],
        out_specs=out_specs,
        out_shape=out_shapes,
        scratch_shapes=[pltpu.VMEM((dm // LANES, ROW_TILE, LANES), F32)],
        compiler_params=_params(("parallel",)),
        name="odd_proj",
    )(x2, g, w, hg)


def _dil_kernel(q_ref, kp_ref, kc_ref, vp_ref, vc_ref, o_ref, lse_ref):
    n = pl.program_id(2)
    rows = q_ref.shape[2]
    wc = DIL_WC
    i = lax.broadcasted_iota(jnp.int32, (wc, 2 * wc), 0)
    j = lax.broadcasted_iota(jnp.int32, (wc, 2 * wc), 1)
    dist = i + wc - j
    band = (dist >= 0) & (dist <= wc)
    band_first = band & ((j >= wc) | (n > 0))
    ones = jnp.ones((2 * wc, DIL_DH), BF16)
    lane_head = lax.broadcasted_iota(jnp.int32, (wc, LANES), 1) // LSE_LANES
    tasks = [(ci, jb, h) for ci in range(q_ref.shape[1]) for jb in range(rows // wc) for h in range(DIL_HEADS)]

    def band_of(ref_prev, ref_cur, ci, jb, hs):
        if jb == 0:
            return jnp.concatenate([ref_prev[0, ci, :, hs], ref_cur[0, ci, 0:wc, hs]], axis=0)
        return ref_cur[0, ci, (jb - 1) * wc:(jb + 1) * wc, hs]

    def scores(task):
        ci, jb, h = task
        hs = slice(h * DIL_DH, (h + 1) * DIL_DH)
        return lax.dot_general(q_ref[0, ci, jb * wc:(jb + 1) * wc, hs], band_of(kp_ref, kc_ref, ci, jb, hs),
                               (((1,), (1,)), ((), ())), preferred_element_type=F32)

    s_next = scores(tasks[0])
    lse = None
    for idx, (ci, jb, h) in enumerate(tasks):
        rs = slice(jb * wc, (jb + 1) * wc)
        hs = slice(h * DIL_DH, (h + 1) * DIL_DH)
        s = s_next
        s_next = scores(tasks[idx + 1]) if idx + 1 < len(tasks) else None
        s = jnp.where(band_first if jb == 0 else band, s, NEG_BIG)
        m = jnp.max(s, axis=-1, keepdims=True)
        p = jnp.exp2(s - m)
        pv = jnp.dot(p.astype(BF16), jnp.concatenate([band_of(vp_ref, vc_ref, ci, jb, hs), ones], axis=1),
                     preferred_element_type=F32)
        l = pv[:, DIL_DH:]
        o_ref[0, ci, rs, hs] = (pv[:, 0:DIL_DH] / l).astype(o_ref.dtype)
        lse_h = m * (1.0 / LOG2E) + jnp.log(l)
        lse = lse_h if h == 0 else jnp.where(lane_head == h, lse_h, lse)
        if h == DIL_HEADS - 1:
            lse_ref[0, ci, rs, :] = lse


def _dilated_group(zg):
    b, d, length, _ = zg.shape
    rows = min(length, DIL_STEP_ROWS)
    per_row = rows // DIL_WC
    cps = max(1, min(d, DIL_STEP_ROWS // rows))

    def cur(which):
        return pl.BlockSpec((1, cps, rows, ODD_MIX), lambda bi, r, n: (bi, r, n, which))

    def prev(which):
        return pl.BlockSpec((1, cps, DIL_WC, ODD_MIX),
                            lambda bi, r, n: (bi, r, jnp.maximum(n * per_row - 1, 0), which))

    out_spec = pl.BlockSpec((1, cps, rows, ODD_MIX), lambda bi, r, n: (bi, r, n, 0))
    lse_spec = pl.BlockSpec((1, cps, rows, LANES), lambda bi, r, n: (bi, r, n, 0))
    return pl.pallas_call(
        _dil_kernel,
        grid=(b, d // cps, length // rows),
        in_specs=[cur(0), prev(1), cur(1), prev(2), cur(2)],
        out_specs=[out_spec, lse_spec],
        out_shape=[jax.ShapeDtypeStruct((b, d, length, ODD_MIX), BF16),
                   jax.ShapeDtypeStruct((b, d, length, LANES), F32)],
        compiler_params=_params(("parallel", "parallel", "arbitrary")),
        name=f"dilated_d{d}",
    )(zg, zg, zg, zg, zg)


def _combine_proj_kernel(*refs):
    n_groups = len(DIL_PATTERNS)
    o_refs, l_refs = refs[:n_groups], refs[n_groups:2 * n_groups]
    w_ref, out_ref, stage_ref = refs[2 * n_groups:]
    tm = out_ref.shape[0]

    def natural(ref, slot):
        d = ref.shape[1]
        if d == 1:
            return ref[0, 0].astype(F32)
        n = tm // d
        n_lane_tiles = ref.shape[3] // LANES
        for r in range(d):
            blk = ref[0, r].astype(F32)
            for c in range(n_lane_tiles):
                stage_ref[slot, c, pl.ds(r, n, stride=d), :] = blk[:, c * LANES:(c + 1) * LANES]
        return jnp.concatenate([stage_ref[slot, c] for c in range(n_lane_tiles)], axis=1)

    outs = [natural(r, 2 * gi) for gi, r in enumerate(o_refs)]
    lses = [natural(r, 2 * gi + 1) for gi, r in enumerate(l_refs)]
    m = functools.reduce(jnp.maximum, lses)
    es = [jnp.exp(l - m) for l in lses]
    inv = 1.0 / sum(es)
    sel_row = lax.broadcasted_iota(jnp.int32, (LANES, ODD_MIX), 0)
    sel_col = lax.broadcasted_iota(jnp.int32, (LANES, ODD_MIX), 1)
    select = jnp.where(sel_row == (sel_col // DIL_DH) * LSE_LANES, 1.0, 0.0).astype(BF16)
    select3 = jnp.concatenate([select] * 3, axis=0)

    def spread(w):
        return jnp.dot(jnp.concatenate(_split3(w), axis=1), select3, preferred_element_type=F32)

    mix = outs[-1] + sum(spread(e * inv) * (o - outs[-1]) for e, o in zip(es[:-1], outs[:-1]))
    out_ref[...] = jnp.dot(mix.astype(BF16), w_ref[...], preferred_element_type=F32).astype(out_ref.dtype)


def _combine_proj(os_, lses, w, seq):
    t, dm = os_[0].shape[0] * seq, w.shape[1]
    tiles = seq // ROW_TILE
    row = pl.BlockSpec((ROW_TILE, dm), lambda i: (i, 0))
    cls = [pl.BlockSpec((1, a.shape[1], ROW_TILE // a.shape[1], a.shape[3]),
                        lambda i: (i // tiles, 0, i % tiles, 0)) for a in list(os_) + list(lses)]
    return pl.pallas_call(
        _combine_proj_kernel,
        grid=(t // ROW_TILE,),
        in_specs=cls + [_const_spec(w.shape)],
        out_specs=row,
        out_shape=jax.ShapeDtypeStruct((t, dm), UPDATE_DTYPE),
        scratch_shapes=[pltpu.VMEM((2 * len(os_), ODD_MIX // LANES, ROW_TILE, LANES), F32)],
        compiler_params=_params(("parallel",)),
        name="combine_proj",
    )(*os_, *lses, w)


FFN_CHUNK = 256
FFN_TILE = 1024
FFN_AHEAD = 1
FFN_DOWN_GROUP = 6
SUBLANES = 8
CARRY_ROWS = 2 * SUBLANES
STAGE_PAD_ROWS = 8


def _ffn_kernel(x_ref, upd_ref, g_ref, wup_ref, cw_ref, wdn_ref, o_ref, carry_ref, hs_ref, acc_ref,
                *, tiles_per_seq):
    tm, dm = x_ref.shape
    ng = tm // SUBLANES
    n_chunks = D_FF // FFN_CHUNK
    n_lane_tiles = dm // LANES

    h = _rms_rows(x_ref[...] + upd_ref[...].astype(F32), g_ref[...])
    pitch = hs_ref.shape[1] // SUBLANES
    for c in range(n_lane_tiles):
        for s in range(SUBLANES):
            hs_ref[c, s * pitch:s * pitch + ng, :] = h[s * ng:(s + 1) * ng, c * LANES:(c + 1) * LANES]
    hb = jnp.concatenate(
        [jnp.concatenate([hs_ref[c, pl.ds(j, SUBLANES, stride=pitch), :] for j in range(ng)], axis=0)
         for c in range(n_lane_tiles)], axis=1).astype(BF16)

    @pl.when(pl.program_id(0) % tiles_per_seq == 0)
    def _():
        carry_ref[...] = jnp.zeros_like(carry_ref)

    last_sublane = lax.broadcasted_iota(jnp.int32, (SUBLANES, FFN_CHUNK), 0) == SUBLANES - 1

    def cols_of(c):
        return (slice(c * FFN_CHUNK, (c + 1) * FFN_CHUNK),
                slice(D_FF + c * FFN_CHUNK, D_FF + (c + 1) * FFN_CHUNK))

    def up(c):
        return [jnp.dot(hb, wup_ref[:, cols], preferred_element_type=F32) for cols in cols_of(c)]

    def wrap(cur, prev):
        return pltpu.roll(jnp.where(last_sublane, prev, cur), 1, 0)

    def conv(u, cols):
        prev = carry_ref[:, cols]
        w2 = wrap(u[tm - 2 * SUBLANES:tm - SUBLANES], prev[0:SUBLANES])
        w1 = wrap(u[tm - SUBLANES:], prev[SUBLANES:])
        u1 = jnp.concatenate([w1, u[:tm - SUBLANES]], axis=0)
        u2 = jnp.concatenate([w2, w1, u[:tm - 2 * SUBLANES]], axis=0)
        carry_ref[:, cols] = u[tm - CARRY_ROWS:]
        cw = cw_ref[:, cols]
        return u2 * cw[0:1] + u1 * cw[1:2] + u * cw[2:3] + cw[3:4]

    ready = [up(c) for c in range(FFN_AHEAD)]
    pending = []
    for c in range(n_chunks):
        u_cur = ready.pop(0)
        if c + FFN_AHEAD < n_chunks:
            ready.append(up(c + FFN_AHEAD))
        gate, val = [conv(u, cols) for u, cols in zip(u_cur, cols_of(c))]
        pending.append((gate / (1.0 + jnp.exp(-gate)) * val).astype(BF16))
        if len(pending) == FFN_DOWN_GROUP or c + 1 == n_chunks:
            first = c + 1 - len(pending)
            act = pending[0] if len(pending) == 1 else jnp.concatenate(pending, axis=1)
            d = jnp.dot(act, wdn_ref[first * FFN_CHUNK:(c + 1) * FFN_CHUNK, :], preferred_element_type=F32)
            for lt in range(n_lane_tiles):
                piece = d[:, lt * LANES:(lt + 1) * LANES]
                acc_ref[lt] = piece if first == 0 else acc_ref[lt] + piece
            pending = []

    for lt in range(n_lane_tiles):
        cs = slice(lt * LANES, (lt + 1) * LANES)
        for s in range(SUBLANES):
            rs = slice(s * ng, (s + 1) * ng)
            o_ref[rs, cs] = (x_ref[rs, cs] + upd_ref[rs, cs].astype(F32)
                             + acc_ref[lt, pl.ds(s, ng, stride=SUBLANES), :])


def _ffn(x2, update, g, w_up, conv_tab, w_down, seq_len):
    t, d = x2.shape
    tile = min(FFN_TILE, seq_len)
    row = pl.BlockSpec((tile, d), lambda i: (i, 0))
    return pl.pallas_call(
        functools.partial(_ffn_kernel, tiles_per_seq=seq_len // tile),
        grid=(t // tile,),
        in_specs=[row, row, _const_spec((1, d)), _const_spec(w_up.shape), _const_spec(conv_tab.shape),
                  _const_spec(w_down.shape)],
        out_specs=row,
        out_shape=jax.ShapeDtypeStruct((t, d), F32),
        scratch_shapes=[pltpu.VMEM((CARRY_ROWS, 2 * D_FF), F32),
                        pltpu.VMEM((d // LANES, tile + STAGE_PAD_ROWS * SUBLANES, LANES), F32),
                        pltpu.VMEM((d // LANES, tile, LANES), F32)],
        compiler_params=_params(("arbitrary",)),
        name="conv_ffn",
    )(x2, update, g, w_up, conv_tab, w_down)


def _even_weights(w_in, w_a2, b_a, dq_gain, dk_gain):
    sizes = [A_QK, A_QK, A_V, A_V, GLA_RANK, B_QK, B_QK, B_V]
    offs = [0]
    for c in sizes:
        offs.append(offs[-1] + c)
    aq, ak, av, ag, ar, bq, bk, bv = [w_in[:, offs[i]:offs[i + 1]] for i in range(8)]
    pad = jnp.zeros((w_in.shape[0], LANES - GLA_RANK), w_in.dtype)
    w = jnp.concatenate([aq, ak, av, ag, bq, bk, bv, ar, pad], axis=1).astype(BF16)
    hg = jnp.ones((EV_COLS,), F32)
    hg = hg.at[EV_BQ:EV_BK].set(jnp.tile(dq_gain.astype(F32), B_QK // DIFF_DH) * (DIFF_DH ** -0.5 * LOG2E))
    hg = hg.at[EV_BK:EV_BV].set(jnp.tile(dk_gain.astype(F32), B_QK // DIFF_DH))
    norm_group = tuple(DIFF_DH if EV_BQ <= c * LANES < EV_BV else 0 for c in range(EV_COLS // LANES))
    wa2p = jnp.concatenate([w_a2, jnp.zeros((LANES - GLA_RANK, A_QK), w_a2.dtype)], axis=0).astype(BF16)
    return w, hg.reshape(1, EV_COLS), norm_group, wa2p, b_a.astype(F32).reshape(1, A_QK)


def _odd_gains(q_gain, k_gain):
    hg = jnp.ones((3, 3, ODD_MIX), F32)
    hg = hg.at[:, 0].set(jnp.tile(q_gain.astype(F32), DIL_HEADS) * (DIL_DH ** -0.5 * LOG2E))
    hg = hg.at[:, 1].set(jnp.tile(k_gain.astype(F32), DIL_HEADS))
    return hg.reshape(1, ODD_IN)


def kernel(x, norm_mix, norm_ffn, ev_w_in, ev_w_a2, ev_b_a, ev_gla_gain, ev_dq_gain, ev_dk_gain,
           ev_lq1, ev_lk1, ev_lq2, ev_lk2, ev_diff_gain, ev_w_out, od_w_in, od_q_gain, od_k_gain,
           od_w_out, ffn_w_up, ffn_conv_w, ffn_conv_b, ffn_w_down):
    b, s, d = x.shape
    depth = norm_mix.shape[0]
    x2 = x.reshape(b * s, d).astype(F32)
    row = lambda v: v.astype(F32).reshape(1, -1)
    for i in range(depth):
        g_mix = row(norm_mix[i])
        if i % 2 == 0:
            e = i // 2
            lambda_init = 0.8 - 0.6 * math.exp(-0.3 * i)
            w, hg, norm_group, wa2p, ba = _even_weights(ev_w_in[e], ev_w_a2[e], ev_b_a[e],
                                                        ev_dq_gain[e], ev_dk_gain[e])
            z3 = _norm_proj(x2, g_mix, w, hg, norm_group).reshape(b, s, EV_COLS)
            oa = _gla(z3, wa2p, ba, row(ev_gla_gain[e]))
            ob = _diff_attention(z3, row(ev_lq1[e]), row(ev_lk1[e]), row(ev_lq2[e]), row(ev_lk2[e]),
                                 row(ev_diff_gain[e]), lambda_init)
            update = _proj2(oa.reshape(b * s, A_V), ob.reshape(b * s, B_V), ev_w_out[e].astype(BF16))
        else:
            o = i // 2
            zs = _odd_proj(x2, g_mix, od_w_in[o].astype(BF16), _odd_gains(od_q_gain[o], od_k_gain[o]), b, s)
            outs = [_dilated_group(zg) for zg in zs]
            update = _combine_proj([o_ for o_, _ in outs], [l_ for _, l_ in outs], od_w_out[o].astype(BF16), s)
        conv_tab = jnp.concatenate([ffn_conv_w[i].reshape(3, 2 * D_FF), ffn_conv_b[i].reshape(1, 2 * D_FF),
                                    jnp.zeros((4, 2 * D_FF), F32)], axis=0).astype(F32)
        x2 = _ffn(x2, update, row(norm_ffn[i]), ffn_w_up[i].astype(BF16), conv_tab,
                  ffn_w_down[i].astype(BF16), s)
    return x2.reshape(b, s, d).astype(x.dtype)
```

```python
import functools
import math

import jax
import jax.numpy as jnp
from jax import lax
from jax.experimental import pallas as pl
from jax.experimental.pallas import tpu as pltpu

F32 = jnp.float32
BF16 = jnp.bfloat16

EPS = 1e-6
LANES = 128
MXU_COLS = 256
VMEM_LIMIT = 56 * 1024 * 1024
UPDATE_DTYPE = BF16
NEG_BIG = -1e30

D_MODEL = 1024
GLA_HEADS, GLA_DK, GLA_DV, GLA_RANK, GLA_TAU, GLA_CHUNK = 4, 64, 128, 16, 16.0, 64
DIFF_HEADS, DIFF_DH, DIFF_DV = 4, 64, 128
A_QK, A_V, B_QK, B_V = 256, 512, 512, 512
DIL_PATTERNS = ((128, 1), (512, 4), (2048, 16))
DIL_HEADS, DIL_DH = 4, 128
ODD_MIX = DIL_HEADS * DIL_DH
ODD_IN = 3 * 3 * ODD_MIX
D_FF = 2816

EV_AQ, EV_AK, EV_AV, EV_AG, EV_BQ, EV_BK, EV_BV, EV_AR = 0, 256, 512, 1024, 1536, 2048, 2560, 3072
EV_COLS = 3200

ROW_TILE = 1024


def _const_spec(shape):
    nd = len(shape)
    return pl.BlockSpec(shape, lambda *_: (0,) * nd)


def _params(sem):
    return pltpu.CompilerParams(dimension_semantics=sem, vmem_limit_bytes=VMEM_LIMIT)


def _rms_rows(x, g):
    return x * lax.rsqrt(jnp.mean(x * x, axis=-1, keepdims=True) + EPS) * g


def _norm_proj_kernel(x_ref, g_ref, w_ref, hg_ref, o_ref, *, norm_group):
    hb = _rms_rows(x_ref[...], g_ref[...]).astype(BF16)
    n_cols = o_ref.shape[1]
    for c0 in range(0, n_cols, MXU_COLS):
        width = min(MXU_COLS, n_cols - c0)
        zw = jnp.dot(hb, w_ref[:, c0:c0 + width], preferred_element_type=F32)
        for sub in range(width // LANES):
            cs = slice(c0 + sub * LANES, c0 + (sub + 1) * LANES)
            z = zw[:, sub * LANES:(sub + 1) * LANES]
            grp = norm_group[cs.start // LANES]
            if grp == LANES:
                ss = jnp.sum(z * z, axis=-1, keepdims=True)
                z = z * lax.rsqrt(ss * (1.0 / LANES) + EPS) * hg_ref[:, cs]
            elif grp == LANES // 2:
                lo = lax.broadcasted_iota(jnp.int32, z.shape, 1) < grp
                sq = z * z
                s_lo = jnp.sum(jnp.where(lo, sq, 0.0), axis=-1, keepdims=True)
                s_hi = jnp.sum(jnp.where(lo, 0.0, sq), axis=-1, keepdims=True)
                ss = jnp.where(lo, s_lo, s_hi)
                z = z * lax.rsqrt(ss * (1.0 / grp) + EPS) * hg_ref[:, cs]
            o_ref[:, cs] = z.astype(o_ref.dtype)


def _norm_proj(x2, g, w, hg, norm_group):
    t, d = x2.shape
    n = w.shape[1]
    return pl.pallas_call(
        functools.partial(_norm_proj_kernel, norm_group=norm_group),
        grid=(t // ROW_TILE,),
        in_specs=[pl.BlockSpec((ROW_TILE, d), lambda i: (i, 0)),
                  _const_spec((1, d)), _const_spec((d, n)), _const_spec((1, n))],
        out_specs=pl.BlockSpec((ROW_TILE, n), lambda i: (i, 0)),
        out_shape=jax.ShapeDtypeStruct((t, n), BF16),
        compiler_params=_params(("parallel",)),
        name="norm_proj",
    )(x2, g, w, hg)


GLA_TILE = 512
GLA_SUB = 256
GLA_BATCH = 2


def _split3(x):
    hi = x.astype(BF16)
    r1 = x - hi.astype(F32)
    mid = r1.astype(BF16)
    lo = (r1 - mid.astype(F32)).astype(BF16)
    return hi, mid, lo


def _gla_kernel(q_ref, k_ref, v_ref, gate_ref, ar_ref, wa2_ref, ba_ref, gain_ref, o_ref, state_ref):
    C = GLA_CHUNK
    tq = q_ref.shape[1]
    sub = min(GLA_SUB, tq)
    n_chunks, n_subs, per_sub = tq // C, tq // sub, sub // C
    shift = C.bit_length() - 1

    @pl.when(pl.program_id(1) == 0)
    def _():
        state_ref[...] = jnp.zeros_like(state_ref)

    row = lax.broadcasted_iota(jnp.int32, (sub, sub), 0)
    col = lax.broadcasted_iota(jnp.int32, (sub, sub), 1)
    same = (row >> shift) == (col >> shift)
    incl = same & (col <= row)
    t_incl = jnp.where(incl, 1.0, 0.0).astype(BF16)
    subs = [slice(sb * sub, (sb + 1) * sub) for sb in range(n_subs)]

    pairs = [(bb, h) for bb in range(q_ref.shape[0]) for h in range(GLA_HEADS)]
    triples = [(bb, h, sb) for bb, h in pairs for sb in range(n_subs)]
    ks = [slice(h * GLA_DK, (h + 1) * GLA_DK) for h in range(GLA_HEADS)]
    vs = [slice(h * GLA_DV, (h + 1) * GLA_DV) for h in range(GLA_HEADS)]

    q_d, k_d, k_tt, b_last_t = {}, {}, {}, {}
    for bb in range(q_ref.shape[0]):
        logits = jnp.dot(ar_ref[bb], wa2_ref[...], preferred_element_type=F32) + ba_ref[...]
        log_sig = jnp.minimum(logits, 0.0) - jnp.log(1.0 + jnp.exp(-jnp.abs(logits)))
        la = log_sig * (1.0 / GLA_TAU)
        parts = _split3(la)
        b = jnp.concatenate([sum(jnp.dot(t_incl, p[rs], preferred_element_type=F32) for p in parts)
                             for rs in subs], axis=0)
        b_last = jnp.concatenate([jnp.broadcast_to(b[(c + 1) * C - 1:(c + 1) * C, :], (C, b.shape[1]))
                                  for c in range(n_chunks)], axis=0)
        q_d[bb] = (q_ref[bb].astype(F32) * (GLA_DK ** -0.5) * jnp.exp(b)).astype(BF16)
        k_f = k_ref[bb].astype(F32)
        k_d[bb] = (k_f * jnp.exp(-b)).astype(BF16)
        k_tt[bb] = (k_f * jnp.exp(b_last - b)).T
        b_last_t[bb] = b_last.T

    att = {(bb, h, sb): lax.dot_general(q_d[bb][subs[sb], ks[h]], k_d[bb][subs[sb], ks[h]],
                                        (((1,), (1,)), ((), ())), preferred_element_type=F32)
           for bb, h, sb in triples}
    kv = {(bb, h, sb): jnp.dot(jnp.where(same, jnp.tile(k_tt[bb][ks[h], subs[sb]], (per_sub, 1)), 0.0).astype(BF16),
                               v_ref[bb, subs[sb], vs[h]], preferred_element_type=F32) for bb, h, sb in triples}
    o_intra = {(bb, h, sb): jnp.dot(jnp.where(incl, att[bb, h, sb], 0.0).astype(BF16), v_ref[bb, subs[sb], vs[h]],
                                    preferred_element_type=F32) for bb, h, sb in triples}
    for bb, h in pairs:
        st = state_ref[bb, h]
        inter = []
        for c in range(n_chunks):
            rows = slice(c * C, (c + 1) * C)
            inter.append(jnp.dot(q_d[bb][rows, ks[h]], st.astype(BF16), preferred_element_type=F32))
            dec = jnp.exp(b_last_t[bb][ks[h], c * C:c * C + 1])
            first = (c % per_sub) * GLA_DK
            st = st * dec + kv[bb, h, c // per_sub][first:first + GLA_DK, :]
        state_ref[bb, h] = st
        o_h = (jnp.concatenate([o_intra[bb, h, sb] for sb in range(n_subs)], axis=0)
               + jnp.concatenate(inter, axis=0))
        g = gate_ref[bb, :, vs[h]].astype(F32)
        o_ref[bb, :, vs[h]] = (_rms_rows(o_h, gain_ref[...]) * (g / (1.0 + jnp.exp(-g)))).astype(o_ref.dtype)


def _gla(z3, wa2p, ba, gain):
    b, s, _ = z3.shape
    tq = min(GLA_TILE, s)
    nb = GLA_BATCH if b % GLA_BATCH == 0 else 1

    def col(width, start):
        return pl.BlockSpec((nb, tq, width), lambda bi, i: (bi, i, start // width))

    return pl.pallas_call(
        _gla_kernel,
        grid=(b // nb, s // tq),
        in_specs=[col(A_QK, EV_AQ), col(A_QK, EV_AK), col(A_V, EV_AV), col(A_V, EV_AG),
                  col(LANES, EV_AR), _const_spec((LANES, A_QK)), _const_spec((1, A_QK)),
                  _const_spec((1, GLA_DV))],
        out_specs=pl.BlockSpec((nb, tq, A_V), lambda bi, i: (bi, i, 0)),
        out_shape=jax.ShapeDtypeStruct((b, s, A_V), BF16),
        scratch_shapes=[pltpu.VMEM((nb, GLA_HEADS, GLA_DK, GLA_DV), F32)],
        compiler_params=_params(("parallel", "arbitrary")),
        name="gla",
    )(z3, z3, z3, z3, z3, wa2p, ba, gain)


DIFF_TILE_Q = 4096
DIFF_TILE_K = 512
DIFF_ROWS_FULL = 512
DIFF_ROWS_DIAG = 256
DIFF_AHEAD = 2
LOG2E = math.log2(math.e)


def _diff_kernel(q_ref, k_ref, v_ref, lq1_ref, lk1_ref, lq2_ref, lk2_ref, gain_ref,
                 o_ref, qv_ref, m_ref, acc_ref, *, lambda_init, tk):
    qi = pl.program_id(2)
    tq = q_ref.shape[1]
    span = tq // tk

    q = q_ref[0]
    first = lax.broadcasted_iota(jnp.int32, q.shape, 1) < DIFF_DH
    zero = jnp.zeros_like(q)
    qv_ref[0:tq, :] = jnp.where(first, q, zero)
    qv_ref[tq:2 * tq, :] = jnp.where(first, zero, q)

    def process(blocks, rc, fresh=False):
        tasks = []
        for bi, (block, diag) in enumerate(blocks):
            rows = pl.ds(pl.multiple_of(block * tk, tk), tk)
            k = k_ref[0, rows, :]
            v1 = jnp.concatenate([v_ref[0, rows, :], jnp.ones((tk, LANES), BF16)], axis=1)
            for c in range(2 * tq // rc):
                kw = tk if diag is None else max(min((c * rc) % tq + rc - diag * tk, tk), 0)
                if kw > 0:
                    tasks.append((k, v1, c, kw, diag, fresh and bi == 0))

        def scores(task):
            k, _, c, kw, _, _ = task
            return lax.dot_general(qv_ref[c * rc:(c + 1) * rc, :], k[0:kw],
                                   (((1,), (1,)), ((), ())), preferred_element_type=F32)

        ready = [scores(t) for t in tasks[:DIFF_AHEAD]]
        for idx, (_, v1, c, kw, diag, start) in enumerate(tasks):
            rs = slice(c * rc, (c + 1) * rc)
            s = ready.pop(0)
            if idx + DIFF_AHEAD < len(tasks):
                ready.append(scores(tasks[idx + DIFF_AHEAD]))
            if diag is not None:
                r = lax.broadcasted_iota(jnp.int32, (rc, kw), 0) + ((c * rc) % tq - diag * tk)
                col = lax.broadcasted_iota(jnp.int32, (rc, kw), 1)
                s = jnp.where(col <= r, s, NEG_BIG)
            m_cur = jnp.max(s, axis=-1, keepdims=True)
            m_new = jnp.broadcast_to(m_cur, (rc, LANES)) if start else jnp.maximum(m_ref[rs, :], m_cur)
            p = jnp.exp2(s - jnp.tile(m_new, (1, kw // LANES)))
            pv = jnp.dot(p.astype(BF16), v1[0:kw], preferred_element_type=F32)
            if start:
                acc_ref[rs, :] = pv
            else:
                alpha = jnp.exp2(m_ref[rs, :] - m_new)
                acc_ref[rs, :] = acc_ref[rs, :] * jnp.tile(alpha, (1, 2)) + pv
            m_ref[rs, :] = m_new

    def below_diagonal(j, carry):
        process([(j * span + d, None) for d in range(span)], DIFF_ROWS_FULL)
        return carry

    process([(qi * span + d, d) for d in range(span)], DIFF_ROWS_DIAG, fresh=True)
    lax.fori_loop(0, qi, below_diagonal, 0)

    lam = (jnp.exp(jnp.sum(lq1_ref[...] * lk1_ref[...], axis=-1, keepdims=True))
           - jnp.exp(jnp.sum(lq2_ref[...] * lk2_ref[...], axis=-1, keepdims=True))
           + lambda_init)
    o1 = acc_ref[0:tq, 0:DIFF_DV] / acc_ref[0:tq, DIFF_DV:]
    o2 = acc_ref[tq:2 * tq, 0:DIFF_DV] / acc_ref[tq:2 * tq, DIFF_DV:]
    o = _rms_rows(o1 - lam * o2, gain_ref[...]) * (1.0 - lambda_init)
    o_ref[0] = o.astype(o_ref.dtype)


def _diff_attention(z3, lq1, lk1, lq2, lk2, gain, lambda_init):
    b, s, _ = z3.shape
    tq = min(DIFF_TILE_Q, s)
    tk = min(DIFF_TILE_K, s)
    qc, kc, vc = EV_BQ // LANES, EV_BK // LANES, EV_BV // LANES
    return pl.pallas_call(
        functools.partial(_diff_kernel, lambda_init=lambda_init, tk=tk),
        grid=(b, DIFF_HEADS, s // tq),
        in_specs=[pl.BlockSpec((1, tq, LANES), lambda bi, h, qi: (bi, qi, qc + h)),
                  pl.BlockSpec((1, s, LANES), lambda bi, h, qi: (bi, 0, kc + h)),
                  pl.BlockSpec((1, s, LANES), lambda bi, h, qi: (bi, 0, vc + h)),
                  _const_spec((1, DIFF_DH)), _const_spec((1, DIFF_DH)),
                  _const_spec((1, DIFF_DH)), _const_spec((1, DIFF_DH)),
                  _const_spec((1, DIFF_DV))],
        out_specs=pl.BlockSpec((1, tq, LANES), lambda bi, h, qi: (bi, qi, h)),
        out_shape=jax.ShapeDtypeStruct((b, s, B_V), BF16),
        scratch_shapes=[pltpu.VMEM((2 * tq, LANES), BF16), pltpu.VMEM((2 * tq, LANES), F32),
                        pltpu.VMEM((2 * tq, 2 * DIFF_DV), F32)],
        compiler_params=_params(("parallel", "parallel", "arbitrary")),
        name="diff_attention",
    )(z3, z3, z3, lq1, lk1, lq2, lk2, gain)


def _proj2_kernel(a_ref, b_ref, w_ref, o_ref):
    ka = a_ref.shape[1]
    o_ref[...] = (jnp.dot(a_ref[...], w_ref[0:ka, :], preferred_element_type=F32)
                  + jnp.dot(b_ref[...], w_ref[ka:, :], preferred_element_type=F32)).astype(o_ref.dtype)


def _proj2(a, b, w):
    t, d = a.shape[0], w.shape[1]
    row = lambda width: pl.BlockSpec((ROW_TILE, width), lambda i: (i, 0))
    return pl.pallas_call(
        _proj2_kernel,
        grid=(t // ROW_TILE,),
        in_specs=[row(a.shape[1]), row(b.shape[1]), _const_spec(w.shape)],
        out_specs=row(d),
        out_shape=jax.ShapeDtypeStruct((t, d), UPDATE_DTYPE),
        compiler_params=_params(("parallel",)),
        name="proj2",
    )(a, b, w)


DIL_WC = 128
DIL_STEP_ROWS = 2048
LSE_LANES = LANES // DIL_HEADS
GROUP_COLS = 3 * ODD_MIX


def _odd_proj_kernel(x_ref, g_ref, w_ref, hg_ref, *refs):
    out_refs, h_ref = refs[:-1], refs[-1]
    tm = x_ref.shape[0]
    h = _rms_rows(x_ref[...], g_ref[...])
    n_lane_tiles = h_ref.shape[0]
    for c in range(n_lane_tiles):
        h_ref[c] = h[:, c * LANES:(c + 1) * LANES]
    for gi, o_ref in enumerate(out_refs):
        d = o_ref.shape[1]
        n = tm // d
        if d == 1:
            hb = h.astype(BF16)
        else:
            hb = jnp.concatenate(
                [jnp.concatenate([h_ref[c, pl.ds(r, n, stride=d), :] for r in range(d)], axis=0)
                 for c in range(n_lane_tiles)], axis=1).astype(BF16)
        for c0 in range(0, GROUP_COLS, MXU_COLS):
            zw = jnp.dot(hb, w_ref[:, gi * GROUP_COLS + c0:gi * GROUP_COLS + c0 + MXU_COLS],
                         preferred_element_type=F32)
            for sub in range(MXU_COLS // LANES):
                col = c0 + sub * LANES
                cs = slice(gi * GROUP_COLS + col, gi * GROUP_COLS + col + LANES)
                z = zw[:, sub * LANES:(sub + 1) * LANES]
                if col < 2 * ODD_MIX:
                    ss = jnp.sum(z * z, axis=-1, keepdims=True)
                    z = z * lax.rsqrt(ss * (1.0 / LANES) + EPS) * hg_ref[:, cs]
                z = z.astype(o_ref.dtype)
                for r in range(d):
                    o_ref[0, r, :, col:col + LANES] = z[r * n:(r + 1) * n]


def _odd_proj(x2, g, w, hg, batch, seq):
    t, dm = x2.shape
    tiles = seq // ROW_TILE
    out_specs, out_shapes = [], []
    for _, d in DIL_PATTERNS:
        out_specs.append(pl.BlockSpec((1, d, ROW_TILE // d, GROUP_COLS),
                                      lambda i: (i // tiles, 0, i % tiles, 0)))
        out_shapes.append(jax.ShapeDtypeStruct((batch, d, seq // d, GROUP_COLS), BF16))
    return pl.pallas_call(
        _odd_proj_kernel,
        grid=(t // ROW_TILE,),
        in_specs=[pl.BlockSpec((ROW_TILE, dm), lambda i: (i, 0)),
                  _const_spec((1, dm)), _const_spec(w.shape), _const_spec(hg.shape)],
        out_specs=out_specs,
        out_shape=out_shapes,
        scratch_shapes=[pltpu.VMEM((dm // LANES, ROW_TILE, LANES), F32)],
        compiler_params=_params(("parallel",)),
        name="odd_proj",
    )(x2, g, w, hg)


def _dil_kernel(q_ref, kp_ref, kc_ref, vp_ref, vc_ref, o_ref, lse_ref):
    n = pl.program_id(2)
    rows = q_ref.shape[2]
    wc = DIL_WC
    i = lax.broadcasted_iota(jnp.int32, (wc, 2 * wc), 0)
    j = lax.broadcasted_iota(jnp.int32, (wc, 2 * wc), 1)
    dist = i + wc - j
    band = (dist >= 0) & (dist <= wc)
    band_first = band & ((j >= wc) | (n > 0))
    ones = jnp.ones((2 * wc, DIL_DH), BF16)
    lane_head = lax.broadcasted_iota(jnp.int32, (wc, LANES), 1) // LSE_LANES
    tasks = [(ci, jb, h) for ci in range(q_ref.shape[1]) for jb in range(rows // wc) for h in range(DIL_HEADS)]

    def band_of(ref_prev, ref_cur, ci, jb, hs):
        if jb == 0:
            return jnp.concatenate([ref_prev[0, ci, :, hs], ref_cur[0, ci, 0:wc, hs]], axis=0)
        return ref_cur[0, ci, (jb - 1) * wc:(jb + 1) * wc, hs]

    def scores(task):
        ci, jb, h = task
        hs = slice(h * DIL_DH, (h + 1) * DIL_DH)
        return lax.dot_general(q_ref[0, ci, jb * wc:(jb + 1) * wc, hs], band_of(kp_ref, kc_ref, ci, jb, hs),
                               (((1,), (1,)), ((), ())), preferred_element_type=F32)

    s_next = scores(tasks[0])
    lse = None
    for idx, (ci, jb, h) in enumerate(tasks):
        rs = slice(jb * wc, (jb + 1) * wc)
        hs = slice(h * DIL_DH, (h + 1) * DIL_DH)
        s = s_next
        s_next = scores(tasks[idx + 1]) if idx + 1 < len(tasks) else None
        s = jnp.where(band_first if jb == 0 else band, s, NEG_BIG)
        m = jnp.max(s, axis=-1, keepdims=True)
        p = jnp.exp2(s - m)
        pv = jnp.dot(p.astype(BF16), jnp.concatenate([band_of(vp_ref, vc_ref, ci, jb, hs), ones], axis=1),
                     preferred_element_type=F32)
        l = pv[:, DIL_DH:]
        o_ref[0, ci, rs, hs] = (pv[:, 0:DIL_DH] / l).astype(o_ref.dtype)
        lse_h = m * (1.0 / LOG2E) + jnp.log(l)
        lse = lse_h if h == 0 else jnp.where(lane_head == h, lse_h, lse)
        if h == DIL_HEADS - 1:
            lse_ref[0, ci, rs, :] = lse


def _dilated_group(zg):
    b, d, length, _ = zg.shape
    rows = min(length, DIL_STEP_ROWS)
    per_row = rows // DIL_WC
    cps = max(1, min(d, DIL_STEP_ROWS // rows))

    def cur(which):
        return pl.BlockSpec((1, cps, rows, ODD_MIX), lambda bi, r, n: (bi, r, n, which))

    def prev(which):
        return pl.BlockSpec((1, cps, DIL_WC, ODD_MIX),
                            lambda bi, r, n: (bi, r, jnp.maximum(n * per_row - 1, 0), which))

    out_spec = pl.BlockSpec((1, cps, rows, ODD_MIX), lambda bi, r, n: (bi, r, n, 0))
    lse_spec = pl.BlockSpec((1, cps, rows, LANES), lambda bi, r, n: (bi, r, n, 0))
    return pl.pallas_call(
        _dil_kernel,
        grid=(b, d // cps, length // rows),
        in_specs=[cur(0), prev(1), cur(1), prev(2), cur(2)],
        out_specs=[out_spec, lse_spec],
        out_shape=[jax.ShapeDtypeStruct((b, d, length, ODD_MIX), BF16),
                   jax.ShapeDtypeStruct((b, d, length, LANES), F32)],
        compiler_params=_params(("parallel", "parallel", "arbitrary")),
        name=f"dilated_d{d}",
    )(zg, zg, zg, zg, zg)


def _combine_proj_kernel(*refs):
    n_groups = len(DIL_PATTERNS)
    o_refs, l_refs = refs[:n_groups], refs[n_groups:2 * n_groups]
    w_ref, out_ref, stage_ref = refs[2 * n_groups:]
    tm = out_ref.shape[0]

    def natural(ref, slot):
        d = ref.shape[1]
        if d == 1:
            return ref[0, 0].astype(F32)
        n = tm // d
        n_lane_tiles = ref.shape[3] // LANES
        for r in range(d):
            blk = ref[0, r].astype(F32)
            for c in range(n_lane_tiles):
                stage_ref[slot, c, pl.ds(r, n, stride=d), :] = blk[:, c * LANES:(c + 1) * LANES]
        return jnp.concatenate([stage_ref[slot, c] for c in range(n_lane_tiles)], axis=1)

    outs = [natural(r, 2 * gi) for gi, r in enumerate(o_refs)]
    lses = [natural(r, 2 * gi + 1) for gi, r in enumerate(l_refs)]
    m = functools.reduce(jnp.maximum, lses)
    es = [jnp.exp(l - m) for l in lses]
    inv = 1.0 / sum(es)
    sel_row = lax.broadcasted_iota(jnp.int32, (LANES, ODD_MIX), 0)
    sel_col = lax.broadcasted_iota(jnp.int32, (LANES, ODD_MIX), 1)
    select = jnp.where(sel_row == (sel_col // DIL_DH) * LSE_LANES, 1.0, 0.0).astype(BF16)
    select3 = jnp.concatenate([select] * 3, axis=0)

    def spread(w):
        return jnp.dot(jnp.concatenate(_split3(w), axis=1), select3, preferred_element_type=F32)

    mix = outs[-1] + sum(spread(e * inv) * (o - outs[-1]) for e, o in zip(es[:-1], outs[:-1]))
    out_ref[...] = jnp.dot(mix.astype(BF16), w_ref[...], preferred_element_type=F32).astype(out_ref.dtype)


def _combine_proj(os_, lses, w, seq):
    t, dm = os_[0].shape[0] * seq, w.shape[1]
    tiles = seq // ROW_TILE
    row = pl.BlockSpec((ROW_TILE, dm), lambda i: (i, 0))
    cls = [pl.BlockSpec((1, a.shape[1], ROW_TILE // a.shape[1], a.shape[3]),
                        lambda i: (i // tiles, 0, i % tiles, 0)) for a in list(os_) + list(lses)]
    return pl.pallas_call(
        _combine_proj_kernel,
        grid=(t // ROW_TILE,),
        in_specs=cls + [_const_spec(w.shape)],
        out_specs=row,
        out_shape=jax.ShapeDtypeStruct((t, dm), UPDATE_DTYPE),
        scratch_shapes=[pltpu.VMEM((2 * len(os_), ODD_MIX // LANES, ROW_TILE, LANES), F32)],
        compiler_params=_params(("parallel",)),
        name="combine_proj",
    )(*os_, *lses, w)


FFN_CHUNK = 256
FFN_TILE = 1024
FFN_AHEAD = 1
FFN_DOWN_GROUP = 6
SUBLANES = 8
CARRY_ROWS = 2 * SUBLANES
STAGE_PAD_ROWS = 8


def _ffn_kernel(x_ref, upd_ref, g_ref, wup_ref, cw_ref, wdn_ref, o_ref, carry_ref, hs_ref, acc_ref,
                *, tiles_per_seq):
    tm, dm = x_ref.shape
    ng = tm // SUBLANES
    n_chunks = D_FF // FFN_CHUNK
    n_lane_tiles = dm // LANES

    h = _rms_rows(x_ref[...] + upd_ref[...].astype(F32), g_ref[...])
    pitch = hs_ref.shape[1] // SUBLANES
    for c in range(n_lane_tiles):
        for s in range(SUBLANES):
            hs_ref[c, s * pitch:s * pitch + ng, :] = h[s * ng:(s + 1) * ng, c * LANES:(c + 1) * LANES]
    hb = jnp.concatenate(
        [jnp.concatenate([hs_ref[c, pl.ds(j, SUBLANES, stride=pitch), :] for j in range(ng)], axis=0)
         for c in range(n_lane_tiles)], axis=1).astype(BF16)

    @pl.when(pl.program_id(0) % tiles_per_seq == 0)
    def _():
        carry_ref[...] = jnp.zeros_like(carry_ref)

    last_sublane = lax.broadcasted_iota(jnp.int32, (SUBLANES, FFN_CHUNK), 0) == SUBLANES - 1

    def cols_of(c):
        return (slice(c * FFN_CHUNK, (c + 1) * FFN_CHUNK),
                slice(D_FF + c * FFN_CHUNK, D_FF + (c + 1) * FFN_CHUNK))

    def up(c):
        return [jnp.dot(hb, wup_ref[:, cols], preferred_element_type=F32) for cols in cols_of(c)]

    def wrap(cur, prev):
        return pltpu.roll(jnp.where(last_sublane, prev, cur), 1, 0)

    def conv(u, cols):
        prev = carry_ref[:, cols]
        w2 = wrap(u[tm - 2 * SUBLANES:tm - SUBLANES], prev[0:SUBLANES])
        w1 = wrap(u[tm - SUBLANES:], prev[SUBLANES:])
        u1 = jnp.concatenate([w1, u[:tm - SUBLANES]], axis=0)
        u2 = jnp.concatenate([w2, w1, u[:tm - 2 * SUBLANES]], axis=0)
        carry_ref[:, cols] = u[tm - CARRY_ROWS:]
        cw = cw_ref[:, cols]
        return u2 * cw[0:1] + u1 * cw[1:2] + u * cw[2:3] + cw[3:4]

    ready = [up(c) for c in range(FFN_AHEAD)]
    pending = []
    for c in range(n_chunks):
        u_cur = ready.pop(0)
        if c + FFN_AHEAD < n_chunks:
            ready.append(up(c + FFN_AHEAD))
        gate, val = [conv(u, cols) for u, cols in zip(u_cur, cols_of(c))]
        pending.append((gate / (1.0 + jnp.exp(-gate)) * val).astype(BF16))
        if len(pending) == FFN_DOWN_GROUP or c + 1 == n_chunks:
            first = c + 1 - len(pending)
            act = pending[0] if len(pending) == 1 else jnp.concatenate(pending, axis=1)
            d = jnp.dot(act, wdn_ref[first * FFN_CHUNK:(c + 1) * FFN_CHUNK, :], preferred_element_type=F32)
            for lt in range(n_lane_tiles):
                piece = d[:, lt * LANES:(lt + 1) * LANES]
                acc_ref[lt] = piece if first == 0 else acc_ref[lt] + piece
            pending = []

    for lt in range(n_lane_tiles):
        cs = slice(lt * LANES, (lt + 1) * LANES)
        for s in range(SUBLANES):
            rs = slice(s * ng, (s + 1) * ng)
            o_ref[rs, cs] = (x_ref[rs, cs] + upd_ref[rs, cs].astype(F32)
                             + acc_ref[lt, pl.ds(s, ng, stride=SUBLANES), :])


def _ffn(x2, update, g, w_up, conv_tab, w_down, seq_len):
    t, d = x2.shape
    tile = min(FFN_TILE, seq_len)
    row = pl.BlockSpec((tile, d), lambda i: (i, 0))
    return pl.pallas_call(
        functools.partial(_ffn_kernel, tiles_per_seq=seq_len // tile),
        grid=(t // tile,),
        in_specs=[row, row, _const_spec((1, d)), _const_spec(w_up.shape), _const_spec(conv_tab.shape),
                  _const_spec(w_down.shape)],
        out_specs=row,
        out_shape=jax.ShapeDtypeStruct((t, d), F32),
        scratch_shapes=[pltpu.VMEM((CARRY_ROWS, 2 * D_FF), F32),
                        pltpu.VMEM((d // LANES, tile + STAGE_PAD_ROWS * SUBLANES, LANES), F32),
                        pltpu.VMEM((d // LANES, tile, LANES), F32)],
        compiler_params=_params(("arbitrary",)),
        name="conv_ffn",
    )(x2, update, g, w_up, conv_tab, w_down)


def _even_weights(w_in, w_a2, b_a, dq_gain, dk_gain):
    sizes = [A_QK, A_QK, A_V, A_V, GLA_RANK, B_QK, B_QK, B_V]
    offs = [0]
    for c in sizes:
        offs.append(offs[-1] + c)
    aq, ak, av, ag, ar, bq, bk, bv = [w_in[:, offs[i]:offs[i + 1]] for i in range(8)]
    pad = jnp.zeros((w_in.shape[0], LANES - GLA_RANK), w_in.dtype)
    w = jnp.concatenate([aq, ak, av, ag, bq, bk, bv, ar, pad], axis=1).astype(BF16)
    hg = jnp.ones((EV_COLS,), F32)
    hg = hg.at[EV_BQ:EV_BK].set(jnp.tile(dq_gain.astype(F32), B_QK // DIFF_DH) * (DIFF_DH ** -0.5 * LOG2E))
    hg = hg.at[EV_BK:EV_BV].set(jnp.tile(dk_gain.astype(F32), B_QK // DIFF_DH))
    norm_group = tuple(DIFF_DH if EV_BQ <= c * LANES < EV_BV else 0 for c in range(EV_COLS // LANES))
    wa2p = jnp.concatenate([w_a2, jnp.zeros((LANES - GLA_RANK, A_QK), w_a2.dtype)], axis=0).astype(BF16)
    return w, hg.reshape(1, EV_COLS), norm_group, wa2p, b_a.astype(F32).reshape(1, A_QK)


def _odd_gains(q_gain, k_gain):
    hg = jnp.ones((3, 3, ODD_MIX), F32)
    hg = hg.at[:, 0].set(jnp.tile(q_gain.astype(F32), DIL_HEADS) * (DIL_DH ** -0.5 * LOG2E))
    hg = hg.at[:, 1].set(jnp.tile(k_gain.astype(F32), DIL_HEADS))
    return hg.reshape(1, ODD_IN)


def kernel(x, norm_mix, norm_ffn, ev_w_in, ev_w_a2, ev_b_a, ev_gla_gain, ev_dq_gain, ev_dk_gain,
           ev_lq1, ev_lk1, ev_lq2, ev_lk2, ev_diff_gain, ev_w_out, od_w_in, od_q_gain, od_k_gain,
           od_w_out, ffn_w_up, ffn_conv_w, ffn_conv_b, ffn_w_down):
    b, s, d = x.shape
    depth = norm_mix.shape[0]
    x2 = x.reshape(b * s, d).astype(F32)
    row = lambda v: v.astype(F32).reshape(1, -1)
    for i in range(depth):
        g_mix = row(norm_mix[i])
        if i % 2 == 0:
            e = i // 2
            lambda_init = 0.8 - 0.6 * math.exp(-0.3 * i)
            w, hg, norm_group, wa2p, ba = _even_weights(ev_w_in[e], ev_w_a2[e], ev_b_a[e],
                                                        ev_dq_gain[e], ev_dk_gain[e])
            z3 = _norm_proj(x2, g_mix, w, hg, norm_group).reshape(b, s, EV_COLS)
            oa = _gla(z3, wa2p, ba, row(ev_gla_gain[e]))
            ob = _diff_attention(z3, row(ev_lq1[e]), row(ev_lk1[e]), row(ev_lq2[e]), row(ev_lk2[e]),
                                 row(ev_diff_gain[e]), lambda_init)
            update = _proj2(oa.reshape(b * s, A_V), ob.reshape(b * s, B_V), ev_w_out[e].astype(BF16))
        else:
            o = i // 2
            zs = _odd_proj(x2, g_mix, od_w_in[o].astype(BF16), _odd_gains(od_q_gain[o], od_k_gain[o]), b, s)
            outs = [_dilated_group(zg) for zg in zs]
            update = _combine_proj([o_ for o_, _ in outs], [l_ for _, l_ in outs], od_w_out[o].astype(BF16), s)
        conv_tab = jnp.concatenate([ffn_conv_w[i].reshape(3, 2 * D_FF), ffn_conv_b[i].reshape(1, 2 * D_FF),
                                    jnp.zeros((4, 2 * D_FF), F32)], axis=0).astype(F32)
        x2 = _ffn(x2, update, row(norm_ffn[i]), ffn_w_up[i].astype(BF16), conv_tab,
                  ffn_w_down[i].astype(BF16), s)
    return x2.reshape(b, s, d).astype(x.dtype)
```

```python
import functools
import math

import jax
import jax.numpy as jnp
from jax import lax
from jax.experimental import pallas as pl
from jax.experimental.pallas import tpu as pltpu

F32 = jnp.float32
BF16 = jnp.bfloat16

EPS = 1e-6
LANES = 128
MXU_COLS = 256
VMEM_LIMIT = 56 * 1024 * 1024
UPDATE_DTYPE = BF16
NEG_BIG = -1e30

D_MODEL = 1024
GLA_HEADS, GLA_DK, GLA_DV, GLA_RANK, GLA_TAU, GLA_CHUNK = 4, 64, 128, 16, 16.0, 64
DIFF_HEADS, DIFF_DH, DIFF_DV = 4, 64, 128
A_QK, A_V, B_QK, B_V = 256, 512, 512, 512
DIL_PATTERNS = ((128, 1), (512, 4), (2048, 16))
DIL_HEADS, DIL_DH = 4, 128
ODD_MIX = DIL_HEADS * DIL_DH
ODD_IN = 3 * 3 * ODD_MIX
D_FF = 2816

EV_AQ, EV_AK, EV_AV, EV_AG, EV_BQ, EV_BK, EV_BV, EV_AR = 0, 256, 512, 1024, 1536, 2048, 2560, 3072
EV_COLS = 3200

ROW_TILE = 1024


def _const_spec(shape):
    nd = len(shape)
    return pl.BlockSpec(shape, lambda *_: (0,) * nd)


def _params(sem):
    return pltpu.CompilerParams(dimension_semantics=sem, vmem_limit_bytes=VMEM_LIMIT)


def _rms_rows(x, g):
    return x * lax.rsqrt(jnp.mean(x * x, axis=-1, keepdims=True) + EPS) * g


def _norm_proj_kernel(x_ref, g_ref, w_ref, hg_ref, o_ref, *, norm_group):
    hb = _rms_rows(x_ref[...], g_ref[...]).astype(BF16)
    n_cols = o_ref.shape[1]
    for c0 in range(0, n_cols, MXU_COLS):
        width = min(MXU_COLS, n_cols - c0)
        zw = jnp.dot(hb, w_ref[:, c0:c0 + width], preferred_element_type=F32)
        for sub in range(width // LANES):
            cs = slice(c0 + sub * LANES, c0 + (sub + 1) * LANES)
            z = zw[:, sub * LANES:(sub + 1) * LANES]
            grp = norm_group[cs.start // LANES]
            if grp == LANES:
                ss = jnp.sum(z * z, axis=-1, keepdims=True)
                z = z * lax.rsqrt(ss * (1.0 / LANES) + EPS) * hg_ref[:, cs]
            elif grp == LANES // 2:
                lo = lax.broadcasted_iota(jnp.int32, z.shape, 1) < grp
                sq = z * z
                s_lo = jnp.sum(jnp.where(lo, sq, 0.0), axis=-1, keepdims=True)
                s_hi = jnp.sum(jnp.where(lo, 0.0, sq), axis=-1, keepdims=True)
                ss = jnp.where(lo, s_lo, s_hi)
                z = z * lax.rsqrt(ss * (1.0 / grp) + EPS) * hg_ref[:, cs]
            o_ref[:, cs] = z.astype(o_ref.dtype)


def _norm_proj(x2, g, w, hg, norm_group):
    t, d = x2.shape
    n = w.shape[1]
    return pl.pallas_call(
        functools.partial(_norm_proj_kernel, norm_group=norm_group),
        grid=(t // ROW_TILE,),
        in_specs=[pl.BlockSpec((ROW_TILE, d), lambda i: (i, 0)),
                  _const_spec((1, d)), _const_spec((d, n)), _const_spec((1, n))],
        out_specs=pl.BlockSpec((ROW_TILE, n), lambda i: (i, 0)),
        out_shape=jax.ShapeDtypeStruct((t, n), BF16),
        compiler_params=_params(("parallel",)),
        name="norm_proj",
    )(x2, g, w, hg)


GLA_TILE = 512
GLA_SUB = 256
GLA_BATCH = 2


def _split3(x):
    hi = x.astype(BF16)
    r1 = x - hi.astype(F32)
    mid = r1.astype(BF16)
    lo = (r1 - mid.astype(F32)).astype(BF16)
    return hi, mid, lo


def _gla_kernel(q_ref, k_ref, v_ref, gate_ref, ar_ref, wa2_ref, ba_ref, gain_ref, o_ref, state_ref):
    C = GLA_CHUNK
    tq = q_ref.shape[1]
    sub = min(GLA_SUB, tq)
    n_chunks, n_subs, per_sub = tq // C, tq // sub, sub // C
    shift = C.bit_length() - 1

    @pl.when(pl.program_id(1) == 0)
    def _():
        state_ref[...] = jnp.zeros_like(state_ref)

    row = lax.broadcasted_iota(jnp.int32, (sub, sub), 0)
    col = lax.broadcasted_iota(jnp.int32, (sub, sub), 1)
    same = (row >> shift) == (col >> shift)
    incl = same & (col <= row)
    t_incl = jnp.where(incl, 1.0, 0.0).astype(BF16)
    subs = [slice(sb * sub, (sb + 1) * sub) for sb in range(n_subs)]

    pairs = [(bb, h) for bb in range(q_ref.shape[0]) for h in range(GLA_HEADS)]
    triples = [(bb, h, sb) for bb, h in pairs for sb in range(n_subs)]
    ks = [slice(h * GLA_DK, (h + 1) * GLA_DK) for h in range(GLA_HEADS)]
    vs = [slice(h * GLA_DV, (h + 1) * GLA_DV) for h in range(GLA_HEADS)]

    q_d, k_d, k_tt, b_last_t = {}, {}, {}, {}
    for bb in range(q_ref.shape[0]):
        logits = jnp.dot(ar_ref[bb], wa2_ref[...], preferred_element_type=F32) + ba_ref[...]
        log_sig = jnp.minimum(logits, 0.0) - jnp.log(1.0 + jnp.exp(-jnp.abs(logits)))
        la = log_sig * (1.0 / GLA_TAU)
        parts = _split3(la)
        b = jnp.concatenate([sum(jnp.dot(t_incl, p[rs], preferred_element_type=F32) for p in parts)
                             for rs in subs], axis=0)
        b_last = jnp.concatenate([jnp.broadcast_to(b[(c + 1) * C - 1:(c + 1) * C, :], (C, b.shape[1]))
                                  for c in range(n_chunks)], axis=0)
        q_d[bb] = (q_ref[bb].astype(F32) * (GLA_DK ** -0.5) * jnp.exp(b)).astype(BF16)
        k_f = k_ref[bb].astype(F32)
        k_d[bb] = (k_f * jnp.exp(-b)).astype(BF16)
        k_tt[bb] = (k_f * jnp.exp(b_last - b)).T
        b_last_t[bb] = b_last.T

    att = {(bb, h, sb): lax.dot_general(q_d[bb][subs[sb], ks[h]], k_d[bb][subs[sb], ks[h]],
                                        (((1,), (1,)), ((), ())), preferred_element_type=F32)
           for bb, h, sb in triples}
    kv = {(bb, h, sb): jnp.dot(jnp.where(same, jnp.tile(k_tt[bb][ks[h], subs[sb]], (per_sub, 1)), 0.0).astype(BF16),
                               v_ref[bb, subs[sb], vs[h]], preferred_element_type=F32) for bb, h, sb in triples}
    o_intra = {(bb, h, sb): jnp.dot(jnp.where(incl, att[bb, h, sb], 0.0).astype(BF16), v_ref[bb, subs[sb], vs[h]],
                                    preferred_element_type=F32) for bb, h, sb in triples}
    for bb, h in pairs:
        st = state_ref[bb, h]
        inter = []
        for c in range(n_chunks):
            rows = slice(c * C, (c + 1) * C)
            inter.append(jnp.dot(q_d[bb][rows, ks[h]], st.astype(BF16), preferred_element_type=F32))
            dec = jnp.exp(b_last_t[bb][ks[h], c * C:c * C + 1])
            first = (c % per_sub) * GLA_DK
            st = st * dec + kv[bb, h, c // per_sub][first:first + GLA_DK, :]
        state_ref[bb, h] = st
        o_h = (jnp.concatenate([o_intra[bb, h, sb] for sb in range(n_subs)], axis=0)
               + jnp.concatenate(inter, axis=0))
        g = gate_ref[bb, :, vs[h]].astype(F32)
        o_ref[bb, :, vs[h]] = (_rms_rows(o_h, gain_ref[...]) * (g / (1.0 + jnp.exp(-g)))).astype(o_ref.dtype)


def _gla(z3, wa2p, ba, gain):
    b, s, _ = z3.shape
    tq = min(GLA_TILE, s)
    nb = GLA_BATCH if b % GLA_BATCH == 0 else 1

    def col(width, start):
        return pl.BlockSpec((nb, tq, width), lambda bi, i: (bi, i, start // width))

    return pl.pallas_call(
        _gla_kernel,
        grid=(b // nb, s // tq),
        in_specs=[col(A_QK, EV_AQ), col(A_QK, EV_AK), col(A_V, EV_AV), col(A_V, EV_AG),
                  col(LANES, EV_AR), _const_spec((LANES, A_QK)), _const_spec((1, A_QK)),
                  _const_spec((1, GLA_DV))],
        out_specs=pl.BlockSpec((nb, tq, A_V), lambda bi, i: (bi, i, 0)),
        out_shape=jax.ShapeDtypeStruct((b, s, A_V), BF16),
        scratch_shapes=[pltpu.VMEM((nb, GLA_HEADS, GLA_DK, GLA_DV), F32)],
        compiler_params=_params(("parallel", "arbitrary")),
        name="gla",
    )(z3, z3, z3, z3, z3, wa2p, ba, gain)


DIFF_TILE_Q = 4096
DIFF_TILE_K = 512
DIFF_ROWS_FULL = 512
DIFF_ROWS_DIAG = 256
DIFF_AHEAD = 2
DIFF_HEADS_PER_STEP = 2
LOG2E = math.log2(math.e)


def _diff_kernel(q_ref, k_ref, v_ref, lq1_ref, lk1_ref, lq2_ref, lk2_ref, gain_ref,
                 o_ref, qv_ref, m_all, acc_all, *, lambda_init, tk):
    qi = pl.program_id(2)
    tq = q_ref.shape[1]
    span = tq // tk

    heads = q_ref.shape[2] // LANES
    lanes = [slice(hh * LANES, (hh + 1) * LANES) for hh in range(heads)]
    for hh in range(heads):
        q = q_ref[0, :, lanes[hh]]
        first = lax.broadcasted_iota(jnp.int32, q.shape, 1) < DIFF_DH
        zero = jnp.zeros_like(q)
        qv_ref[hh, 0:tq, :] = jnp.where(first, q, zero)
        qv_ref[hh, tq:2 * tq, :] = jnp.where(first, zero, q)

    def process(blocks, rc, fresh=False):
        tasks = []
        for bi, (block, diag) in enumerate(blocks):
            rows = pl.ds(pl.multiple_of(block * tk, tk), tk)
            ks = [k_ref[0, rows, lanes[hh]] for hh in range(heads)]
            v1s = [jnp.concatenate([v_ref[0, rows, lanes[hh]], jnp.ones((tk, LANES), BF16)], axis=1)
                   for hh in range(heads)]
            for c in range(2 * tq // rc):
                kw = tk if diag is None else max(min((c * rc) % tq + rc - diag * tk, tk), 0)
                for hh in range(heads if kw > 0 else 0):
                    tasks.append((ks[hh], v1s[hh], c, kw, diag, fresh and bi == 0, hh))

        def scores(task):
            k, _, c, kw, _, _, hh = task
            return lax.dot_general(qv_ref[hh, c * rc:(c + 1) * rc, :], k[0:kw],
                                   (((1,), (1,)), ((), ())), preferred_element_type=F32)

        ready = [scores(t) for t in tasks[:DIFF_AHEAD]]
        for idx, (_, v1, c, kw, diag, start, hh) in enumerate(tasks):
            rs = slice(c * rc, (c + 1) * rc)
            m_ref, acc_ref = m_all.at[hh], acc_all.at[hh]
            s = ready.pop(0)
            if idx + DIFF_AHEAD < len(tasks):
                ready.append(scores(tasks[idx + DIFF_AHEAD]))
            if diag is not None:
                r = lax.broadcasted_iota(jnp.int32, (rc, kw), 0) + ((c * rc) % tq - diag * tk)
                col = lax.broadcasted_iota(jnp.int32, (rc, kw), 1)
                s = jnp.where(col <= r, s, NEG_BIG)
            m_cur = jnp.max(s, axis=-1, keepdims=True)
            m_new = jnp.broadcast_to(m_cur, (rc, LANES)) if start else jnp.maximum(m_ref[rs, :], m_cur)
            p = jnp.exp2(s - jnp.tile(m_new, (1, kw // LANES)))
            pv = jnp.dot(p.astype(BF16), v1[0:kw], preferred_element_type=F32)
            if start:
                acc_ref[rs, :] = pv
            else:
                alpha = jnp.exp2(m_ref[rs, :] - m_new)
                acc_ref[rs, :] = acc_ref[rs, :] * jnp.tile(alpha, (1, 2)) + pv
            m_ref[rs, :] = m_new

    def below_diagonal(j, carry):
        process([(j * span + d, None) for d in range(span)], DIFF_ROWS_FULL)
        return carry

    process([(qi * span + d, d) for d in range(span)], DIFF_ROWS_DIAG, fresh=True)
    lax.fori_loop(0, qi, below_diagonal, 0)

    lam = (jnp.exp(jnp.sum(lq1_ref[...] * lk1_ref[...], axis=-1, keepdims=True))
           - jnp.exp(jnp.sum(lq2_ref[...] * lk2_ref[...], axis=-1, keepdims=True))
           + lambda_init)
    for hh in range(heads):
        acc_ref = acc_all.at[hh]
        o1 = acc_ref[0:tq, 0:DIFF_DV] / acc_ref[0:tq, DIFF_DV:]
        o2 = acc_ref[tq:2 * tq, 0:DIFF_DV] / acc_ref[tq:2 * tq, DIFF_DV:]
        o = _rms_rows(o1 - lam * o2, gain_ref[...]) * (1.0 - lambda_init)
        o_ref[0, :, lanes[hh]] = o.astype(o_ref.dtype)


def _diff_attention(z3, lq1, lk1, lq2, lk2, gain, lambda_init):
    b, s, _ = z3.shape
    tq = min(DIFF_TILE_Q, s)
    tk = min(DIFF_TILE_K, s)
    hp = DIFF_HEADS_PER_STEP
    width = hp * LANES
    qc, kc, vc = EV_BQ // width, EV_BK // width, EV_BV // width
    return pl.pallas_call(
        functools.partial(_diff_kernel, lambda_init=lambda_init, tk=tk),
        grid=(b, DIFF_HEADS // hp, s // tq),
        in_specs=[pl.BlockSpec((1, tq, width), lambda bi, h, qi: (bi, qi, qc + h)),
                  pl.BlockSpec((1, s, width), lambda bi, h, qi: (bi, 0, kc + h)),
                  pl.BlockSpec((1, s, width), lambda bi, h, qi: (bi, 0, vc + h)),
                  _const_spec((1, DIFF_DH)), _const_spec((1, DIFF_DH)),
                  _const_spec((1, DIFF_DH)), _const_spec((1, DIFF_DH)),
                  _const_spec((1, DIFF_DV))],
        out_specs=pl.BlockSpec((1, tq, width), lambda bi, h, qi: (bi, qi, h)),
        out_shape=jax.ShapeDtypeStruct((b, s, B_V), BF16),
        scratch_shapes=[pltpu.VMEM((hp, 2 * tq, LANES), BF16), pltpu.VMEM((hp, 2 * tq, LANES), F32),
                        pltpu.VMEM((hp, 2 * tq, 2 * DIFF_DV), F32)],
        compiler_params=_params(("parallel", "parallel", "arbitrary")),
        name="diff_attention",
    )(z3, z3, z3, lq1, lk1, lq2, lk2, gain)


def _proj2_kernel(a_ref, b_ref, w_ref, o_ref):
    ka = a_ref.shape[1]
    o_ref[...] = (jnp.dot(a_ref[...], w_ref[0:ka, :], preferred_element_type=F32)
                  + jnp.dot(b_ref[...], w_ref[ka:, :], preferred_element_type=F32)).astype(o_ref.dtype)


def _proj2(a, b, w):
    t, d = a.shape[0], w.shape[1]
    row = lambda width: pl.BlockSpec((ROW_TILE, width), lambda i: (i, 0))
    return pl.pallas_call(
        _proj2_kernel,
        grid=(t // ROW_TILE,),
        in_specs=[row(a.shape[1]), row(b.shape[1]), _const_spec(w.shape)],
        out_specs=row(d),
        out_shape=jax.ShapeDtypeStruct((t, d), UPDATE_DTYPE),
        compiler_params=_params(("parallel",)),
        name="proj2",
    )(a, b, w)


DIL_WC = 128
DIL_STEP_ROWS = 2048
LSE_LANES = LANES // DIL_HEADS
GROUP_COLS = 3 * ODD_MIX


def _odd_proj_kernel(x_ref, g_ref, w_ref, hg_ref, *refs):
    out_refs, h_ref = refs[:-1], refs[-1]
    tm = x_ref.shape[0]
    h = _rms_rows(x_ref[...], g_ref[...])
    n_lane_tiles = h_ref.shape[0]
    for c in range(n_lane_tiles):
        h_ref[c] = h[:, c * LANES:(c + 1) * LANES]
    for gi, o_ref in enumerate(out_refs):
        d = o_ref.shape[1]
        n = tm // d
        if d == 1:
            hb = h.astype(BF16)
        else:
            hb = jnp.concatenate(
                [jnp.concatenate([h_ref[c, pl.ds(r, n, stride=d), :] for r in range(d)], axis=0)
                 for c in range(n_lane_tiles)], axis=1).astype(BF16)
        for c0 in range(0, GROUP_COLS, MXU_COLS):
            zw = jnp.dot(hb, w_ref[:, gi * GROUP_COLS + c0:gi * GROUP_COLS + c0 + MXU_COLS],
                         preferred_element_type=F32)
            for sub in range(MXU_COLS // LANES):
                col = c0 + sub * LANES
                cs = slice(gi * GROUP_COLS + col, gi * GROUP_COLS + col + LANES)
                z = zw[:, sub * LANES:(sub + 1) * LANES]
                if col < 2 * ODD_MIX:
                    ss = jnp.sum(z * z, axis=-1, keepdims=True)
                    z = z * lax.rsqrt(ss * (1.0 / LANES) + EPS) * hg_ref[:, cs]
                z = z.astype(o_ref.dtype)
                for r in range(d):
                    o_ref[0, r, :, col:col + LANES] = z[r * n:(r + 1) * n]


def _odd_proj(x2, g, w, hg, batch, seq):
    t, dm = x2.shape
    tiles = seq // ROW_TILE
    out_specs, out_shapes = [], []
    for _, d in DIL_PATTERNS:
        out_specs.append(pl.BlockSpec((1, d, ROW_TILE // d, GROUP_COLS),
                                      lambda i: (i // tiles, 0, i % tiles, 0)))
        out_shapes.append(jax.ShapeDtypeStruct((batch, d, seq // d, GROUP_COLS), BF16))
    return pl.pallas_call(
        _odd_proj_kernel,
        grid=(t // ROW_TILE,),
        in_specs=[pl.BlockSpec((ROW_TILE, dm), lambda i: (i, 0)),
                  _const_spec((1, dm)), _const_spec(w.shape), _const_spec(hg.shape)],
        out_specs=out_specs,
        out_shape=out_shapes,
        scratch_shapes=[pltpu.VMEM((dm // LANES, ROW_TILE, LANES), F32)],
        compiler_params=_params(("parallel",)),
        name="odd_proj",
    )(x2, g, w, hg)


def _dil_kernel(q_ref, kp_ref, kc_ref, vp_ref, vc_ref, o_ref, lse_ref):
    n = pl.program_id(2)
    rows = q_ref.shape[2]
    wc = DIL_WC
    i = lax.broadcasted_iota(jnp.int32, (wc, 2 * wc), 0)
    j = lax.broadcasted_iota(jnp.int32, (wc, 2 * wc), 1)
    dist = i + wc - j
    band = (dist >= 0) & (dist <= wc)
    band_first = band & ((j >= wc) | (n > 0))
    ones = jnp.ones((2 * wc, DIL_DH), BF16)
    lane_head = lax.broadcasted_iota(jnp.int32, (wc, LANES), 1) // LSE_LANES
    tasks = [(ci, jb, h) for ci in range(q_ref.shape[1]) for jb in range(rows // wc) for h in range(DIL_HEADS)]

    def band_of(ref_prev, ref_cur, ci, jb, hs):
        if jb == 0:
            return jnp.concatenate([ref_prev[0, ci, :, hs], ref_cur[0, ci, 0:wc, hs]], axis=0)
        return ref_cur[0, ci, (jb - 1) * wc:(jb + 1) * wc, hs]

    def scores(task):
        ci, jb, h = task
        hs = slice(h * DIL_DH, (h + 1) * DIL_DH)
        return lax.dot_general(q_ref[0, ci, jb * wc:(jb + 1) * wc, hs], band_of(kp_ref, kc_ref, ci, jb, hs),
                               (((1,), (1,)), ((), ())), preferred_element_type=F32)

    s_next = scores(tasks[0])
    lse = None
    for idx, (ci, jb, h) in enumerate(tasks):
        rs = slice(jb * wc, (jb + 1) * wc)
        hs = slice(h * DIL_DH, (h + 1) * DIL_DH)
        s = s_next
        s_next = scores(tasks[idx + 1]) if idx + 1 < len(tasks) else None
        s = jnp.where(band_first if jb == 0 else band, s, NEG_BIG)
        m = jnp.max(s, axis=-1, keepdims=True)
        p = jnp.exp2(s - m)
        pv = jnp.dot(p.astype(BF16), jnp.concatenate([band_of(vp_ref, vc_ref, ci, jb, hs), ones], axis=1),
                     preferred_element_type=F32)
        l = pv[:, DIL_DH:]
        o_ref[0, ci, rs, hs] = (pv[:, 0:DIL_DH] / l).astype(o_ref.dtype)
        lse_h = m * (1.0 / LOG2E) + jnp.log(l)
        lse = lse_h if h == 0 else jnp.where(lane_head == h, lse_h, lse)
        if h == DIL_HEADS - 1:
            lse_ref[0, ci, rs, :] = lse


def _dilated_group(zg):
    b, d, length, _ = zg.shape
    rows = min(length, DIL_STEP_ROWS)
    per_row = rows // DIL_WC
    cps = max(1, min(d, DIL_STEP_ROWS // rows))

    def cur(which):
        return pl.BlockSpec((1, cps, rows, ODD_MIX), lambda bi, r, n: (bi, r, n, which))

    def prev(which):
        return pl.BlockSpec((1, cps, DIL_WC, ODD_MIX),
                            lambda bi, r, n: (bi, r, jnp.maximum(n * per_row - 1, 0), which))

    out_spec = pl.BlockSpec((1, cps, rows, ODD_MIX), lambda bi, r, n: (bi, r, n, 0))
    lse_spec = pl.BlockSpec((1, cps, rows, LANES), lambda bi, r, n: (bi, r, n, 0))
    return pl.pallas_call(
        _dil_kernel,
        grid=(b, d // cps, length // rows),
        in_specs=[cur(0), prev(1), cur(1), prev(2), cur(2)],
        out_specs=[out_spec, lse_spec],
        out_shape=[jax.ShapeDtypeStruct((b, d, length, ODD_MIX), BF16),
                   jax.ShapeDtypeStruct((b, d, length, LANES), F32)],
        compiler_params=_params(("parallel", "parallel", "arbitrary")),
        name=f"dilated_d{d}",
    )(zg, zg, zg, zg, zg)


def _combine_proj_kernel(*refs):
    n_groups = len(DIL_PATTERNS)
    o_refs, l_refs = refs[:n_groups], refs[n_groups:2 * n_groups]
    w_ref, out_ref, stage_ref = refs[2 * n_groups:]
    tm = out_ref.shape[0]

    def natural(ref, slot):
        d = ref.shape[1]
        if d == 1:
            return ref[0, 0].astype(F32)
        n = tm // d
        n_lane_tiles = ref.shape[3] // LANES
        for r in range(d):
            blk = ref[0, r].astype(F32)
            for c in range(n_lane_tiles):
                stage_ref[slot, c, pl.ds(r, n, stride=d), :] = blk[:, c * LANES:(c + 1) * LANES]
        return jnp.concatenate([stage_ref[slot, c] for c in range(n_lane_tiles)], axis=1)

    outs = [natural(r, 2 * gi) for gi, r in enumerate(o_refs)]
    lses = [natural(r, 2 * gi + 1) for gi, r in enumerate(l_refs)]
    m = functools.reduce(jnp.maximum, lses)
    es = [jnp.exp(l - m) for l in lses]
    inv = 1.0 / sum(es)
    sel_row = lax.broadcasted_iota(jnp.int32, (LANES, ODD_MIX), 0)
    sel_col = lax.broadcasted_iota(jnp.int32, (LANES, ODD_MIX), 1)
    select = jnp.where(sel_row == (sel_col // DIL_DH) * LSE_LANES, 1.0, 0.0).astype(BF16)
    select3 = jnp.concatenate([select] * 3, axis=0)

    def spread(w):
        return jnp.dot(jnp.concatenate(_split3(w), axis=1), select3, preferred_element_type=F32)

    mix = outs[-1] + sum(spread(e * inv) * (o - outs[-1]) for e, o in zip(es[:-1], outs[:-1]))
    out_ref[...] = jnp.dot(mix.astype(BF16), w_ref[...], preferred_element_type=F32).astype(out_ref.dtype)


def _combine_proj(os_, lses, w, seq):
    t, dm = os_[0].shape[0] * seq, w.shape[1]
    tiles = seq // ROW_TILE
    row = pl.BlockSpec((ROW_TILE, dm), lambda i: (i, 0))
    cls = [pl.BlockSpec((1, a.shape[1], ROW_TILE // a.shape[1], a.shape[3]),
                        lambda i: (i // tiles, 0, i % tiles, 0)) for a in list(os_) + list(lses)]
    return pl.pallas_call(
        _combine_proj_kernel,
        grid=(t // ROW_TILE,),
        in_specs=cls + [_const_spec(w.shape)],
        out_specs=row,
        out_shape=jax.ShapeDtypeStruct((t, dm), UPDATE_DTYPE),
        scratch_shapes=[pltpu.VMEM((2 * len(os_), ODD_MIX // LANES, ROW_TILE, LANES), F32)],
        compiler_params=_params(("parallel",)),
        name="combine_proj",
    )(*os_, *lses, w)


FFN_CHUNK = 256
FFN_TILE = 1024
FFN_AHEAD = 1
FFN_DOWN_GROUP = 6
SUBLANES = 8
CARRY_ROWS = 2 * SUBLANES
STAGE_PAD_ROWS = 8


def _ffn_kernel(x_ref, upd_ref, g_ref, wup_ref, cw_ref, wdn_ref, o_ref, carry_ref, hs_ref, acc_ref,
                *, tiles_per_seq):
    tm, dm = x_ref.shape
    ng = tm // SUBLANES
    n_chunks = D_FF // FFN_CHUNK
    n_lane_tiles = dm // LANES

    h = _rms_rows(x_ref[...] + upd_ref[...].astype(F32), g_ref[...])
    pitch = hs_ref.shape[1] // SUBLANES
    for c in range(n_lane_tiles):
        for s in range(SUBLANES):
            hs_ref[c, s * pitch:s * pitch + ng, :] = h[s * ng:(s + 1) * ng, c * LANES:(c + 1) * LANES]
    hb = jnp.concatenate(
        [jnp.concatenate([hs_ref[c, pl.ds(j, SUBLANES, stride=pitch), :] for j in range(ng)], axis=0)
         for c in range(n_lane_tiles)], axis=1).astype(BF16)

    @pl.when(pl.program_id(0) % tiles_per_seq == 0)
    def _():
        carry_ref[...] = jnp.zeros_like(carry_ref)

    last_sublane = lax.broadcasted_iota(jnp.int32, (SUBLANES, FFN_CHUNK), 0) == SUBLANES - 1

    def cols_of(c):
        return (slice(c * FFN_CHUNK, (c + 1) * FFN_CHUNK),
                slice(D_FF + c * FFN_CHUNK, D_FF + (c + 1) * FFN_CHUNK))

    def up(c):
        return [jnp.dot(hb, wup_ref[:, cols], preferred_element_type=F32) for cols in cols_of(c)]

    def wrap(cur, prev):
        return pltpu.roll(jnp.where(last_sublane, prev, cur), 1, 0)

    def conv(u, cols):
        prev = carry_ref[:, cols]
        w2 = wrap(u[tm - 2 * SUBLANES:tm - SUBLANES], prev[0:SUBLANES])
        w1 = wrap(u[tm - SUBLANES:], prev[SUBLANES:])
        u1 = jnp.concatenate([w1, u[:tm - SUBLANES]], axis=0)
        u2 = jnp.concatenate([w2, w1, u[:tm - 2 * SUBLANES]], axis=0)
        carry_ref[:, cols] = u[tm - CARRY_ROWS:]
        cw = cw_ref[:, cols]
        return u2 * cw[0:1] + u1 * cw[1:2] + u * cw[2:3] + cw[3:4]

    ready = [up(c) for c in range(FFN_AHEAD)]
    pending = []
    for c in range(n_chunks):
        u_cur = ready.pop(0)
        if c + FFN_AHEAD < n_chunks:
            ready.append(up(c + FFN_AHEAD))
        gate, val = [conv(u, cols) for u, cols in zip(u_cur, cols_of(c))]
        pending.append((gate / (1.0 + jnp.exp(-gate)) * val).astype(BF16))
        if len(pending) == FFN_DOWN_GROUP or c + 1 == n_chunks:
            first = c + 1 - len(pending)
            act = pending[0] if len(pending) == 1 else jnp.concatenate(pending, axis=1)
            d = jnp.dot(act, wdn_ref[first * FFN_CHUNK:(c + 1) * FFN_CHUNK, :], preferred_element_type=F32)
            for lt in range(n_lane_tiles):
                piece = d[:, lt * LANES:(lt + 1) * LANES]
                acc_ref[lt] = piece if first == 0 else acc_ref[lt] + piece
            pending = []

    for lt in range(n_lane_tiles):
        cs = slice(lt * LANES, (lt + 1) * LANES)
        for s in range(SUBLANES):
            rs = slice(s * ng, (s + 1) * ng)
            o_ref[rs, cs] = (x_ref[rs, cs] + upd_ref[rs, cs].astype(F32)
                             + acc_ref[lt, pl.ds(s, ng, stride=SUBLANES), :])


def _ffn(x2, update, g, w_up, conv_tab, w_down, seq_len):
    t, d = x2.shape
    tile = min(FFN_TILE, seq_len)
    row = pl.BlockSpec((tile, d), lambda i: (i, 0))
    return pl.pallas_call(
        functools.partial(_ffn_kernel, tiles_per_seq=seq_len // tile),
        grid=(t // tile,),
        in_specs=[row, row, _const_spec((1, d)), _const_spec(w_up.shape), _const_spec(conv_tab.shape),
                  _const_spec(w_down.shape)],
        out_specs=row,
        out_shape=jax.ShapeDtypeStruct((t, d), F32),
        scratch_shapes=[pltpu.VMEM((CARRY_ROWS, 2 * D_FF), F32),
                        pltpu.VMEM((d // LANES, tile + STAGE_PAD_ROWS * SUBLANES, LANES), F32),
                        pltpu.VMEM((d // LANES, tile, LANES), F32)],
        compiler_params=_params(("arbitrary",)),
        name="conv_ffn",
    )(x2, update, g, w_up, conv_tab, w_down)


def _even_weights(w_in, w_a2, b_a, dq_gain, dk_gain):
    sizes = [A_QK, A_QK, A_V, A_V, GLA_RANK, B_QK, B_QK, B_V]
    offs = [0]
    for c in sizes:
        offs.append(offs[-1] + c)
    aq, ak, av, ag, ar, bq, bk, bv = [w_in[:, offs[i]:offs[i + 1]] for i in range(8)]
    pad = jnp.zeros((w_in.shape[0], LANES - GLA_RANK), w_in.dtype)
    w = jnp.concatenate([aq, ak, av, ag, bq, bk, bv, ar, pad], axis=1).astype(BF16)
    hg = jnp.ones((EV_COLS,), F32)
    hg = hg.at[EV_BQ:EV_BK].set(jnp.tile(dq_gain.astype(F32), B_QK // DIFF_DH) * (DIFF_DH ** -0.5 * LOG2E))
    hg = hg.at[EV_BK:EV_BV].set(jnp.tile(dk_gain.astype(F32), B_QK // DIFF_DH))
    norm_group = tuple(DIFF_DH if EV_BQ <= c * LANES < EV_BV else 0 for c in range(EV_COLS // LANES))
    wa2p = jnp.concatenate([w_a2, jnp.zeros((LANES - GLA_RANK, A_QK), w_a2.dtype)], axis=0).astype(BF16)
    return w, hg.reshape(1, EV_COLS), norm_group, wa2p, b_a.astype(F32).reshape(1, A_QK)


def _odd_gains(q_gain, k_gain):
    hg = jnp.ones((3, 3, ODD_MIX), F32)
    hg = hg.at[:, 0].set(jnp.tile(q_gain.astype(F32), DIL_HEADS) * (DIL_DH ** -0.5 * LOG2E))
    hg = hg.at[:, 1].set(jnp.tile(k_gain.astype(F32), DIL_HEADS))
    return hg.reshape(1, ODD_IN)


def kernel(x, norm_mix, norm_ffn, ev_w_in, ev_w_a2, ev_b_a, ev_gla_gain, ev_dq_gain, ev_dk_gain,
           ev_lq1, ev_lk1, ev_lq2, ev_lk2, ev_diff_gain, ev_w_out, od_w_in, od_q_gain, od_k_gain,
           od_w_out, ffn_w_up, ffn_conv_w, ffn_conv_b, ffn_w_down):
    b, s, d = x.shape
    depth = norm_mix.shape[0]
    x2 = x.reshape(b * s, d).astype(F32)
    row = lambda v: v.astype(F32).reshape(1, -1)
    for i in range(depth):
        g_mix = row(norm_mix[i])
        if i % 2 == 0:
            e = i // 2
            lambda_init = 0.8 - 0.6 * math.exp(-0.3 * i)
            w, hg, norm_group, wa2p, ba = _even_weights(ev_w_in[e], ev_w_a2[e], ev_b_a[e],
                                                        ev_dq_gain[e], ev_dk_gain[e])
            z3 = _norm_proj(x2, g_mix, w, hg, norm_group).reshape(b, s, EV_COLS)
            oa = _gla(z3, wa2p, ba, row(ev_gla_gain[e]))
            ob = _diff_attention(z3, row(ev_lq1[e]), row(ev_lk1[e]), row(ev_lq2[e]), row(ev_lk2[e]),
                                 row(ev_diff_gain[e]), lambda_init)
            update = _proj2(oa.reshape(b * s, A_V), ob.reshape(b * s, B_V), ev_w_out[e].astype(BF16))
        else:
            o = i // 2
            zs = _odd_proj(x2, g_mix, od_w_in[o].astype(BF16), _odd_gains(od_q_gain[o], od_k_gain[o]), b, s)
            outs = [_dilated_group(zg) for zg in zs]
            update = _combine_proj([o_ for o_, _ in outs], [l_ for _, l_ in outs], od_w_out[o].astype(BF16), s)
        conv_tab = jnp.concatenate([ffn_conv_w[i].reshape(3, 2 * D_FF), ffn_conv_b[i].reshape(1, 2 * D_FF),
                                    jnp.zeros((4, 2 * D_FF), F32)], axis=0).astype(F32)
        x2 = _ffn(x2, update, row(norm_ffn[i]), ffn_w_up[i].astype(BF16), conv_tab,
                  ffn_w_down[i].astype(BF16), s)
    return x2.reshape(b, s, d).astype(x.dtype)
```
